```python
import math
import jax, jax.numpy as jnp
from jax import lax
import numpy as np

D_MODEL = 2048
BATCH = 2
SEQ = 8192
DEPTH = 2

GRID_W = 64
CTX_LEN = 256
HEAD_DIM = 128
ROPE_PAIRS = HEAD_DIM // 4
ROPE_THETA = 10000.0
EPS = 1e-6
NEG_INF = -1e30
Q_BLOCK = 128
A_HEADS = 8
A_KV_HEADS = 2
WINDOW = 128
A_BLOCK = 128
B_HEADS = 4
B_VDIM = 2 * HEAD_DIM
C_HEADS = 8
C_KV_HEADS = 2
A_Q = A_HEADS * HEAD_DIM
A_KV = A_KV_HEADS * HEAD_DIM
B_QK = B_HEADS * 2 * HEAD_DIM
B_V = B_HEADS * B_VDIM
C_Q = C_HEADS * HEAD_DIM
C_KV = C_KV_HEADS * HEAD_DIM
BRANCH_WIDTH = 1024
IN_COLS = A_Q + 2 * A_KV + 2 * B_QK + B_V + C_Q + 2 * C_KV + 3 * D_MODEL
D_FF = 5632
N_EXPERTS = 8
TOP_K = 2
D_FF_EXPERT = 7168
MOE_BLOCK = 256
N_DENSE = (DEPTH + 1) // 2
N_MOE = DEPTH // 2

kernel_name = "hybrid_diffusion_gated_branch_trunk"

f32 = jnp.float32


def rmsnorm(x, g):
    xf = x.astype(f32)
    y = xf * lax.rsqrt(jnp.mean(xf * xf, axis=-1, keepdims=True) + EPS)
    return (y * g.astype(f32)).astype(x.dtype)


def axial_rope(n_tokens):
    rows = n_tokens // GRID_W
    row = jnp.repeat(jnp.arange(rows, dtype=jnp.int32), GRID_W)
    col = jnp.tile(jnp.arange(GRID_W, dtype=jnp.int32), rows)
    inv = ROPE_THETA ** (-jnp.arange(ROPE_PAIRS, dtype=f32) / ROPE_PAIRS)
    ang = jnp.stack([row.astype(f32)[:, None] * inv, col.astype(f32)[:, None] * inv], axis=1)
    return jnp.cos(ang), jnp.sin(ang)


def apply_rope(t, cos, sin):
    xr = t.astype(f32).reshape(t.shape[:-1] + (2, 2, ROPE_PAIRS))
    x1, x2 = xr[..., 0, :], xr[..., 1, :]
    c, s = cos[:, None], sin[:, None]
    out = jnp.stack([x1 * c - x2 * s, x1 * s + x2 * c], axis=-2)
    return out.reshape(t.shape).astype(t.dtype)


def heads(t, n):
    return t.reshape(t.shape[:-1] + (n, HEAD_DIM))


def project(h, w_in, qk_g, rope):
    sizes = [A_Q, A_KV, A_KV, B_QK, B_QK, B_V, C_Q, C_KV, C_KV, D_MODEL, D_MODEL, D_MODEL]
    idx = [int(v) for v in np.cumsum(sizes)[:-1]]
    aq, ak, av, bq, bk, bv, cq, ck, cv, ga, gb, gc = jnp.split(h @ w_in, idx, axis=-1)
    B_, T = h.shape[0], h.shape[1]
    qks = [rmsnorm(heads(aq, A_HEADS), qk_g[0]), rmsnorm(heads(ak, A_KV_HEADS), qk_g[1]),
           rmsnorm(heads(bq, 2 * B_HEADS), qk_g[2]), rmsnorm(heads(bk, 2 * B_HEADS), qk_g[3]),
           rmsnorm(heads(cq, C_HEADS), qk_g[4]), rmsnorm(heads(ck, C_KV_HEADS), qk_g[5])]
    if rope is not None:
        qks = [apply_rope(t, rope[0], rope[1]) for t in qks]
    aq, ak, bq, bk, cq, ck = qks
    bq = bq.reshape(B_, T, B_HEADS, 2, HEAD_DIM)
    bk = bk.reshape(B_, T, B_HEADS, 2, HEAD_DIM)
    bv = bv.reshape(B_, T, B_HEADS, B_VDIM)
    gates = (jax.nn.sigmoid(ga), jax.nn.sigmoid(gb), jax.nn.sigmoid(gc))
    return aq, ak, heads(av, A_KV_HEADS), bq, bk, bv, cq, ck, heads(cv, C_KV_HEADS), gates


def gqa(q, k, v, sink=None):
    B_, T, Hq, dh = q.shape
    Hkv = k.shape[2]
    G = Hq // Hkv
    qg = q.reshape(B_, T, Hkv, G, dh)
    s = jnp.einsum('btkgd,blkd->bkgtl', qg, k).astype(f32) * dh ** -0.5
    if sink is not None:
        s_sink = jnp.broadcast_to(sink.astype(f32).reshape(Hkv, G, 1, 1), s.shape[:-1] + (1,))
        s = jnp.concatenate([s, s_sink], axis=-1)
    p = jax.nn.softmax(s, axis=-1)
    if sink is not None:
        p = p[..., :-1]
    o = jnp.einsum('bkgtl,blkd->btkgd', p.astype(v.dtype), v)
    return o.reshape(B_, T, Hq * dh)


def window_gqa(q, k, v, kc, vc, sink):
    B_, S, Hq, dh = q.shape
    Hkv = k.shape[2]
    G = Hq // Hkv
    nb = S // A_BLOCK
    C = kc.shape[1]
    scale = dh ** -0.5
    qb = q.reshape(B_, nb, A_BLOCK, Hkv, G, dh)

    def band(t):
        tp = jnp.pad(t, ((0, 0), (A_BLOCK, A_BLOCK), (0, 0), (0, 0)))
        tp = tp.reshape(B_, nb + 2, A_BLOCK, Hkv, dh)
        return jnp.concatenate([tp[:, :-2], tp[:, 1:-1], tp[:, 2:]], axis=2)

    kb, vb = band(k), band(v)
    s_ctx = jnp.einsum('bnqkgd,bckd->bnkgqc', qb, kc).astype(f32) * scale
    s_loc = jnp.einsum('bnqkgd,bnskd->bnkgqs', qb, kb).astype(f32) * scale
    blk = jnp.arange(nb, dtype=jnp.int32)[:, None]
    qpos = blk * A_BLOCK + jnp.arange(A_BLOCK, dtype=jnp.int32)[None, :]
    kpos = (blk - 1) * A_BLOCK + jnp.arange(3 * A_BLOCK, dtype=jnp.int32)[None, :]
    valid = ((jnp.abs(qpos[:, :, None] - kpos[:, None, :]) <= WINDOW)
             & (kpos >= 0)[:, None, :] & (kpos < S)[:, None, :])
    s_loc = jnp.where(valid[None, :, None, None], s_loc, NEG_INF)
    s_sink = jnp.broadcast_to(sink.astype(f32).reshape(1, 1, Hkv, G, 1, 1), s_loc.shape[:-1] + (1,))
    p = jax.nn.softmax(jnp.concatenate([s_ctx, s_loc, s_sink], axis=-1), axis=-1)
    p_ctx = p[..., :C].astype(v.dtype)
    p_loc = p[..., C:C + 3 * A_BLOCK].astype(v.dtype)
    o = (jnp.einsum('bnkgqc,bckd->bnqkgd', p_ctx, vc)
         + jnp.einsum('bnkgqs,bnskd->bnqkgd', p_loc, vb))
    return o.reshape(B_, S, Hq * dh)


def diff_attn(q, k, v, lam):
    s = jnp.einsum('bthcd,blhcd->bhctl', q, k).astype(f32) * HEAD_DIM ** -0.5
    p = jax.nn.softmax(s, axis=-1)
    a = p[:, :, 0] - lam * p[:, :, 1]
    return jnp.einsum('bhtl,blhe->bthe', a.astype(v.dtype), v)


def over_query_blocks(fn, q):
    B_, S = q.shape[0], q.shape[1]
    nb = S // Q_BLOCK
    qb = jnp.moveaxis(q.reshape((B_, nb, Q_BLOCK) + q.shape[2:]), 1, 0)
    ob = jnp.moveaxis(lax.map(fn, qb), 0, 1)
    return ob.reshape((B_, S) + ob.shape[3:])


def diff_subln(o, g, lam_init):
    o = rmsnorm(o, g) * (1.0 - lam_init)
    return o.reshape(o.shape[:2] + (B_V,))


def merge(oa, ob, oc, gates, w_branch, w_out):
    ga, gb, gc = gates
    y = ga * (oa @ w_branch[0]) + gb * (ob @ w_branch[1]) + gc * (oc @ w_branch[2])
    return y @ w_out


def mixer(h, hc, w_in, qk_g, a_sink, b_lam, b_subln, w_branch, w_out, lam_init, with_ctx):
    rope = axial_rope(h.shape[1])
    aq, ak, av, bq, bk, bv, cq, ck, cv, gates = project(h, w_in, qk_g, rope)
    aqc, akc, avc, bqc, bkc, bvc, cqc, ckc, cvc, gates_c = project(hc, w_in, qk_g, None)
    lam = (jnp.exp(jnp.sum(b_lam[0].astype(f32) * b_lam[1].astype(f32)))
           - jnp.exp(jnp.sum(b_lam[2].astype(f32) * b_lam[3].astype(f32))) + lam_init)
    oa = window_gqa(aq, ak, av, akc, avc, a_sink)
    bk_all = jnp.concatenate([bkc, bk], axis=1)
    bv_all = jnp.concatenate([bvc, bv], axis=1)
    ob = diff_subln(over_query_blocks(lambda qb: diff_attn(qb, bk_all, bv_all, lam), bq), b_subln, lam_init)
    ck_all = jnp.concatenate([ckc, ck], axis=1)
    cv_all = jnp.concatenate([cvc, cv], axis=1)
    oc = over_query_blocks(lambda qb: gqa(qb, ck_all, cv_all), cq)
    y = merge(oa, ob, oc, gates, w_branch, w_out)
    if not with_ctx:
        return y, None
    oac = gqa(aqc, akc, avc, a_sink)
    obc = diff_subln(diff_attn(bqc, bkc, bvc, lam), b_subln, lam_init)
    occ = gqa(cqc, ckc, cvc)
    yc = merge(oac, obc, occ, gates_c, w_branch, w_out)
    return y, yc


def swiglu(t, w1, w3, w2):
    return (jax.nn.silu(t @ w1) * (t @ w3)) @ w2


def moe_swiglu(h, w_router, w1, w3, w2):
    shp = h.shape
    t = h.reshape(-1, shp[-1])
    n = t.shape[0]
    logits = (t @ w_router).astype(f32)
    top_v, top_i = lax.top_k(logits, TOP_K)
    wts = jax.nn.softmax(top_v, axis=-1).astype(t.dtype)
    flat_e = top_i.reshape(-1).astype(jnp.int32)
    n_slots = n * TOP_K
    order = jnp.argsort(flat_e)
    sorted_e = flat_e[order]
    counts = jnp.zeros((N_EXPERTS,), jnp.int32).at[flat_e].add(1)
    padded = (counts + MOE_BLOCK - 1) // MOE_BLOCK * MOE_BLOCK
    start = jnp.cumsum(counts) - counts
    pend = jnp.cumsum(padded)
    pstart = pend - padded
    rank = jnp.arange(n_slots, dtype=jnp.int32) - start[sorted_e]
    dest = jnp.zeros((n_slots,), jnp.int32).at[order].set(pstart[sorted_e] + rank)
    cap = -(-(n_slots + N_EXPERTS * MOE_BLOCK) // MOE_BLOCK) * MOE_BLOCK
    src = jnp.full((cap,), n, jnp.int32).at[dest].set(jnp.arange(n_slots, dtype=jnp.int32) // TOP_K)
    t_pad = jnp.concatenate([t, jnp.zeros((1, shp[-1]), t.dtype)], axis=0)
    n_blocks = cap // MOE_BLOCK
    xb = t_pad[src].reshape(n_blocks, MOE_BLOCK, shp[-1])
    blk_start = jnp.arange(n_blocks, dtype=jnp.int32) * MOE_BLOCK
    blk_e = jnp.minimum(jnp.searchsorted(pend, blk_start, side='right'), N_EXPERTS - 1)

    def expert_block(args):
        xe, e = args
        return swiglu(xe, w1[e], w3[e], w2[e])

    yb = lax.map(expert_block, (xb, blk_e)).reshape(cap, shp[-1])
    y = yb[dest].reshape(n, TOP_K, shp[-1])
    return jnp.einsum('nk,nkd->nd', wts, y).reshape(shp)


def setup_inputs(seed: int = 0) -> dict:
    key = jax.random.key(seed)
    ks = jax.random.split(key, 22)
    D = D_MODEL

    def nrm(k, shape, s):
        return jax.random.normal(k, shape, f32) * s

    return {
        'x': nrm(ks[0], (BATCH, SEQ, D), 1.0),
        'c': nrm(ks[1], (BATCH, D), 1.0),
        'ctx': nrm(ks[2], (BATCH, CTX_LEN, D), 1.0),
        'c_ctx': nrm(ks[3], (D,), 1.0),
        'w_mod': nrm(ks[4], (DEPTH, D, 6 * D), 0.5 * D ** -0.5),
        'b_mod': nrm(ks[5], (DEPTH, 6 * D), 0.02),
        'norm1': 1.0 + nrm(ks[6], (DEPTH, D), 0.1),
        'norm2': 1.0 + nrm(ks[7], (DEPTH, D), 0.1),
        'w_in': nrm(ks[8], (DEPTH, D, IN_COLS), D ** -0.5),
        'qk_gain': 1.0 + nrm(ks[9], (DEPTH, 6, HEAD_DIM), 0.1),
        'a_sink': nrm(ks[10], (DEPTH, A_HEADS), 1.0),
        'b_lambda': nrm(ks[11], (DEPTH, 4, HEAD_DIM), 0.1),
        'b_subln': 1.0 + nrm(ks[12], (DEPTH, B_VDIM), 0.1),
        'w_branch': nrm(ks[13], (DEPTH, 3, BRANCH_WIDTH, D), BRANCH_WIDTH ** -0.5),
        'w_out': nrm(ks[14], (DEPTH, D, D), D ** -0.5),
        'dense_w1': nrm(ks[15], (N_DENSE, D, D_FF), D ** -0.5),
        'dense_w3': nrm(ks[16], (N_DENSE, D, D_FF), D ** -0.5),
        'dense_w2': nrm(ks[17], (N_DENSE, D_FF, D), D_FF ** -0.5),
        'moe_router': nrm(ks[18], (N_MOE, D, N_EXPERTS), D ** -0.5),
        'moe_w1': nrm(ks[19], (N_MOE, N_EXPERTS, D, D_FF_EXPERT), D ** -0.5),
        'moe_w3': nrm(ks[20], (N_MOE, N_EXPERTS, D, D_FF_EXPERT), D ** -0.5),
        'moe_w2': nrm(ks[21], (N_MOE, N_EXPERTS, D_FF_EXPERT, D), D_FF_EXPERT ** -0.5),
    }


def channel_mixer(t, l, dense_w1, dense_w3, dense_w2, moe_router, moe_w1, moe_w3, moe_w2):
    if l % 2 == 0:
        i = l // 2
        return swiglu(t, dense_w1[i], dense_w3[i], dense_w2[i])
    i = l // 2
    return moe_swiglu(t, moe_router[i], moe_w1[i], moe_w3[i], moe_w2[i])


def reference(x, c, ctx, c_ctx, w_mod, b_mod, norm1, norm2, w_in, qk_gain, a_sink, b_lambda,
              b_subln, w_branch, w_out, dense_w1, dense_w3, dense_w2, moe_router, moe_w1,
              moe_w3, moe_w2):
    for l in range(DEPTH):
        with_ctx = l < DEPTH - 1
        lam_init = 0.8 - 0.6 * math.exp(-0.3 * l)
        m = (jax.nn.silu(c) @ w_mod[l] + b_mod[l])[:, None, :]
        mc = jax.nn.silu(c_ctx) @ w_mod[l] + b_mod[l]
        sh1, sc1, g1, sh2, sc2, g2 = jnp.split(m, 6, axis=-1)
        csh1, csc1, cg1, csh2, csc2, cg2 = jnp.split(mc, 6, axis=-1)
        h = rmsnorm(x, norm1[l]) * (1.0 + sc1) + sh1
        hc = rmsnorm(ctx, norm1[l]) * (1.0 + csc1) + csh1
        y, yc = mixer(h, hc, w_in[l], qk_gain[l], a_sink[l], b_lambda[l], b_subln[l],
                      w_branch[l], w_out[l], lam_init, with_ctx)
        x = x + g1 * y
        h2 = rmsnorm(x, norm2[l]) * (1.0 + sc2) + sh2
        x = x + g2 * channel_mixer(h2, l, dense_w1, dense_w3, dense_w2, moe_router, moe_w1, moe_w3, moe_w2)
        if with_ctx:
            ctx = ctx + cg1 * yc
            hc2 = rmsnorm(ctx, norm2[l]) * (1.0 + csc2) + csh2
            ctx = ctx + cg2 * channel_mixer(hc2, l, dense_w1, dense_w3, dense_w2, moe_router, moe_w1, moe_w3, moe_w2)
    return x
```

```python
import functools
import math

import jax
import jax.numpy as jnp
from jax import lax
from jax.experimental import pallas as pl
from jax.experimental.pallas import tpu as pltpu

f32 = jnp.float32
bf16 = jnp.bfloat16

HEAD_DIM = 128
GRID_W = 64
ROPE_PAIRS = HEAD_DIM // 4
ROPE_THETA = 10000.0
EPS = 1e-6
NEG_INF = -1e30
WINDOW = 128
A_HEADS, A_KV_HEADS = 8, 2
B_HEADS = 4
C_HEADS, C_KV_HEADS = 8, 2
GQA_GROUP = 4
TOP_K = 2
LANES = 128
MIB = 1024 * 1024

A_Q, A_KV = A_HEADS * HEAD_DIM, A_KV_HEADS * HEAD_DIM
B_QK, B_V = B_HEADS * 2 * HEAD_DIM, B_HEADS * 2 * HEAD_DIM
C_Q, C_KV = C_HEADS * HEAD_DIM, C_KV_HEADS * HEAD_DIM
QK_CQ, QK_AQ, QK_BQ, QK_BK = 0, C_Q, C_Q + A_Q, C_Q + A_Q + B_QK
QK_CK = QK_BK + B_QK
QK_AK = QK_CK + C_KV
QK_COLS = QK_AK + A_KV
V_BV, V_CV, V_AV = 0, B_V, B_V + C_KV
V_COLS = V_AV + A_KV


def _cparams(sem, vmem_mib):
    return pltpu.CompilerParams(dimension_semantics=sem, vmem_limit_bytes=vmem_mib * MIB)


def _dot(a, b):
    return jnp.dot(a, b, preferred_element_type=f32)


def _dot_nt(a, b):
    return lax.dot_general(a, b, (((1,), (1,)), ((), ())), preferred_element_type=f32)


def _sigmoid(x):
    return 1.0 / (1.0 + jnp.exp(-x))


def _mod_kernel(c_ref, w_ref, b_ref, o_ref):
    c = c_ref[...]
    a = (c * _sigmoid(c)).astype(bf16)
    o_ref[0] = _dot(a, w_ref[0].astype(bf16)) + b_ref[0]


def _mod_call(cc, w_mod, b_mod):
    depth, d, n = w_mod.shape
    tn = 1024
    return pl.pallas_call(
        _mod_kernel,
        out_shape=jax.ShapeDtypeStruct((depth, 8, n), f32),
        grid=(depth, n // tn),
        in_specs=[pl.BlockSpec((8, d), lambda l, j: (0, 0)),
                  pl.BlockSpec((1, d, tn), lambda l, j: (l, 0, j)),
                  pl.BlockSpec((1, 1, tn), lambda l, j: (l, 0, j))],
        out_specs=pl.BlockSpec((1, 8, tn), lambda l, j: (l, 0, j)),
        compiler_params=_cparams(("arbitrary", "arbitrary"), 40),
        name="mod_vectors",
    )(cc, w_mod, b_mod.reshape(depth, 1, n))


def _norm_kernel(x_ref, nw_ref, sc_ref, sh_ref, o_ref):
    x = x_ref[...]
    y = x * lax.rsqrt(jnp.mean(x * x, axis=-1, keepdims=True) + EPS) * nw_ref[...]
    o_ref[...] = (y * (1.0 + sc_ref[0]) + sh_ref[0]).astype(o_ref.dtype)


def _norm_call(x, nw, modl, k_sc, k_sh, seg_of_block, tm, out_dtype):
    r, d = x.shape
    return pl.pallas_call(
        _norm_kernel,
        out_shape=jax.ShapeDtypeStruct((r, d), out_dtype),
        grid=(r // tm,),
        in_specs=[pl.BlockSpec((tm, d), lambda i: (i, 0)),
                  pl.BlockSpec((1, d), lambda i: (0, 0)),
                  pl.BlockSpec((1, 1, d), lambda i: (seg_of_block(i) * 6 + k_sc, 0, 0)),
                  pl.BlockSpec((1, 1, d), lambda i: (seg_of_block(i) * 6 + k_sh, 0, 0))],
        out_specs=pl.BlockSpec((tm, d), lambda i: (i, 0)),
        compiler_params=_cparams(("arbitrary",), 32),
        name="norm_modulate",
    )(x, nw.reshape(1, d), modl, modl)


def _qk_proj_kernel(h_ref, w_ref, gain_ref, cos_ref, sa_ref, sb_ref, o_ref):
    acc = _dot(h_ref[...], w_ref[...])
    cos, sa, sb = cos_ref[...], sa_ref[...], sb_ref[...]
    gain = gain_ref[...]
    tn = acc.shape[1]
    for hd in range(tn // HEAD_DIM):
        sl = slice(hd * HEAD_DIM, (hd + 1) * HEAD_DIM)
        t = acc[:, sl]
        y = t * lax.rsqrt(jnp.mean(t * t, axis=-1, keepdims=True) + EPS) * gain[:, sl]
        up = pltpu.roll(y, HEAD_DIM - ROPE_PAIRS, axis=1)
        dn = pltpu.roll(y, ROPE_PAIRS, axis=1)
        o_ref[:, sl] = (y * cos + up * sa + dn * sb).astype(o_ref.dtype)


def _plain_proj_kernel(h_ref, w_ref, o_ref):
    o_ref[...] = _dot(h_ref[...], w_ref[...]).astype(o_ref.dtype)


def _gate_proj_kernel(h_ref, w_ref, o_ref):
    o_ref[...] = _sigmoid(_dot(h_ref[...], w_ref[...])).astype(o_ref.dtype)


def _proj_call(kernel, h, w, extra, extra_specs, nrows, tm, tn, name):
    k = h.shape[1]
    n = w.shape[1]
    return pl.pallas_call(
        kernel,
        out_shape=jax.ShapeDtypeStruct((nrows, n), bf16),
        grid=(nrows // tm, n // tn),
        in_specs=[pl.BlockSpec((tm, k), lambda i, j: (i, 0)),
                  pl.BlockSpec((k, tn), lambda i, j: (0, j))] + extra_specs,
        out_specs=pl.BlockSpec((tm, tn), lambda i, j: (i, j)),
        compiler_params=_cparams(("arbitrary", "arbitrary"), 40),
        name=name,
    )(h, w, *extra)


def _stack_heads(q, n):
    return jnp.concatenate([q[:, g * HEAD_DIM:(g + 1) * HEAD_DIM] for g in range(n)], axis=0)


def _unstack_heads(o, n):
    t = o.shape[0] // n
    return jnp.concatenate([o[g * t:(g + 1) * t] for g in range(n)], axis=1)


def _q_row_block(nlq, ncq, nbatch):
    def f(b, qi):
        return jnp.where(qi < nlq, b * nlq + qi, nbatch * nlq + b * ncq + (qi - nlq))
    return f


def _attn_a_kernel(sink_ref, q_ref, kl_ref, kc_ref, vl_ref, vc_ref, o_ref, *, tq, seq, nlq):
    kh = pl.program_id(1)
    qi = pl.program_id(2)
    qs = _stack_heads(q_ref[...], GQA_GROUP)
    is_lat = qi < nlq
    q0 = jnp.where(is_lat, qi, 0) * tq
    band = tq + 2 * WINDOW
    start = pl.multiple_of(jnp.clip(q0 - WINDOW, 0, seq - band), WINDOW)
    kb = kl_ref[pl.ds(start, band), :]
    vb = vl_ref[pl.ds(start, band), :]
    s_ctx = _dot_nt(qs, kc_ref[...])
    s_loc = _dot_nt(qs, kb)
    m_rows = GQA_GROUP * tq
    qpos = q0 + (lax.broadcasted_iota(jnp.int32, (m_rows, band), 0) & (tq - 1))
    kpos = start + lax.broadcasted_iota(jnp.int32, (m_rows, band), 1)
    win = jnp.where(is_lat, WINDOW, -1)
    s_loc = jnp.where(jnp.abs(qpos - kpos) <= win, s_loc, NEG_INF)
    sink = jnp.concatenate(
        [jnp.full((tq, 1), sink_ref[kh * GQA_GROUP + g], f32) for g in range(GQA_GROUP)], axis=0)
    m = jnp.maximum(jnp.maximum(jnp.max(s_ctx, axis=-1, keepdims=True),
                                jnp.max(s_loc, axis=-1, keepdims=True)), sink)
    p_ctx = jnp.exp(s_ctx - m)
    p_loc = jnp.exp(s_loc - m)
    l = (jnp.sum(p_ctx, axis=-1, keepdims=True) + jnp.sum(p_loc, axis=-1, keepdims=True)
         + jnp.exp(sink - m))
    o = (_dot(p_ctx.astype(bf16), vc_ref[...]) + _dot(p_loc.astype(bf16), vb)) / l
    o_ref[...] = _unstack_heads(o, GQA_GROUP).astype(o_ref.dtype)


def _attn_c_kernel(q_ref, kl_ref, kc_ref, vl_ref, vc_ref, o_ref, m_ref, l_ref, acc_ref,
                   *, tk, seq, nlq):
    qi = pl.program_id(2)
    qs = _stack_heads(q_ref[...], GQA_GROUP)

    s = _dot_nt(qs, kc_ref[...])
    m0 = jnp.max(s, axis=-1, keepdims=True)
    p = jnp.exp(s - m0)
    m_ref[...] = m0
    l_ref[...] = jnp.sum(p, axis=-1, keepdims=True)
    acc_ref[...] = _dot(p.astype(bf16), vc_ref[...])

    def chunk(c, carry):
        off = pl.multiple_of(c * tk, tk)
        s = _dot_nt(qs, kl_ref[pl.ds(off, tk), :])
        m_old = m_ref[...]
        m_new = jnp.maximum(m_old, jnp.max(s, axis=-1, keepdims=True))
        alpha = jnp.exp(m_old - m_new)
        p = jnp.exp(s - m_new)
        l_ref[...] = alpha * l_ref[...] + jnp.sum(p, axis=-1, keepdims=True)
        acc_ref[...] = alpha * acc_ref[...] + _dot(p.astype(bf16), vl_ref[pl.ds(off, tk), :])
        m_ref[...] = m_new
        return carry

    lax.fori_loop(0, jnp.where(qi < nlq, seq // tk, 0), chunk, 0)
    o = acc_ref[...] / l_ref[...]
    o_ref[...] = _unstack_heads(o, GQA_GROUP).astype(o_ref.dtype)


def _attn_b_kernel(q_ref, kl_ref, kc_ref, vl_ref, vc_ref, lam_ref, g_ref, o_ref,
                   m_ref, l_ref, acc_ref, *, tk, seq, nlq, lam_init):
    qi = pl.program_id(2)
    q = q_ref[...]
    tq = q.shape[0]
    qc = (q[:, :HEAD_DIM], q[:, HEAD_DIM:])

    def scores(k):
        return (_dot_nt(qc[0], k[:, :HEAD_DIM]), _dot_nt(qc[1], k[:, HEAD_DIM:]))

    s0, s1 = scores(kc_ref[...])
    s = jnp.concatenate([s0, s1], axis=0)
    m0 = jnp.max(s, axis=-1, keepdims=True)
    p = jnp.exp(s - m0)
    m_ref[...] = m0
    l_ref[...] = jnp.sum(p, axis=-1, keepdims=True)
    acc_ref[...] = _dot(p.astype(bf16), vc_ref[...])

    def chunk(c, carry):
        off = pl.multiple_of(c * tk, tk)
        s0, s1 = scores(kl_ref[pl.ds(off, tk), :])
        s = jnp.concatenate([s0, s1], axis=0)
        m_old = m_ref[...]
        m_new = jnp.maximum(m_old, jnp.max(s, axis=-1, keepdims=True))
        alpha = jnp.exp(m_old - m_new)
        p = jnp.exp(s - m_new)
        l_ref[...] = alpha * l_ref[...] + jnp.sum(p, axis=-1, keepdims=True)
        acc_ref[...] = alpha * acc_ref[...] + _dot(p.astype(bf16), vl_ref[pl.ds(off, tk), :])
        m_ref[...] = m_new
        return carry

    lax.fori_loop(0, jnp.where(qi < nlq, seq // tk, 0), chunk, 0)

    bl = lam_ref[...]
    lam = (jnp.exp(jnp.sum(bl[0:1] * bl[1:2], axis=-1, keepdims=True))
           - jnp.exp(jnp.sum(bl[2:3] * bl[3:4], axis=-1, keepdims=True)) + lam_init)
    on = acc_ref[...] / l_ref[...]
    o = on[:tq] - lam * on[tq:]
    y = o * lax.rsqrt(jnp.mean(o * o, axis=-1, keepdims=True) + EPS) * g_ref[...]
    o_ref[...] = (y * (1.0 - lam_init)).astype(o_ref.dtype)


def _attention(qk, v, a_sink, b_lam, b_subln, *, nbatch, seq, ctx, with_ctx, lam_init):
    tq = 256
    tk = min(512, seq)
    nlq = seq // tq
    ncq = ctx // tq
    nq = nlq + (ncq if with_ctx else 0)
    nl_rows = nbatch * seq
    rows = nl_rows + (nbatch * ctx if with_ctx else 0)
    cb = nl_rows // ctx
    qrow = _q_row_block(nlq, ncq, nbatch)
    gw = GQA_GROUP * HEAD_DIM

    def gqa_specs(q_off, k_off, v_off):
        return [
            pl.BlockSpec((tq, gw), lambda b, h, qi, *_: (qrow(b, qi), q_off // gw + h)),
            pl.BlockSpec((seq, HEAD_DIM), lambda b, h, qi, *_: (b, k_off // HEAD_DIM + h)),
            pl.BlockSpec((ctx, HEAD_DIM), lambda b, h, qi, *_: (cb + b, k_off // HEAD_DIM + h)),
            pl.BlockSpec((seq, HEAD_DIM), lambda b, h, qi, *_: (b, v_off // HEAD_DIM + h)),
            pl.BlockSpec((ctx, HEAD_DIM), lambda b, h, qi, *_: (cb + b, v_off // HEAD_DIM + h)),
        ]

    sem = ("arbitrary", "arbitrary", "arbitrary")
    m_rows = GQA_GROUP * tq

    oa = pl.pallas_call(
        functools.partial(_attn_a_kernel, tq=tq, seq=seq, nlq=nlq),
        out_shape=jax.ShapeDtypeStruct((rows, A_Q), bf16),
        grid_spec=pltpu.PrefetchScalarGridSpec(
            num_scalar_prefetch=1,
            grid=(nbatch, A_KV_HEADS, nq),
            in_specs=gqa_specs(QK_AQ, QK_AK, V_AV),
            out_specs=pl.BlockSpec((tq, gw), lambda b, h, qi, *_: (qrow(b, qi), h)),
        ),
        compiler_params=_cparams(sem, 40),
        name="attn_window",
    )(a_sink, qk, qk, qk, v, v)

    oc = pl.pallas_call(
        functools.partial(_attn_c_kernel, tk=tk, seq=seq, nlq=nlq),
        out_shape=jax.ShapeDtypeStruct((rows, C_Q), bf16),
        grid=(nbatch, C_KV_HEADS, nq),
        in_specs=gqa_specs(QK_CQ, QK_CK, V_CV),
        out_specs=pl.BlockSpec((tq, gw), lambda b, h, qi: (qrow(b, qi), h)),
        scratch_shapes=[pltpu.VMEM((m_rows, 1), f32), pltpu.VMEM((m_rows, 1), f32),
                        pltpu.VMEM((m_rows, HEAD_DIM), f32)],
        compiler_params=_cparams(sem, 40),
        name="attn_global",
    )(qk, qk, qk, v, v)

    bw = 2 * HEAD_DIM
    ob = pl.pallas_call(
        functools.partial(_attn_b_kernel, tk=tk, seq=seq, nlq=nlq, lam_init=lam_init),
        out_shape=jax.ShapeDtypeStruct((rows, B_V), bf16),
        grid=(nbatch, B_HEADS, nq),
        in_specs=[
            pl.BlockSpec((tq, bw), lambda b, h, qi: (qrow(b, qi), QK_BQ // bw + h)),
            pl.BlockSpec((seq, bw), lambda b, h, qi: (b, QK_BK // bw + h)),
            pl.BlockSpec((ctx, bw), lambda b, h, qi: (cb + b, QK_BK // bw + h)),
            pl.BlockSpec((seq, bw), lambda b, h, qi: (b, V_BV // bw + h)),
            pl.BlockSpec((ctx, bw), lambda b, h, qi: (cb + b, V_BV // bw + h)),
            pl.BlockSpec((4, HEAD_DIM), lambda b, h, qi: (0, 0)),
            pl.BlockSpec((1, bw), lambda b, h, qi: (0, 0)),
        ],
        out_specs=pl.BlockSpec((tq, bw), lambda b, h, qi: (qrow(b, qi), h)),
        scratch_shapes=[pltpu.VMEM((2 * tq, 1), f32), pltpu.VMEM((2 * tq, 1), f32),
                        pltpu.VMEM((2 * tq, bw), f32)],
        compiler_params=_cparams(sem, 48),
        name="attn_diff",
    )(qk, qk, qk, v, v, b_lam, b_subln.reshape(1, bw))
    return oa, ob, oc


def _merge_kernel(oa_ref, ob_ref, oc_ref, w_ref, ga_ref, gb_ref, gc_ref, y_ref):
    y = (ga_ref[...].astype(f32) * _dot(oa_ref[...], w_ref[0])
         + gb_ref[...].astype(f32) * _dot(ob_ref[...], w_ref[1])
         + gc_ref[...].astype(f32) * _dot(oc_ref[...], w_ref[2]))
    y_ref[...] = y.astype(y_ref.dtype)


def _merge_call(oa, ob, oc, wb, gates, tm, tn):
    rows, bwid = oa.shape
    d = wb.shape[2]
    nj = d // tn
    o_spec = pl.BlockSpec((tm, bwid), lambda i, j: (i, 0))
    return pl.pallas_call(
        _merge_kernel,
        out_shape=jax.ShapeDtypeStruct((rows, d), bf16),
        grid=(rows // tm, nj),
        in_specs=[o_spec, o_spec, o_spec,
                  pl.BlockSpec((3, bwid, tn), lambda i, j: (0, 0, j)),
                  pl.BlockSpec((tm, tn), lambda i, j: (i, j)),
                  pl.BlockSpec((tm, tn), lambda i, j: (i, nj + j)),
                  pl.BlockSpec((tm, tn), lambda i, j: (i, 2 * nj + j))],
        out_specs=pl.BlockSpec((tm, tn), lambda i, j: (i, j)),
        compiler_params=_cparams(("arbitrary", "arbitrary"), 40),
        name="branch_merge",
    )(oa, ob, oc, wb, gates, gates, gates)


def _out_proj_kernel(y_ref, w_ref, x_ref, g1_ref, nw_ref, sc_ref, sh_ref, xo_ref, h_ref):
    xn = x_ref[...] + g1_ref[0] * _dot(y_ref[...], w_ref[...])
    xo_ref[...] = xn
    t = xn * lax.rsqrt(jnp.mean(xn * xn, axis=-1, keepdims=True) + EPS) * nw_ref[...]
    h_ref[...] = (t * (1.0 + sc_ref[0]) + sh_ref[0]).astype(h_ref.dtype)


def _out_proj_call(y, w_out, x, modl, nw2, seg_of_block, tm, h_dtype):
    rows, d = y.shape

    def mod_spec(k):
        return pl.BlockSpec((1, 1, d), lambda i: (seg_of_block(i) * 6 + k, 0, 0))

    row_spec = pl.BlockSpec((tm, d), lambda i: (i, 0))
    return pl.pallas_call(
        _out_proj_kernel,
        out_shape=(jax.ShapeDtypeStruct((rows, d), f32), jax.ShapeDtypeStruct((rows, d), h_dtype)),
        grid=(rows // tm,),
        in_specs=[row_spec, pl.BlockSpec((d, d), lambda i: (0, 0)), row_spec,
                  mod_spec(2), pl.BlockSpec((1, d), lambda i: (0, 0)), mod_spec(4), mod_spec(3)],
        out_specs=(row_spec, row_spec),
        compiler_params=_cparams(("arbitrary",), 48),
        name="out_proj_residual_norm",
    )(y, w_out, x, modl, nw2.reshape(1, d), modl, modl)


def _dense_ffn_kernel(h_ref, w1_ref, w3_ref, w2_ref, x_ref, g2_ref, o_ref, acc_ref):
    j = pl.program_id(1)

    @pl.when(j == 0)
    def _():
        acc_ref[...] = jnp.zeros_like(acc_ref)

    h = h_ref[...]
    a = _dot(h, w1_ref[...])
    b = _dot(h, w3_ref[...])
    g = (a * _sigmoid(a) * b).astype(bf16)
    acc_ref[...] += _dot(g, w2_ref[...])

    @pl.when(j == pl.num_programs(1) - 1)
    def _():
        o_ref[...] = x_ref[...] + g2_ref[0] * acc_ref[...]


def _dense_ffn_call(h2, w1, w3, w2, x, modl, seg_of_block, tm, tf):
    rows, d = h2.shape
    ff = w1.shape[1]
    row_spec = pl.BlockSpec((tm, d), lambda i, j: (i, 0))
    return pl.pallas_call(
        _dense_ffn_kernel,
        out_shape=jax.ShapeDtypeStruct((rows, d), f32),
        grid=(rows // tm, ff // tf),
        in_specs=[row_spec,
                  pl.BlockSpec((d, tf), lambda i, j: (0, j)),
                  pl.BlockSpec((d, tf), lambda i, j: (0, j)),
                  pl.BlockSpec((tf, d), lambda i, j: (j, 0)),
                  row_spec,
                  pl.BlockSpec((1, 1, d), lambda i, j: (seg_of_block(i) * 6 + 5, 0, 0))],
        out_specs=row_spec,
        scratch_shapes=[pltpu.VMEM((tm, d), f32)],
        compiler_params=_cparams(("arbitrary", "arbitrary"), 52),
        name="dense_swiglu",
    )(h2, w1, w3, w2, x, modl)


def _router_kernel(h_ref, w_ref, idx_ref, wt_ref, *, n_experts):
    logits = jnp.dot(h_ref[...], w_ref[...], preferred_element_type=f32,
                     precision=lax.Precision.HIGHEST)
    lane = lax.broadcasted_iota(jnp.int32, logits.shape, 1)
    lg = jnp.where(lane < n_experts, logits, -jnp.inf)
    m1 = jnp.max(lg, axis=-1, keepdims=True)
    i1 = jnp.min(jnp.where(lg == m1, lane, LANES), axis=-1, keepdims=True)
    lg2 = jnp.where(lane == i1, -jnp.inf, lg)
    m2 = jnp.max(lg2, axis=-1, keepdims=True)
    i2 = jnp.min(jnp.where(lg2 == m2, lane, LANES), axis=-1, keepdims=True)
    e2 = jnp.exp(m2 - m1)
    w1 = 1.0 / (1.0 + e2)
    w2 = e2 / (1.0 + e2)
    idx_ref[...] = jnp.where(lane == 0, i1, jnp.where(lane == 1, i2, 0))
    wt_ref[...] = jnp.where(lane == 0, w1, jnp.where(lane == 1, w2, 0.0))


def _router_call(h2, w_router, tm):
    n, d = h2.shape
    e = w_router.shape[1]
    wp = jnp.zeros((d, LANES), f32).at[:, :e].set(w_router)
    return pl.pallas_call(
        functools.partial(_router_kernel, n_experts=e),
        out_shape=(jax.ShapeDtypeStruct((n, LANES), jnp.int32), jax.ShapeDtypeStruct((n, LANES), f32)),
        grid=(n // tm,),
        in_specs=[pl.BlockSpec((tm, d), lambda i: (i, 0)), pl.BlockSpec((d, LANES), lambda i: (0, 0))],
        out_specs=(pl.BlockSpec((tm, LANES), lambda i: (i, 0)), pl.BlockSpec((tm, LANES), lambda i: (i, 0))),
        compiler_params=_cparams(("arbitrary",), 32),
        name="moe_router_top2",
    )(h2, wp)


def _gather_rows_kernel(src_ref, x_hbm, o_hbm, sem, *, chunk):
    base = pl.program_id(0) * chunk

    def row_copy(r):
        return pltpu.make_async_copy(x_hbm.at[pl.ds(src_ref[base + r], 1)],
                                     o_hbm.at[pl.ds(base + r, 1)], sem)

    def issue(r, c):
        row_copy(r).start()
        return c

    def drain(r, c):
        row_copy(r).wait()
        return c

    lax.fori_loop(0, chunk, issue, 0)
    lax.fori_loop(0, chunk, drain, 0)


def _gather_rows_call(x, src, chunk):
    cap = src.shape[0]
    d = x.shape[1]
    return pl.pallas_call(
        functools.partial(_gather_rows_kernel, chunk=chunk),
        out_shape=jax.ShapeDtypeStruct((cap, d), x.dtype),
        grid_spec=pltpu.PrefetchScalarGridSpec(
            num_scalar_prefetch=1,
            grid=(cap // chunk,),
            in_specs=[pl.BlockSpec(memory_space=pl.ANY)],
            out_specs=pl.BlockSpec(memory_space=pl.ANY),
            scratch_shapes=[pltpu.SemaphoreType.DMA(())],
        ),
        compiler_params=_cparams(("arbitrary",), 16),
        name="moe_gather_rows",
    )(src, x)


def _expert_kernel(be_ref, nu_ref, x_ref, w1_ref, w3_ref, w2_ref, o_ref, acc_ref):
    i = pl.program_id(0)
    j = pl.program_id(1)

    @pl.when(i < nu_ref[0])
    def _():
        @pl.when(j == 0)
        def _():
            acc_ref[...] = jnp.zeros_like(acc_ref)

        h = x_ref[...].astype(bf16)
        a = _dot(h, w1_ref[0])
        b = _dot(h, w3_ref[0])
        g = (a * _sigmoid(a) * b).astype(bf16)
        acc_ref[...] += _dot(g, w2_ref[0])

        @pl.when(j == pl.num_programs(1) - 1)
        def _():
            o_ref[...] = acc_ref[...]

    @pl.when(jnp.logical_and(i >= nu_ref[0], j == pl.num_programs(1) - 1))
    def _():
        o_ref[...] = jnp.zeros_like(o_ref)


def _expert_call(xs, blk_e, n_used, w1, w3, w2, tm, tf):
    cap, d = xs.shape
    ff = w1.shape[2]
    nf = ff // tf

    def bi(i, nu):
        return jnp.minimum(i, nu[0] - 1)

    def fj(i, j, nu):
        return jnp.where(i < nu[0], j, nf - 1)

    return pl.pallas_call(
        _expert_kernel,
        out_shape=jax.ShapeDtypeStruct((cap, d), f32),
        grid_spec=pltpu.PrefetchScalarGridSpec(
            num_scalar_prefetch=2,
            grid=(cap // tm, nf),
            in_specs=[pl.BlockSpec((tm, d), lambda i, j, be, nu: (bi(i, nu), 0)),
                      pl.BlockSpec((1, d, tf), lambda i, j, be, nu: (be[bi(i, nu)], 0, fj(i, j, nu))),
                      pl.BlockSpec((1, d, tf), lambda i, j, be, nu: (be[bi(i, nu)], 0, fj(i, j, nu))),
                      pl.BlockSpec((1, tf, d), lambda i, j, be, nu: (be[bi(i, nu)], fj(i, j, nu), 0))],
            out_specs=pl.BlockSpec((tm, d), lambda i, j, be, nu: (i, 0)),
            scratch_shapes=[pltpu.VMEM((tm, d), f32)],
        ),
        compiler_params=_cparams(("arbitrary", "arbitrary"), 52),
        name="moe_expert_swiglu",
    )(blk_e, n_used, xs, w1, w3, w2)


def _combine_kernel(dest_ref, y_hbm, x_ref, g2_ref, wt_ref, o_ref, buf0, buf1, sem, *, tm):
    base = pl.program_id(0) * tm

    def copies(r):
        s = (base + r) * TOP_K
        return (pltpu.make_async_copy(y_hbm.at[pl.ds(dest_ref[s], 1)], buf0.at[pl.ds(r, 1)], sem),
                pltpu.make_async_copy(y_hbm.at[pl.ds(dest_ref[s + 1], 1)], buf1.at[pl.ds(r, 1)], sem))

    def issue(r, c):
        c0, c1 = copies(r)
        c0.start()
        c1.start()
        return c

    def drain(r, c):
        c0, c1 = copies(r)
        c0.wait()
        c1.wait()
        return c

    lax.fori_loop(0, tm, issue, 0)
    lax.fori_loop(0, tm, drain, 0)
    wt = wt_ref[...]
    mix = wt[:, 0:1] * buf0[...] + wt[:, 1:2] * buf1[...]
    o_ref[...] = x_ref[...] + g2_ref[0] * mix


def _combine_call(yb, dest, x, modl, wts, seg_of_block, tm):
    n, d = x.shape
    return pl.pallas_call(
        functools.partial(_combine_kernel, tm=tm),
        out_shape=jax.ShapeDtypeStruct((n, d), f32),
        grid_spec=pltpu.PrefetchScalarGridSpec(
            num_scalar_prefetch=1,
            grid=(n // tm,),
            in_specs=[pl.BlockSpec(memory_space=pl.ANY),
                      pl.BlockSpec((tm, d), lambda i, dst: (i, 0)),
                      pl.BlockSpec((1, 1, d), lambda i, dst: (seg_of_block(i) * 6 + 5, 0, 0)),
                      pl.BlockSpec((tm, LANES), lambda i, dst: (i, 0))],
            out_specs=pl.BlockSpec((tm, d), lambda i, dst: (i, 0)),
            scratch_shapes=[pltpu.VMEM((tm, d), f32), pltpu.VMEM((tm, d), f32),
                            pltpu.SemaphoreType.DMA(())],
        ),
        compiler_params=_cparams(("arbitrary",), 32),
        name="moe_combine_residual",
    )(dest, yb, x, modl, wts)


def _moe_call(h2, x, modl, w_router, w1, w3, w2, seg_of_block):
    n, d = h2.shape
    n_exp = w_router.shape[1]
    tm = 512
    idx, wts = _router_call(h2, w_router, 256)
    flat_e = idx[:, :TOP_K].reshape(-1)
    n_slots = n * TOP_K
    onehot = (flat_e[:, None] == jnp.arange(n_exp, dtype=jnp.int32)[None, :]).astype(jnp.int32)
    csum = jnp.cumsum(onehot, axis=0)
    counts = csum[-1]
    rank = jnp.sum((csum - 1) * onehot, axis=1)
    padded = (counts + tm - 1) // tm * tm
    pend = jnp.cumsum(padded)
    pstart = pend - padded
    dest = (pstart[flat_e] + rank).astype(jnp.int32)
    cap = (n_slots // tm + n_exp) * tm
    src = jnp.zeros((cap,), jnp.int32).at[dest].set(jnp.arange(n_slots, dtype=jnp.int32) // TOP_K)
    n_blocks = cap // tm
    blk_start = jnp.arange(n_blocks, dtype=jnp.int32) * tm
    blk_e = jnp.minimum(jnp.searchsorted(pend, blk_start, side='right'), n_exp - 1).astype(jnp.int32)
    n_used = (pend[-1:] // tm).astype(jnp.int32)

    xs = _gather_rows_call(h2, src, 256)
    yb = _expert_call(xs, blk_e, n_used, w1, w3, w2, tm, 512)
    return _combine_call(yb, dest, x, modl, wts, seg_of_block, 256)


def _rope_tables(nbatch, seq, ctx):
    t = jnp.arange(seq, dtype=jnp.int32)
    row = (t // GRID_W).astype(f32)
    col = (t % GRID_W).astype(f32)
    inv = ROPE_THETA ** (-jnp.arange(ROPE_PAIRS, dtype=f32) / ROPE_PAIRS)
    lane = jnp.arange(HEAD_DIM)
    pos = jnp.where((lane // (2 * ROPE_PAIRS))[None, :] == 0, row[:, None], col[:, None])
    ang = pos * inv[lane % ROPE_PAIRS][None, :]
    first = ((lane % (2 * ROPE_PAIRS)) < ROPE_PAIRS)[None, :]
    cos = jnp.cos(ang)
    sin = jnp.sin(ang)
    sa = jnp.where(first, -sin, 0.0)
    sb = jnp.where(first, 0.0, sin)
    nctx = nbatch * ctx

    def full(tab, fill):
        return jnp.concatenate([jnp.tile(tab, (nbatch, 1)), jnp.full((nctx, HEAD_DIM), fill, f32)], axis=0)

    return full(cos, 1.0), full(sa, 0.0), full(sb, 0.0)


def kernel(x, c, ctx, c_ctx, w_mod, b_mod, norm1, norm2, w_in, qk_gain, a_sink, b_lambda, b_subln,
           w_branch, w_out, dense_w1, dense_w3, dense_w2, moe_router, moe_w1, moe_w3, moe_w2):
    nbatch, seq, d = x.shape
    nctx_len = ctx.shape[1]
    depth = w_mod.shape[0]
    nl = nbatch * seq
    nc = nbatch * nctx_len
    tm = 512

    def seg_of_block_for(t):
        return lambda i: jnp.minimum(i // (seq // t), nbatch)

    xs = jnp.concatenate([x.reshape(nl, d), ctx.reshape(nc, d)], axis=0)
    cc = jnp.zeros((8, d), f32).at[:nbatch].set(c).at[nbatch].set(c_ctx)
    mod = _mod_call(cc, w_mod, b_mod)
    cos, sa, sb = _rope_tables(nbatch, seq, nctx_len)
    scale = HEAD_DIM ** -0.5

    for l in range(depth):
        with_ctx = l < depth - 1
        lam_init = 0.8 - 0.6 * math.exp(-0.3 * l)
        rows = nl + nc if with_ctx else nl
        modl = mod[l].reshape(8 * 6, 1, d)

        wl = w_in[l]
        o = 0
        parts = {}
        for name, width in (("aq", A_Q), ("ak", A_KV), ("av", A_KV), ("bq", B_QK), ("bk", B_QK),
                            ("bv", B_V), ("cq", C_Q), ("ck", C_KV), ("cv", C_KV), ("g", 3 * d)):
            parts[name] = wl[:, o:o + width]
            o += width
        w_qk = jnp.concatenate([parts[k] for k in ("cq", "aq", "bq", "bk", "ck", "ak")], axis=1).astype(bf16)
        w_v = jnp.concatenate([parts[k] for k in ("bv", "cv", "av")], axis=1).astype(bf16)
        w_g = parts["g"].astype(bf16)
        g = qk_gain[l]
        gain = jnp.concatenate([jnp.tile(g[4] * scale, C_HEADS), jnp.tile(g[0] * scale, A_HEADS),
                                jnp.tile(g[2] * scale, 2 * B_HEADS), jnp.tile(g[3], 2 * B_HEADS),
                                jnp.tile(g[5], C_KV_HEADS), jnp.tile(g[1], A_KV_HEADS)]).reshape(1, QK_COLS)

        h = _norm_call(xs, norm1[l], modl, 1, 0, seg_of_block_for(tm), tm, bf16)
        tab_spec = pl.BlockSpec((tm, HEAD_DIM), lambda i, j: (i, 0))
        qk = _proj_call(_qk_proj_kernel, h, w_qk, (gain, cos, sa, sb),
                        [pl.BlockSpec((1, 512), lambda i, j: (0, j)), tab_spec, tab_spec, tab_spec],
                        nl + nc, tm, 512, "qk_proj_norm_rope")
        v = _proj_call(_plain_proj_kernel, h, w_v, (), [], nl + nc, tm, 512, "v_proj")
        gates = _proj_call(_gate_proj_kernel, h, w_g, (), [], rows, tm, 512, "gate_proj")

        oa, ob, oc = _attention(qk, v, a_sink[l], b_lambda[l], b_subln[l], nbatch=nbatch, seq=seq,
                                ctx=nctx_len, with_ctx=with_ctx, lam_init=lam_init)
        y = _merge_call(oa, ob, oc, w_branch[l].astype(bf16), gates, tm, 512)
        moe_layer = l % 2 == 1
        xn, h2 = _out_proj_call(y, w_out[l].astype(bf16), xs, modl, norm2[l], seg_of_block_for(256), 256,
                                f32 if moe_layer else bf16)
        i = l // 2
        if not moe_layer:
            xs = _dense_ffn_call(h2, dense_w1[i].astype(bf16), dense_w3[i].astype(bf16),
                                 dense_w2[i].astype(bf16), xn, modl, seg_of_block_for(tm), tm, 512)
        else:
            xs = _moe_call(h2, xn, modl, moe_router[i], moe_w1[i].astype(bf16), moe_w3[i].astype(bf16),
                           moe_w2[i].astype(bf16), seg_of_block_for(256))
    return xs[:nl].reshape(nbatch, seq, d)
```

```python
import functools
import math

import jax
import jax.numpy as jnp
from jax import lax
from jax.experimental import pallas as pl
from jax.experimental.pallas import tpu as pltpu

f32 = jnp.float32
bf16 = jnp.bfloat16

HEAD_DIM = 128
GRID_W = 64
ROPE_PAIRS = HEAD_DIM // 4
ROPE_THETA = 10000.0
EPS = 1e-6
NEG_INF = -1e30
WINDOW = 128
A_HEADS, A_KV_HEADS = 8, 2
B_HEADS = 4
C_HEADS, C_KV_HEADS = 8, 2
GQA_GROUP = 4
TOP_K = 2
LANES = 128
MXU_WIDTH = 256
MIB = 1024 * 1024
LOG2E = math.log2(math.e)

A_Q, A_KV = A_HEADS * HEAD_DIM, A_KV_HEADS * HEAD_DIM
B_QK, B_V = B_HEADS * 2 * HEAD_DIM, B_HEADS * 2 * HEAD_DIM
C_Q, C_KV = C_HEADS * HEAD_DIM, C_KV_HEADS * HEAD_DIM
QK_CQ, QK_AQ, QK_BQ, QK_BK = 0, C_Q, C_Q + A_Q, C_Q + A_Q + B_QK
QK_CK = QK_BK + B_QK
QK_AK = QK_CK + C_KV
QK_COLS = QK_AK + A_KV
V_BV, V_CV = 0, B_V
V_AV = V_CV + C_KV_HEADS * 2 * HEAD_DIM
V_COLS = V_AV + A_KV


def _cparams(sem, vmem_mib):
    return pltpu.CompilerParams(dimension_semantics=sem, vmem_limit_bytes=vmem_mib * MIB)


def _dot(a, b):
    return jnp.dot(a, b, preferred_element_type=f32)


def _dot_nt(a, b):
    return lax.dot_general(a, b, (((1,), (1,)), ((), ())), preferred_element_type=f32)


def _sigmoid(x):
    return 0.5 * jnp.tanh(0.5 * x) + 0.5


def _rowmax(s):
    return jnp.max(s, axis=-1, keepdims=True)


def _rowsum(s):
    return jnp.sum(s, axis=-1, keepdims=True)


def _lane_tile(x, n):
    return jnp.concatenate([x] * n, axis=1)


def _mod_kernel(c_ref, w_ref, b_ref, o_ref):
    c = c_ref[...]
    a = (c * _sigmoid(c)).astype(bf16)
    o_ref[0] = _dot(a, w_ref[0].astype(bf16)) + b_ref[0]


def _mod_call(cc, w_mod, b_mod):
    depth, d, n = w_mod.shape
    tn = 1024
    return pl.pallas_call(
        _mod_kernel,
        out_shape=jax.ShapeDtypeStruct((depth, 8, n), f32),
        grid=(depth, n // tn),
        in_specs=[pl.BlockSpec((8, d), lambda l, j: (0, 0)),
                  pl.BlockSpec((1, d, tn), lambda l, j: (l, 0, j)),
                  pl.BlockSpec((1, 1, tn), lambda l, j: (l, 0, j))],
        out_specs=pl.BlockSpec((1, 8, tn), lambda l, j: (l, 0, j)),
        compiler_params=_cparams(("arbitrary", "arbitrary"), 40),
        name="mod_vectors",
    )(cc, w_mod, b_mod.reshape(depth, 1, n))


def _norm_kernel(x_ref, nw_ref, sc_ref, sh_ref, o_ref):
    x = x_ref[...]
    y = x * lax.rsqrt(jnp.mean(x * x, axis=-1, keepdims=True) + EPS) * nw_ref[...]
    o_ref[...] = (y * (1.0 + sc_ref[0]) + sh_ref[0]).astype(o_ref.dtype)


def _norm_call(x, nw, modl, k_sc, k_sh, seg_of_block, tm, out_dtype):
    r, d = x.shape
    return pl.pallas_call(
        _norm_kernel,
        out_shape=jax.ShapeDtypeStruct((r, d), out_dtype),
        grid=(r // tm,),
        in_specs=[pl.BlockSpec((tm, d), lambda i: (i, 0)),
                  pl.BlockSpec((1, d), lambda i: (0, 0)),
                  pl.BlockSpec((1, 1, d), lambda i: (seg_of_block(i) * 6 + k_sc, 0, 0)),
                  pl.BlockSpec((1, 1, d), lambda i: (seg_of_block(i) * 6 + k_sh, 0, 0))],
        out_specs=pl.BlockSpec((tm, d), lambda i: (i, 0)),
        compiler_params=_cparams(("arbitrary",), 32),
        name="norm_modulate",
    )(x, nw.reshape(1, d), modl, modl)


def _qk_proj_kernel(h_ref, w_ref, gain_ref, cos_ref, sa_ref, sb_ref, o_ref):
    cos, sa, sb = cos_ref[...], sa_ref[...], sb_ref[...]
    tn = o_ref.shape[1]
    for grp in range(tn // MXU_WIDTH):
        c0 = grp * MXU_WIDTH
        acc = _dot(h_ref[...], w_ref[:, c0:c0 + MXU_WIDTH])
        for hd in range(MXU_WIDTH // HEAD_DIM):
            t = acc[:, hd * HEAD_DIM:(hd + 1) * HEAD_DIM]
            sl = slice(c0 + hd * HEAD_DIM, c0 + (hd + 1) * HEAD_DIM)
            y = t * lax.rsqrt(jnp.mean(t * t, axis=-1, keepdims=True) + EPS) * gain_ref[:, sl]
            up = pltpu.roll(y, HEAD_DIM - ROPE_PAIRS, axis=1)
            dn = pltpu.roll(y, ROPE_PAIRS, axis=1)
            o_ref[:, sl] = (y * cos + up * sa + dn * sb).astype(o_ref.dtype)


def _bias_proj_kernel(h_ref, w_ref, b_ref, o_ref):
    o_ref[...] = (_dot(h_ref[...], w_ref[...]) + b_ref[...]).astype(o_ref.dtype)


def _gate_proj_kernel(h_ref, w_ref, o_ref):
    tn = o_ref.shape[1]
    for grp in range(tn // MXU_WIDTH):
        sl = slice(grp * MXU_WIDTH, (grp + 1) * MXU_WIDTH)
        o_ref[:, sl] = _sigmoid(_dot(h_ref[...], w_ref[:, sl])).astype(o_ref.dtype)


def _proj_call(kernel, h, w, extra, extra_specs, nrows, tm, tn, name):
    k = h.shape[1]
    n = w.shape[1]
    return pl.pallas_call(
        kernel,
        out_shape=jax.ShapeDtypeStruct((nrows, n), bf16),
        grid=(nrows // tm, n // tn),
        in_specs=[pl.BlockSpec((tm, k), lambda i, j: (i, 0)),
                  pl.BlockSpec((k, tn), lambda i, j: (0, j))] + extra_specs,
        out_specs=pl.BlockSpec((tm, tn), lambda i, j: (i, j)),
        compiler_params=_cparams(("arbitrary", "arbitrary"), 40),
        name=name,
    )(h, w, *extra)


def _stack_heads(q, n):
    return jnp.concatenate([q[:, g * HEAD_DIM:(g + 1) * HEAD_DIM] for g in range(n)], axis=0)


def _unstack_heads(o, n):
    t = o.shape[0] // n
    return jnp.concatenate([o[g * t:(g + 1) * t] for g in range(n)], axis=1)


def _softmax_pv(s, v, ex, extra_logit=None):
    m = _rowmax(s)
    if extra_logit is not None:
        m = jnp.maximum(m, extra_logit)
    p = ex(s - m)
    l = _rowsum(p)
    if extra_logit is not None:
        l = l + ex(extra_logit - m)
    return _dot(p.astype(bf16), v) / l


def _sink_column(sink_ref, kh, tq):
    return jnp.concatenate(
        [jnp.full((tq, 1), sink_ref[kh * GQA_GROUP + g], f32) for g in range(GQA_GROUP)], axis=0)


def _diff_finish(on0, on1, lam_ref, g_ref, lam_init):
    bl = lam_ref[...]
    lam = (jnp.exp(_rowsum(bl[0:1] * bl[1:2])) - jnp.exp(_rowsum(bl[2:3] * bl[3:4])) + lam_init)
    o = on0 - lam * on1
    y = o * lax.rsqrt(jnp.mean(o * o, axis=-1, keepdims=True) + EPS) * g_ref[...]
    return y * (1.0 - lam_init)


def _attn_a_lat_kernel(sink_ref, q_ref, kl_ref, kc_ref, vl_ref, vc_ref, o_ref, *, tq, seq):
    kh = pl.program_id(1)
    qi = pl.program_id(2)
    qs = _stack_heads(q_ref[...], GQA_GROUP)
    q0 = qi * tq
    band = tq + 2 * WINDOW
    start = pl.multiple_of(jnp.clip(q0 - WINDOW, 0, seq - band), WINDOW)
    kb = kl_ref[pl.ds(start, band), :]
    vb = vl_ref[pl.ds(start, band), :]
    s_ctx = _dot_nt(qs, kc_ref[...])
    s_loc = _dot_nt(qs, kb)
    m_rows = GQA_GROUP * tq
    qpos = q0 + (lax.broadcasted_iota(jnp.int32, (m_rows, band), 0) & (tq - 1))
    kpos = start + lax.broadcasted_iota(jnp.int32, (m_rows, band), 1)
    s_loc = jnp.where(jnp.abs(qpos - kpos) <= WINDOW, s_loc, NEG_INF)
    sink = _sink_column(sink_ref, kh, tq)
    m = jnp.maximum(jnp.maximum(_rowmax(s_ctx), _rowmax(s_loc)), sink)
    p_ctx = jnp.exp(s_ctx - m)
    p_loc = jnp.exp(s_loc - m)
    l = _rowsum(p_ctx) + _rowsum(p_loc) + jnp.exp(sink - m)
    o = (_dot(p_ctx.astype(bf16), vc_ref[...]) + _dot(p_loc.astype(bf16), vb)) / l
    o_ref[...] = _unstack_heads(o, GQA_GROUP).astype(o_ref.dtype)


def _attn_a_ctx_kernel(sink_ref, q_ref, kc_ref, vc_ref, o_ref, *, tq):
    qs = _stack_heads(q_ref[...], GQA_GROUP)
    o = _softmax_pv(_dot_nt(qs, kc_ref[...]), vc_ref[...], jnp.exp,
                    _sink_column(sink_ref, pl.program_id(1), tq))
    o_ref[...] = _unstack_heads(o, GQA_GROUP).astype(o_ref.dtype)


def _attn_c_lat_kernel(q_ref, kl_ref, kc_ref, vl_ref, vc_ref, o_ref, m_ref, acc_ref, *, tk, n_chunks):
    qs = _stack_heads(q_ref[...], GQA_GROUP)
    s = _dot_nt(qs, kc_ref[...])
    m0 = _rowmax(s)
    m_ref[...] = jnp.broadcast_to(m0, m_ref.shape)
    acc_ref[...] = _dot(jnp.exp2(s - m0).astype(bf16), vc_ref[...])

    def chunk(c, carry):
        off = pl.multiple_of(c * tk, tk)
        s = _dot_nt(qs, kl_ref[pl.ds(off, tk), :])
        m_old = m_ref[...]
        m_new = jnp.maximum(m_old, _rowmax(s))
        alpha = jnp.exp2(m_old - m_new)
        p = jnp.exp2(s - _lane_tile(m_new, tk // LANES))
        acc_ref[...] = (_lane_tile(alpha, 2) * acc_ref[...]
                        + _dot(p.astype(bf16), vl_ref[pl.ds(off, tk), :]))
        m_ref[...] = m_new
        return carry

    lax.fori_loop(0, n_chunks, chunk, 0)
    acc = acc_ref[...]
    o = acc[:, :HEAD_DIM] / acc[:, HEAD_DIM:]
    o_ref[...] = _unstack_heads(o, GQA_GROUP).astype(o_ref.dtype)


def _attn_c_ctx_kernel(q_ref, kc_ref, vc_ref, o_ref):
    qs = _stack_heads(q_ref[...], GQA_GROUP)
    o = _softmax_pv(_dot_nt(qs, kc_ref[...]), vc_ref[...], jnp.exp2)
    o_ref[...] = _unstack_heads(o, GQA_GROUP).astype(o_ref.dtype)


def _diff_scores(q, k):
    return jnp.concatenate([_dot_nt(q[:, :HEAD_DIM], k[:, :HEAD_DIM]),
                            _dot_nt(q[:, HEAD_DIM:], k[:, HEAD_DIM:])], axis=0)


def _attn_b_lat_kernel(q_ref, kl_ref, kc_ref, vl_ref, vc_ref, lam_ref, g_ref, o_ref,
                       m_ref, l_ref, acc_ref, *, tk, n_chunks, lam_init):
    q = q_ref[...]
    tq = q.shape[0]
    s = _diff_scores(q, kc_ref[...])
    m0 = _rowmax(s)
    p = jnp.exp2(s - m0)
    m_ref[...] = jnp.broadcast_to(m0, m_ref.shape)
    l_ref[...] = jnp.broadcast_to(_rowsum(p), l_ref.shape)
    acc_ref[...] = _dot(p.astype(bf16), vc_ref[...])

    def chunk(c, carry):
        off = pl.multiple_of(c * tk, tk)
        s = _diff_scores(q, kl_ref[pl.ds(off, tk), :])
        m_old = m_ref[...]
        m_new = jnp.maximum(m_old, _rowmax(s))
        alpha = jnp.exp2(m_old - m_new)
        p = jnp.exp2(s - _lane_tile(m_new, tk // LANES))
        l_ref[...] = alpha * l_ref[...] + _rowsum(p)
        acc_ref[...] = (_lane_tile(alpha, 2) * acc_ref[...]
                        + _dot(p.astype(bf16), vl_ref[pl.ds(off, tk), :]))
        m_ref[...] = m_new
        return carry

    lax.fori_loop(0, n_chunks, chunk, 0)
    on = acc_ref[...] / _lane_tile(l_ref[...], 2)
    o_ref[...] = _diff_finish(on[:tq], on[tq:], lam_ref, g_ref, lam_init).astype(o_ref.dtype)


def _attn_b_ctx_kernel(q_ref, kc_ref, vc_ref, lam_ref, g_ref, o_ref, *, lam_init):
    q = q_ref[...]
    tq = q.shape[0]
    on = _softmax_pv(_diff_scores(q, kc_ref[...]), vc_ref[...], jnp.exp2)
    o_ref[...] = _diff_finish(on[:tq], on[tq:], lam_ref, g_ref, lam_init).astype(o_ref.dtype)


def _attention(qk, v, a_sink, b_lam, b_subln, *, nbatch, seq, ctx, with_ctx, lam_init):
    tq_gqa = 256
    tq_diff = min(512, seq)
    tq_ctx = ctx
    tk = min(512, seq)
    nl_rows = nbatch * seq
    nc_rows = nbatch * ctx
    cb = nl_rows // ctx
    gw = GQA_GROUP * HEAD_DIM
    bw = 2 * HEAD_DIM
    sem = ("arbitrary", "arbitrary", "arbitrary")

    def lat_q(tq, width, off):
        return pl.BlockSpec((tq, width), lambda b, h, qi, *_: (b * (seq // tq) + qi, off // width + h))

    def ctx_q(width, off):
        return pl.BlockSpec((tq_ctx, width), lambda b, h, qi, *_: (cb + b, off // width + h))

    def lat_kv(width, off):
        return pl.BlockSpec((seq, width), lambda b, h, qi, *_: (b, off // width + h))

    def ctx_kv(width, off):
        return pl.BlockSpec((ctx, width), lambda b, h, qi, *_: (cb + b, off // width + h))

    def ctx_out(width):
        return pl.BlockSpec((tq_ctx, width), lambda b, h, qi, *_: (b, h))

    oa = pl.pallas_call(
        functools.partial(_attn_a_lat_kernel, tq=tq_gqa, seq=seq),
        out_shape=jax.ShapeDtypeStruct((nl_rows, A_Q), bf16),
        grid_spec=pltpu.PrefetchScalarGridSpec(
            num_scalar_prefetch=1,
            grid=(nbatch, A_KV_HEADS, seq // tq_gqa),
            in_specs=[lat_q(tq_gqa, gw, QK_AQ), lat_kv(HEAD_DIM, QK_AK), ctx_kv(HEAD_DIM, QK_AK),
                      lat_kv(HEAD_DIM, V_AV), ctx_kv(HEAD_DIM, V_AV)],
            out_specs=lat_q(tq_gqa, gw, 0),
        ),
        compiler_params=_cparams(sem, 40),
        name="attn_window",
    )(a_sink, qk, qk, qk, v, v)
    if with_ctx:
        oa_ctx = pl.pallas_call(
            functools.partial(_attn_a_ctx_kernel, tq=tq_ctx),
            out_shape=jax.ShapeDtypeStruct((nc_rows, A_Q), bf16),
            grid_spec=pltpu.PrefetchScalarGridSpec(
                num_scalar_prefetch=1,
                grid=(nbatch, A_KV_HEADS, 1),
                in_specs=[ctx_q(gw, QK_AQ), ctx_kv(HEAD_DIM, QK_AK), ctx_kv(HEAD_DIM, V_AV)],
                out_specs=ctx_out(gw),
            ),
            compiler_params=_cparams(sem, 32),
            name="attn_window_ctx",
        )(a_sink, qk, qk, v)
        oa = jnp.concatenate([oa, oa_ctx], axis=0)

    m_rows = GQA_GROUP * tq_gqa
    oc = pl.pallas_call(
        functools.partial(_attn_c_lat_kernel, tk=tk, n_chunks=seq // tk),
        out_shape=jax.ShapeDtypeStruct((nl_rows, C_Q), bf16),
        grid=(nbatch, C_KV_HEADS, seq // tq_gqa),
        in_specs=[lat_q(tq_gqa, gw, QK_CQ), lat_kv(HEAD_DIM, QK_CK), ctx_kv(HEAD_DIM, QK_CK),
                  lat_kv(2 * HEAD_DIM, V_CV), ctx_kv(2 * HEAD_DIM, V_CV)],
        out_specs=lat_q(tq_gqa, gw, 0),
        scratch_shapes=[pltpu.VMEM((m_rows, LANES), f32), pltpu.VMEM((m_rows, 2 * HEAD_DIM), f32)],
        compiler_params=_cparams(sem, 40),
        name="attn_global",
    )(qk, qk, qk, v, v)
    if with_ctx:
        oc_ctx = pl.pallas_call(
            _attn_c_ctx_kernel,
            out_shape=jax.ShapeDtypeStruct((nc_rows, C_Q), bf16),
            grid=(nbatch, C_KV_HEADS, 1),
            in_specs=[ctx_q(gw, QK_CQ), ctx_kv(HEAD_DIM, QK_CK),
                      pl.BlockSpec((ctx, HEAD_DIM), lambda b, h, qi: (cb + b, V_CV // HEAD_DIM + 2 * h))],
            out_specs=ctx_out(gw),
            compiler_params=_cparams(sem, 32),
            name="attn_global_ctx",
        )(qk, qk, v)
        oc = jnp.concatenate([oc, oc_ctx], axis=0)

    lam_spec = pl.BlockSpec((4, HEAD_DIM), lambda b, h, qi: (0, 0))
    g_spec = pl.BlockSpec((1, bw), lambda b, h, qi: (0, 0))
    g2d = b_subln.reshape(1, bw)
    ob = pl.pallas_call(
        functools.partial(_attn_b_lat_kernel, tk=tk, n_chunks=seq // tk, lam_init=lam_init),
        out_shape=jax.ShapeDtypeStruct((nl_rows, B_V), bf16),
        grid=(nbatch, B_HEADS, seq // tq_diff),
        in_specs=[lat_q(tq_diff, bw, QK_BQ), lat_kv(bw, QK_BK), ctx_kv(bw, QK_BK),
                  lat_kv(bw, V_BV), ctx_kv(bw, V_BV), lam_spec, g_spec],
        out_specs=lat_q(tq_diff, bw, 0),
        scratch_shapes=[pltpu.VMEM((2 * tq_diff, LANES), f32), pltpu.VMEM((2 * tq_diff, LANES), f32),
                        pltpu.VMEM((2 * tq_diff, bw), f32)],
        compiler_params=_cparams(sem, 48),
        name="attn_diff",
    )(qk, qk, qk, v, v, b_lam, g2d)
    if with_ctx:
        ob_ctx = pl.pallas_call(
            functools.partial(_attn_b_ctx_kernel, lam_init=lam_init),
            out_shape=jax.ShapeDtypeStruct((nc_rows, B_V), bf16),
            grid=(nbatch, B_HEADS, 1),
            in_specs=[ctx_q(bw, QK_BQ), ctx_kv(bw, QK_BK), ctx_kv(bw, V_BV), lam_spec, g_spec],
            out_specs=ctx_out(bw),
            compiler_params=_cparams(sem, 32),
            name="attn_diff_ctx",
        )(qk, qk, v, b_lam, g2d)
        ob = jnp.concatenate([ob, ob_ctx], axis=0)
    return oa, ob, oc


def _merge_kernel(oa_ref, ob_ref, oc_ref, w_ref, ga_ref, gb_ref, gc_ref, y_ref):
    y = (ga_ref[...].astype(f32) * _dot(oa_ref[...], w_ref[0])
         + gb_ref[...].astype(f32) * _dot(ob_ref[...], w_ref[1])
         + gc_ref[...].astype(f32) * _dot(oc_ref[...], w_ref[2]))
    y_ref[...] = y.astype(y_ref.dtype)


def _merge_call(oa, ob, oc, wb, gates, tm, tn):
    rows, bwid = oa.shape
    d = wb.shape[2]
    nj = d // tn
    o_spec = pl.BlockSpec((tm, bwid), lambda i, j: (i, 0))
    return pl.pallas_call(
        _merge_kernel,
        out_shape=jax.ShapeDtypeStruct((rows, d), bf16),
        grid=(rows // tm, nj),
        in_specs=[o_spec, o_spec, o_spec,
                  pl.BlockSpec((3, bwid, tn), lambda i, j: (0, 0, j)),
                  pl.BlockSpec((tm, tn), lambda i, j: (i, j)),
                  pl.BlockSpec((tm, tn), lambda i, j: (i, nj + j)),
                  pl.BlockSpec((tm, tn), lambda i, j: (i, 2 * nj + j))],
        out_specs=pl.BlockSpec((tm, tn), lambda i, j: (i, j)),
        compiler_params=_cparams(("arbitrary", "arbitrary"), 40),
        name="branch_merge",
    )(oa, ob, oc, wb, gates, gates, gates)


def _out_proj_kernel(y_ref, w_ref, x_ref, g1_ref, nw_ref, sc_ref, sh_ref, xo_ref, h_ref):
    xn = x_ref[...] + g1_ref[0] * _dot(y_ref[...], w_ref[...])
    xo_ref[...] = xn
    t = xn * lax.rsqrt(jnp.mean(xn * xn, axis=-1, keepdims=True) + EPS) * nw_ref[...]
    h_ref[...] = (t * (1.0 + sc_ref[0]) + sh_ref[0]).astype(h_ref.dtype)


def _out_proj_call(y, w_out, x, modl, nw2, seg_of_block, tm, h_dtype):
    rows, d = y.shape

    def mod_spec(k):
        return pl.BlockSpec((1, 1, d), lambda i: (seg_of_block(i) * 6 + k, 0, 0))

    row_spec = pl.BlockSpec((tm, d), lambda i: (i, 0))
    return pl.pallas_call(
        _out_proj_kernel,
        out_shape=(jax.ShapeDtypeStruct((rows, d), f32), jax.ShapeDtypeStruct((rows, d), h_dtype)),
        grid=(rows // tm,),
        in_specs=[row_spec, pl.BlockSpec((d, d), lambda i: (0, 0)), row_spec,
                  mod_spec(2), pl.BlockSpec((1, d), lambda i: (0, 0)), mod_spec(4), mod_spec(3)],
        out_specs=(row_spec, row_spec),
        compiler_params=_cparams(("arbitrary",), 48),
        name="out_proj_residual_norm",
    )(y, w_out, x, modl, nw2.reshape(1, d), modl, modl)


def _dense_ffn_kernel(h_ref, w1_ref, w3_ref, w2_ref, x_ref, g2_ref, o_ref, acc_ref):
    j = pl.program_id(1)

    @pl.when(j == 0)
    def _():
        acc_ref[...] = jnp.zeros_like(acc_ref)

    h = h_ref[...]
    a = _dot(h, w1_ref[...])
    b = _dot(h, w3_ref[...])
    g = (a * _sigmoid(a) * b).astype(bf16)
    acc_ref[...] += _dot(g, w2_ref[...])

    @pl.when(j == pl.num_programs(1) - 1)
    def _():
        o_ref[...] = x_ref[...] + g2_ref[0] * acc_ref[...]


def _dense_ffn_call(h2, w1, w3, w2, x, modl, seg_of_block, tm, tf):
    rows, d = h2.shape
    ff = w1.shape[1]
    row_spec = pl.BlockSpec((tm, d), lambda i, j: (i, 0))
    return pl.pallas_call(
        _dense_ffn_kernel,
        out_shape=jax.ShapeDtypeStruct((rows, d), f32),
        grid=(rows // tm, ff // tf),
        in_specs=[row_spec,
                  pl.BlockSpec((d, tf), lambda i, j: (0, j)),
                  pl.BlockSpec((d, tf), lambda i, j: (0, j)),
                  pl.BlockSpec((tf, d), lambda i, j: (j, 0)),
                  row_spec,
                  pl.BlockSpec((1, 1, d), lambda i, j: (seg_of_block(i) * 6 + 5, 0, 0))],
        out_specs=row_spec,
        scratch_shapes=[pltpu.VMEM((tm, d), f32)],
        compiler_params=_cparams(("arbitrary", "arbitrary"), 52),
        name="dense_swiglu",
    )(h2, w1, w3, w2, x, modl)


def _router_kernel(h_ref, w_ref, idx_ref, wt_ref, *, n_experts):
    logits = jnp.dot(h_ref[...], w_ref[...], preferred_element_type=f32,
                     precision=lax.Precision.HIGHEST)
    lane = lax.broadcasted_iota(jnp.int32, logits.shape, 1)
    lg = jnp.where(lane < n_experts, logits, -jnp.inf)
    m1 = _rowmax(lg)
    i1 = jnp.min(jnp.where(lg == m1, lane, LANES), axis=-1, keepdims=True)
    lg2 = jnp.where(lane == i1, -jnp.inf, lg)
    m2 = _rowmax(lg2)
    i2 = jnp.min(jnp.where(lg2 == m2, lane, LANES), axis=-1, keepdims=True)
    e2 = jnp.exp(m2 - m1)
    w1 = 1.0 / (1.0 + e2)
    w2 = e2 / (1.0 + e2)
    idx_ref[...] = jnp.where(lane == 0, i1, jnp.where(lane == 1, i2, 0))
    wt_ref[...] = jnp.where(lane == 0, w1, jnp.where(lane == 1, w2, 0.0))


def _router_call(h2, w_router, tm):
    n, d = h2.shape
    e = w_router.shape[1]
    wp = jnp.zeros((d, LANES), f32).at[:, :e].set(w_router)
    return pl.pallas_call(
        functools.partial(_router_kernel, n_experts=e),
        out_shape=(jax.ShapeDtypeStruct((n, LANES), jnp.int32), jax.ShapeDtypeStruct((n, LANES), f32)),
        grid=(n // tm,),
        in_specs=[pl.BlockSpec((tm, d), lambda i: (i, 0)), pl.BlockSpec((d, LANES), lambda i: (0, 0))],
        out_specs=(pl.BlockSpec((tm, LANES), lambda i: (i, 0)), pl.BlockSpec((tm, LANES), lambda i: (i, 0))),
        compiler_params=_cparams(("arbitrary",), 32),
        name="moe_router_top2",
    )(h2, wp)


def _gather_rows_kernel(src_ref, x_hbm, o_ref, buf, sem, *, chunk):
    base = pl.program_id(0) * chunk

    def row_copy(r):
        return pltpu.make_async_copy(x_hbm.at[pl.ds(src_ref[base + r], 1)], buf.at[pl.ds(r, 1)], sem)

    def issue(r, c):
        row_copy(r).start()
        return c

    def drain(r, c):
        row_copy(r).wait()
        return c

    lax.fori_loop(0, chunk, issue, 0, unroll=8)
    lax.fori_loop(0, chunk, drain, 0, unroll=8)
    o_ref[...] = buf[...].astype(o_ref.dtype)


def _gather_rows_call(x, src, chunk):
    cap = src.shape[0]
    d = x.shape[1]
    return pl.pallas_call(
        functools.partial(_gather_rows_kernel, chunk=chunk),
        out_shape=jax.ShapeDtypeStruct((cap, d), bf16),
        grid_spec=pltpu.PrefetchScalarGridSpec(
            num_scalar_prefetch=1,
            grid=(cap // chunk,),
            in_specs=[pl.BlockSpec(memory_space=pl.ANY)],
            out_specs=pl.BlockSpec((chunk, d), lambda i, s: (i, 0)),
            scratch_shapes=[pltpu.VMEM((chunk, d), x.dtype), pltpu.SemaphoreType.DMA(())],
        ),
        compiler_params=_cparams(("arbitrary",), 32),
        name="moe_gather_rows",
    )(src, x)


def _expert_kernel(be_ref, nu_ref, x_ref, w1_ref, w3_ref, w2_ref, o_ref, acc_ref):
    i = pl.program_id(0)
    j = pl.program_id(1)

    @pl.when(i < nu_ref[0])
    def _():
        @pl.when(j == 0)
        def _():
            acc_ref[...] = jnp.zeros_like(acc_ref)

        h = x_ref[...]
        a = _dot(h, w1_ref[0])
        b = _dot(h, w3_ref[0])
        g = (a * _sigmoid(a) * b).astype(bf16)
        acc_ref[...] += _dot(g, w2_ref[0])

        @pl.when(j == pl.num_programs(1) - 1)
        def _():
            o_ref[...] = acc_ref[...]

    @pl.when(jnp.logical_and(i >= nu_ref[0], j == pl.num_programs(1) - 1))
    def _():
        o_ref[...] = jnp.zeros_like(o_ref)


def _expert_call(xs, blk_e, n_used, w1, w3, w2, tm, tf):
    cap, d = xs.shape
    ff = w1.shape[2]
    nf = ff // tf

    def bi(i, nu):
        return jnp.minimum(i, nu[0] - 1)

    def fj(i, j, nu):
        return jnp.where(i < nu[0], j, nf - 1)

    return pl.pallas_call(
        _expert_kernel,
        out_shape=jax.ShapeDtypeStruct((cap, d), f32),
        grid_spec=pltpu.PrefetchScalarGridSpec(
            num_scalar_prefetch=2,
            grid=(cap // tm, nf),
            in_specs=[pl.BlockSpec((tm, d), lambda i, j, be, nu: (bi(i, nu), 0)),
                      pl.BlockSpec((1, d, tf), lambda i, j, be, nu: (be[bi(i, nu)], 0, fj(i, j, nu))),
                      pl.BlockSpec((1, d, tf), lambda i, j, be, nu: (be[bi(i, nu)], 0, fj(i, j, nu))),
                      pl.BlockSpec((1, tf, d), lambda i, j, be, nu: (be[bi(i, nu)], fj(i, j, nu), 0))],
            out_specs=pl.BlockSpec((tm, d), lambda i, j, be, nu: (i, 0)),
            scratch_shapes=[pltpu.VMEM((tm, d), f32)],
        ),
        compiler_params=_cparams(("arbitrary", "arbitrary"), 52),
        name="moe_expert_swiglu",
    )(blk_e, n_used, xs, w1, w3, w2)


def _combine_kernel(dest_ref, y_hbm, x_ref, g2_ref, wt_ref, o_ref, buf0, buf1, sem, *, tm):
    base = pl.program_id(0) * tm

    def copies(r):
        s = (base + r) * TOP_K
        return (pltpu.make_async_copy(y_hbm.at[pl.ds(dest_ref[s], 1)], buf0.at[pl.ds(r, 1)], sem),
                pltpu.make_async_copy(y_hbm.at[pl.ds(dest_ref[s + 1], 1)], buf1.at[pl.ds(r, 1)], sem))

    def issue(r, c):
        c0, c1 = copies(r)
        c0.start()
        c1.start()
        return c

    def drain(r, c):
        c0, c1 = copies(r)
        c0.wait()
        c1.wait()
        return c

    lax.fori_loop(0, tm, issue, 0, unroll=8)
    lax.fori_loop(0, tm, drain, 0, unroll=8)
    wt = wt_ref[...]
    mix = wt[:, 0:1] * buf0[...] + wt[:, 1:2] * buf1[...]
    o_ref[...] = x_ref[...] + g2_ref[0] * mix


def _combine_call(yb, dest, x, modl, wts, seg_of_block, tm):
    n, d = x.shape
    return pl.pallas_call(
        functools.partial(_combine_kernel, tm=tm),
        out_shape=jax.ShapeDtypeStruct((n, d), f32),
        grid_spec=pltpu.PrefetchScalarGridSpec(
            num_scalar_prefetch=1,
            grid=(n // tm,),
            in_specs=[pl.BlockSpec(memory_space=pl.ANY),
                      pl.BlockSpec((tm, d), lambda i, dst: (i, 0)),
                      pl.BlockSpec((1, 1, d), lambda i, dst: (seg_of_block(i) * 6 + 5, 0, 0)),
                      pl.BlockSpec((tm, LANES), lambda i, dst: (i, 0))],
            out_specs=pl.BlockSpec((tm, d), lambda i, dst: (i, 0)),
            scratch_shapes=[pltpu.VMEM((tm, d), f32), pltpu.VMEM((tm, d), f32),
                            pltpu.SemaphoreType.DMA(())],
        ),
        compiler_params=_cparams(("arbitrary",), 32),
        name="moe_combine_residual",
    )(dest, yb, x, modl, wts)


def _moe_call(h2, x, modl, w_router, w1, w3, w2, seg_of_block):
    n, d = h2.shape
    n_exp = w_router.shape[1]
    tm = 512
    idx, wts = _router_call(h2, w_router, 256)
    flat_e = idx[:, :TOP_K].reshape(-1)
    n_slots = n * TOP_K
    onehot = (flat_e[:, None] == jnp.arange(n_exp, dtype=jnp.int32)[None, :]).astype(jnp.int32)
    csum = jnp.cumsum(onehot, axis=0)
    counts = csum[-1]
    rank = jnp.sum((csum - 1) * onehot, axis=1)
    padded = (counts + tm - 1) // tm * tm
    pend = jnp.cumsum(padded)
    pstart = pend - padded
    dest = (pstart[flat_e] + rank).astype(jnp.int32)
    cap = (n_slots // tm + n_exp) * tm
    src = jnp.zeros((cap,), jnp.int32).at[dest].set(jnp.arange(n_slots, dtype=jnp.int32) // TOP_K)
    n_blocks = cap // tm
    blk_start = jnp.arange(n_blocks, dtype=jnp.int32) * tm
    blk_e = jnp.minimum(jnp.searchsorted(pend, blk_start, side='right'), n_exp - 1).astype(jnp.int32)
    n_used = (pend[-1:] // tm).astype(jnp.int32)

    xs = _gather_rows_call(h2, src, 256)
    yb = _expert_call(xs, blk_e, n_used, w1, w3, w2, tm, 512)
    return _combine_call(yb, dest, x, modl, wts, seg_of_block, 256)


def _rope_tables(nbatch, seq, ctx):
    t = jnp.arange(seq, dtype=jnp.int32)
    row = (t // GRID_W).astype(f32)
    col = (t % GRID_W).astype(f32)
    inv = ROPE_THETA ** (-jnp.arange(ROPE_PAIRS, dtype=f32) / ROPE_PAIRS)
    lane = jnp.arange(HEAD_DIM)
    pos = jnp.where((lane // (2 * ROPE_PAIRS))[None, :] == 0, row[:, None], col[:, None])
    ang = pos * inv[lane % ROPE_PAIRS][None, :]
    first = ((lane % (2 * ROPE_PAIRS)) < ROPE_PAIRS)[None, :]
    cos = jnp.cos(ang)
    sin = jnp.sin(ang)
    sa = jnp.where(first, -sin, 0.0)
    sb = jnp.where(first, 0.0, sin)
    nctx = nbatch * ctx

    def full(tab, fill):
        return jnp.concatenate([jnp.tile(tab, (nbatch, 1)), jnp.full((nctx, HEAD_DIM), fill, f32)], axis=0)

    return full(cos, 1.0), full(sa, 0.0), full(sb, 0.0)


def kernel(x, c, ctx, c_ctx, w_mod, b_mod, norm1, norm2, w_in, qk_gain, a_sink, b_lambda, b_subln,
           w_branch, w_out, dense_w1, dense_w3, dense_w2, moe_router, moe_w1, moe_w3, moe_w2):
    nbatch, seq, d = x.shape
    nctx_len = ctx.shape[1]
    depth = w_mod.shape[0]
    nl = nbatch * seq
    nc = nbatch * nctx_len
    tm = 512

    def seg_of_block_for(t):
        return lambda i: jnp.minimum(i // (seq // t), nbatch)

    xs = jnp.concatenate([x.reshape(nl, d), ctx.reshape(nc, d)], axis=0)
    cc = jnp.zeros((8, d), f32).at[:nbatch].set(c).at[nbatch].set(c_ctx)
    mod = _mod_call(cc, w_mod, b_mod)
    cos, sa, sb = _rope_tables(nbatch, seq, nctx_len)
    scale = HEAD_DIM ** -0.5
    ones_cols = jnp.ones((HEAD_DIM,), f32)
    zero_w = jnp.zeros((d, HEAD_DIM), bf16)

    for l in range(depth):
        with_ctx = l < depth - 1
        lam_init = 0.8 - 0.6 * math.exp(-0.3 * l)
        rows = nl + nc if with_ctx else nl
        modl = mod[l].reshape(8 * 6, 1, d)

        wl = w_in[l]
        o = 0
        parts = {}
        for name, width in (("aq", A_Q), ("ak", A_KV), ("av", A_KV), ("bq", B_QK), ("bk", B_QK),
                            ("bv", B_V), ("cq", C_Q), ("ck", C_KV), ("cv", C_KV), ("g", 3 * d)):
            parts[name] = wl[:, o:o + width].astype(bf16)
            o += width
        w_qk = jnp.concatenate([parts[k] for k in ("cq", "aq", "bq", "bk", "ck", "ak")], axis=1)
        cv = parts["cv"]
        cv_cols = []
        for hd in range(C_KV_HEADS):
            cv_cols += [cv[:, hd * HEAD_DIM:(hd + 1) * HEAD_DIM], zero_w]
        w_v = jnp.concatenate([parts["bv"]] + cv_cols + [parts["av"]], axis=1)
        zeros_bv = jnp.zeros((B_V,), f32)
        zeros_h = jnp.zeros((HEAD_DIM,), f32)
        bias_v = jnp.concatenate([zeros_bv] + [zeros_h, ones_cols] * C_KV_HEADS
                                 + [jnp.zeros((A_KV,), f32)]).reshape(1, V_COLS)
        w_g = parts["g"]
        g = qk_gain[l]
        gain = jnp.concatenate([jnp.tile(g[4] * (scale * LOG2E), C_HEADS), jnp.tile(g[0] * scale, A_HEADS),
                                jnp.tile(g[2] * (scale * LOG2E), 2 * B_HEADS), jnp.tile(g[3], 2 * B_HEADS),
                                jnp.tile(g[5], C_KV_HEADS), jnp.tile(g[1], A_KV_HEADS)]).reshape(1, QK_COLS)

        h = _norm_call(xs, norm1[l], modl, 1, 0, seg_of_block_for(tm), tm, bf16)
        tab_spec = pl.BlockSpec((tm, HEAD_DIM), lambda i, j: (i, 0))
        qk = _proj_call(_qk_proj_kernel, h, w_qk, (gain, cos, sa, sb),
                        [pl.BlockSpec((1, 512), lambda i, j: (0, j)), tab_spec, tab_spec, tab_spec],
                        nl + nc, tm, 512, "qk_proj_norm_rope")
        tn_v = V_COLS // 2
        v = _proj_call(_bias_proj_kernel, h, w_v, (bias_v,), [pl.BlockSpec((1, tn_v), lambda i, j: (0, j))],
                       nl + nc, tm, tn_v, "v_proj")
        gates = _proj_call(_gate_proj_kernel, h, w_g, (), [], rows, tm, 512, "gate_proj")

        oa, ob, oc = _attention(qk, v, a_sink[l], b_lambda[l], b_subln[l], nbatch=nbatch, seq=seq,
                                ctx=nctx_len, with_ctx=with_ctx, lam_init=lam_init)
        y = _merge_call(oa, ob, oc, w_branch[l].astype(bf16), gates, tm, 512)
        moe_layer = l % 2 == 1
        xn, h2 = _out_proj_call(y, w_out[l].astype(bf16), xs, modl, norm2[l], seg_of_block_for(256), 256,
                                f32 if moe_layer else bf16)
        i = l // 2
        if not moe_layer:
            xs = _dense_ffn_call(h2, dense_w1[i].astype(bf16), dense_w3[i].astype(bf16),
                                 dense_w2[i].astype(bf16), xn, modl, seg_of_block_for(tm), tm, 512)
        else:
            xs = _moe_call(h2, xn, modl, moe_router[i], moe_w1[i].astype(bf16), moe_w3[i].astype(bf16),
                           moe_w2[i].astype(bf16), seg_of_block_for(256))
    return xs[:nl].reshape(nbatch, seq, d)
```

```python
import functools
import math

import jax
import jax.numpy as jnp
from jax import lax
from jax.experimental import pallas as pl
from jax.experimental.pallas import tpu as pltpu

f32 = jnp.float32
bf16 = jnp.bfloat16

HEAD_DIM = 128
GRID_W = 64
ROPE_PAIRS = HEAD_DIM // 4
ROPE_THETA = 10000.0
EPS = 1e-6
NEG_INF = -1e30
WINDOW = 128
A_HEADS, A_KV_HEADS = 8, 2
B_HEADS = 4
C_HEADS, C_KV_HEADS = 8, 2
GQA_GROUP = 4
TOP_K = 2
LANES = 128
MXU_WIDTH = 256
MIB = 1024 * 1024
LOG2E = math.log2(math.e)

A_Q, A_KV = A_HEADS * HEAD_DIM, A_KV_HEADS * HEAD_DIM
B_QK, B_V = B_HEADS * 2 * HEAD_DIM, B_HEADS * 2 * HEAD_DIM
C_Q, C_KV = C_HEADS * HEAD_DIM, C_KV_HEADS * HEAD_DIM
QK_CQ, QK_AQ, QK_BQ, QK_BK = 0, C_Q, C_Q + A_Q, C_Q + A_Q + B_QK
QK_CK = QK_BK + B_QK
QK_AK = QK_CK + C_KV
QK_COLS = QK_AK + A_KV
V_BV, V_CV = 0, B_V
V_AV = V_CV + C_KV_HEADS * 2 * HEAD_DIM
V_COLS = V_AV + A_KV


def _cparams(sem, vmem_mib):
    return pltpu.CompilerParams(dimension_semantics=sem, vmem_limit_bytes=vmem_mib * MIB)


def _dot(a, b):
    return jnp.dot(a, b, preferred_element_type=f32)


def _dot_nt(a, b):
    return lax.dot_general(a, b, (((1,), (1,)), ((), ())), preferred_element_type=f32)


def _sigmoid(x):
    return 0.5 * jnp.tanh(0.5 * x) + 0.5


def _rowmax(s):
    return jnp.max(s, axis=-1, keepdims=True)


def _rowsum(s):
    return jnp.sum(s, axis=-1, keepdims=True)


def _lane_tile(x, n):
    return jnp.concatenate([x] * n, axis=1)


def _mod_kernel(c_ref, w_ref, b_ref, o_ref):
    c = c_ref[...]
    a = (c * _sigmoid(c)).astype(bf16)
    o_ref[0] = _dot(a, w_ref[0].astype(bf16)) + b_ref[0]


def _mod_call(cc, w_mod, b_mod):
    depth, d, n = w_mod.shape
    tn = 1024
    return pl.pallas_call(
        _mod_kernel,
        out_shape=jax.ShapeDtypeStruct((depth, 8, n), f32),
        grid=(depth, n // tn),
        in_specs=[pl.BlockSpec((8, d), lambda l, j: (0, 0)),
                  pl.BlockSpec((1, d, tn), lambda l, j: (l, 0, j)),
                  pl.BlockSpec((1, 1, tn), lambda l, j: (l, 0, j))],
        out_specs=pl.BlockSpec((1, 8, tn), lambda l, j: (l, 0, j)),
        compiler_params=_cparams(("arbitrary", "arbitrary"), 40),
        name="mod_vectors",
    )(cc, w_mod, b_mod.reshape(depth, 1, n))


def _norm_kernel(x_ref, nw_ref, sc_ref, sh_ref, o_ref):
    x = x_ref[...]
    y = x * lax.rsqrt(jnp.mean(x * x, axis=-1, keepdims=True) + EPS) * nw_ref[...]
    o_ref[...] = (y * (1.0 + sc_ref[0]) + sh_ref[0]).astype(o_ref.dtype)


def _norm_call(x, nw, modl, k_sc, k_sh, seg_of_block, tm, out_dtype):
    r, d = x.shape
    return pl.pallas_call(
        _norm_kernel,
        out_shape=jax.ShapeDtypeStruct((r, d), out_dtype),
        grid=(r // tm,),
        in_specs=[pl.BlockSpec((tm, d), lambda i: (i, 0)),
                  pl.BlockSpec((1, d), lambda i: (0, 0)),
                  pl.BlockSpec((1, 1, d), lambda i: (seg_of_block(i) * 6 + k_sc, 0, 0)),
                  pl.BlockSpec((1, 1, d), lambda i: (seg_of_block(i) * 6 + k_sh, 0, 0))],
        out_specs=pl.BlockSpec((tm, d), lambda i: (i, 0)),
        compiler_params=_cparams(("arbitrary",), 32),
        name="norm_modulate",
    )(x, nw.reshape(1, d), modl, modl)


def _qk_epilogue(acc_ref, o_ref, gain_ref, cos_ref, sin_ref, ones_ref):
    cos, sin = cos_ref[...], sin_ref[...]
    for hd in range(o_ref.shape[1] // HEAD_DIM):
        sl = slice(hd * HEAD_DIM, (hd + 1) * HEAD_DIM)
        t = acc_ref[:, sl]
        ss = _dot((t * t).astype(bf16), ones_ref[...])
        y = t * lax.rsqrt(ss * (1.0 / HEAD_DIM) + EPS) * gain_ref[:, sl]
        o_ref[:, sl] = (y * cos + pltpu.roll(y, HEAD_DIM // 2, axis=1) * sin).astype(o_ref.dtype)


def _gate_epilogue(acc_ref, o_ref):
    o_ref[...] = _sigmoid(acc_ref[...]).astype(o_ref.dtype)


def _deferred_mm_kernel(h_ref, w_ref, *rest, epilogue, n_extra):
    extra, o_ref, acc_ref = rest[:n_extra], rest[n_extra], rest[n_extra + 1]
    t = pl.program_id(0)

    @pl.when(t == 0)
    def _():
        acc_ref[1] = jnp.zeros(acc_ref.shape[1:], f32)

    for slot in (0, 1):
        @pl.when(lax.rem(t, 2) == slot)
        def _(slot=slot):
            epilogue(acc_ref.at[1 - slot], o_ref, *extra)
            acc_ref[slot] = _dot(h_ref[...], w_ref[...])


def _deferred_mm_call(epilogue, h, w, extra, extra_specs, nrows, tm, tn, name):
    k = h.shape[1]
    n = w.shape[1]
    nj = n // tn
    n_tiles = (nrows // tm) * nj

    def cur(t):
        tt = jnp.minimum(t, n_tiles - 1)
        return tt // nj, tt % nj

    def prev(t):
        tp = jnp.maximum(t - 1, 0)
        return tp // nj, tp % nj

    return pl.pallas_call(
        functools.partial(_deferred_mm_kernel, epilogue=epilogue, n_extra=len(extra)),
        out_shape=jax.ShapeDtypeStruct((nrows, n), bf16),
        grid=(n_tiles + 1,),
        in_specs=[pl.BlockSpec((tm, k), lambda t: (cur(t)[0], 0)),
                  pl.BlockSpec((k, tn), lambda t: (0, cur(t)[1]))]
                 + [pl.BlockSpec(shape, functools.partial(lambda t, f: f(*prev(t)), f=f))
                    for shape, f in extra_specs],
        out_specs=pl.BlockSpec((tm, tn), lambda t: prev(t)),
        scratch_shapes=[pltpu.VMEM((2, tm, tn), f32)],
        compiler_params=_cparams(("arbitrary",), 40),
        name=name,
    )(h, w, *extra)


def _bias_proj_kernel(h_ref, w_ref, b_ref, o_ref):
    o_ref[...] = (_dot(h_ref[...], w_ref[...]) + b_ref[...]).astype(o_ref.dtype)


def _proj_call(kernel, h, w, extra, extra_specs, nrows, tm, tn, name):
    k = h.shape[1]
    n = w.shape[1]
    return pl.pallas_call(
        kernel,
        out_shape=jax.ShapeDtypeStruct((nrows, n), bf16),
        grid=(nrows // tm, n // tn),
        in_specs=[pl.BlockSpec((tm, k), lambda i, j: (i, 0)),
                  pl.BlockSpec((k, tn), lambda i, j: (0, j))] + extra_specs,
        out_specs=pl.BlockSpec((tm, tn), lambda i, j: (i, j)),
        compiler_params=_cparams(("arbitrary", "arbitrary"), 40),
        name=name,
    )(h, w, *extra)


def _stack_heads(q, n):
    return jnp.concatenate([q[:, g * HEAD_DIM:(g + 1) * HEAD_DIM] for g in range(n)], axis=0)


def _unstack_heads(o, n):
    t = o.shape[0] // n
    return jnp.concatenate([o[g * t:(g + 1) * t] for g in range(n)], axis=1)


def _softmax_pv(s, v, ex, extra_logit=None):
    m = _rowmax(s)
    if extra_logit is not None:
        m = jnp.maximum(m, extra_logit)
    p = ex(s - m)
    l = _rowsum(p)
    if extra_logit is not None:
        l = l + ex(extra_logit - m)
    return _dot(p.astype(bf16), v) / l


def _sink_column(sink_ref, kh, tq):
    return jnp.concatenate(
        [jnp.full((tq, 1), sink_ref[kh * GQA_GROUP + g], f32) for g in range(GQA_GROUP)], axis=0)


def _diff_finish(on0, on1, lam_ref, g_ref, lam_init):
    bl = lam_ref[...]
    lam = (jnp.exp(_rowsum(bl[0:1] * bl[1:2])) - jnp.exp(_rowsum(bl[2:3] * bl[3:4])) + lam_init)
    o = on0 - lam * on1
    y = o * lax.rsqrt(jnp.mean(o * o, axis=-1, keepdims=True) + EPS) * g_ref[...]
    return y * (1.0 - lam_init)


def _attn_a_lat_kernel(sink_ref, q_ref, kl_ref, kc_ref, vl_ref, vc_ref, o_ref, *, tq, seq):
    kh = pl.program_id(1)
    qi = pl.program_id(2)
    qs = _stack_heads(q_ref[...], GQA_GROUP)
    q0 = qi * tq
    band = tq + 2 * WINDOW
    start = pl.multiple_of(jnp.clip(q0 - WINDOW, 0, seq - band), WINDOW)
    kb = kl_ref[pl.ds(start, band), :]
    vb = vl_ref[pl.ds(start, band), :]
    s_ctx = _dot_nt(qs, kc_ref[...])
    s_loc = _dot_nt(qs, kb)
    m_rows = GQA_GROUP * tq
    qpos = q0 + (lax.broadcasted_iota(jnp.int32, (m_rows, band), 0) & (tq - 1))
    kpos = start + lax.broadcasted_iota(jnp.int32, (m_rows, band), 1)
    s_loc = jnp.where(jnp.abs(qpos - kpos) <= WINDOW, s_loc, NEG_INF)
    sink = _sink_column(sink_ref, kh, tq)
    m = jnp.maximum(jnp.maximum(_rowmax(s_ctx), _rowmax(s_loc)), sink)
    p_ctx = jnp.exp(s_ctx - m)
    p_loc = jnp.exp(s_loc - m)
    l = _rowsum(p_ctx) + _rowsum(p_loc) + jnp.exp(sink - m)
    o = (_dot(p_ctx.astype(bf16), vc_ref[...]) + _dot(p_loc.astype(bf16), vb)) / l
    o_ref[...] = _unstack_heads(o, GQA_GROUP).astype(o_ref.dtype)


def _attn_a_ctx_kernel(sink_ref, q_ref, kc_ref, vc_ref, o_ref, *, tq):
    qs = _stack_heads(q_ref[...], GQA_GROUP)
    o = _softmax_pv(_dot_nt(qs, kc_ref[...]), vc_ref[...], jnp.exp,
                    _sink_column(sink_ref, pl.program_id(1), tq))
    o_ref[...] = _unstack_heads(o, GQA_GROUP).astype(o_ref.dtype)


def _attn_c_lat_kernel(q_ref, kl_ref, kc_ref, vl_ref, vc_ref, o_ref, m_ref, acc_ref, *, tk, n_chunks):
    qs = _stack_heads(q_ref[...], GQA_GROUP)
    s = _dot_nt(qs, kc_ref[...])
    m0 = _rowmax(s)
    m_ref[...] = jnp.broadcast_to(m0, m_ref.shape)
    acc_ref[...] = _dot(jnp.exp2(s - m0).astype(bf16), vc_ref[...])

    def chunk(c, carry):
        off = pl.multiple_of(c * tk, tk)
        s = _dot_nt(qs, kl_ref[pl.ds(off, tk), :])
        m_old = m_ref[...]
        m_new = jnp.maximum(m_old, _rowmax(s))
        alpha = jnp.exp2(m_old - m_new)
        p = jnp.exp2(s - _lane_tile(m_new, tk // LANES))
        acc_ref[...] = (_lane_tile(alpha, 2) * acc_ref[...]
                        + _dot(p.astype(bf16), vl_ref[pl.ds(off, tk), :]))
        m_ref[...] = m_new
        return carry

    lax.fori_loop(0, n_chunks, chunk, 0, unroll=8)
    acc = acc_ref[...]
    o = acc[:, :HEAD_DIM] / acc[:, HEAD_DIM:]
    o_ref[...] = _unstack_heads(o, GQA_GROUP).astype(o_ref.dtype)


def _attn_c_ctx_kernel(q_ref, kc_ref, vc_ref, o_ref):
    qs = _stack_heads(q_ref[...], GQA_GROUP)
    o = _softmax_pv(_dot_nt(qs, kc_ref[...]), vc_ref[...], jnp.exp2)
    o_ref[...] = _unstack_heads(o, GQA_GROUP).astype(o_ref.dtype)


def _diff_scores(q, k):
    return jnp.concatenate([_dot_nt(q[:, :HEAD_DIM], k[:, :HEAD_DIM]),
                            _dot_nt(q[:, HEAD_DIM:], k[:, HEAD_DIM:])], axis=0)


def _attn_b_lat_kernel(q_ref, kl_ref, kc_ref, vl_ref, vc_ref, lam_ref, g_ref, o_ref,
                       m_ref, l_ref, acc_ref, *, tk, n_chunks, lam_init):
    q = q_ref[...]
    tq = q.shape[0]
    s = _diff_scores(q, kc_ref[...])
    m0 = _rowmax(s)
    p = jnp.exp2(s - m0)
    m_ref[...] = jnp.broadcast_to(m0, m_ref.shape)
    l_ref[...] = jnp.broadcast_to(_rowsum(p), l_ref.shape)
    acc_ref[...] = _dot(p.astype(bf16), vc_ref[...])

    def chunk(c, carry):
        off = pl.multiple_of(c * tk, tk)
        s = _diff_scores(q, kl_ref[pl.ds(off, tk), :])
        m_old = m_ref[...]
        m_new = jnp.maximum(m_old, _rowmax(s))
        alpha = jnp.exp2(m_old - m_new)
        p = jnp.exp2(s - _lane_tile(m_new, tk // LANES))
        l_ref[...] = alpha * l_ref[...] + _rowsum(p)
        acc_ref[...] = (_lane_tile(alpha, 2) * acc_ref[...]
                        + _dot(p.astype(bf16), vl_ref[pl.ds(off, tk), :]))
        m_ref[...] = m_new
        return carry

    lax.fori_loop(0, n_chunks, chunk, 0, unroll=8)
    on = acc_ref[...] / _lane_tile(l_ref[...], 2)
    o_ref[...] = _diff_finish(on[:tq], on[tq:], lam_ref, g_ref, lam_init).astype(o_ref.dtype)


def _attn_b_ctx_kernel(q_ref, kc_ref, vc_ref, lam_ref, g_ref, o_ref, *, lam_init):
    q = q_ref[...]
    tq = q.shape[0]
    on = _softmax_pv(_diff_scores(q, kc_ref[...]), vc_ref[...], jnp.exp2)
    o_ref[...] = _diff_finish(on[:tq], on[tq:], lam_ref, g_ref, lam_init).astype(o_ref.dtype)


def _attention(qk, v, a_sink, b_lam, b_subln, *, nbatch, seq, ctx, with_ctx, lam_init):
    tq_gqa = 256
    tq_diff = min(512, seq)
    tq_ctx = ctx
    tk = min(512, seq)
    nl_rows = nbatch * seq
    nc_rows = nbatch * ctx
    cb = nl_rows // ctx
    gw = GQA_GROUP * HEAD_DIM
    bw = 2 * HEAD_DIM
    sem = ("arbitrary", "arbitrary", "arbitrary")

    def lat_q(tq, width, off):
        return pl.BlockSpec((tq, width), lambda b, h, qi, *_: (b * (seq // tq) + qi, off // width + h))

    def ctx_q(width, off):
        return pl.BlockSpec((tq_ctx, width), lambda b, h, qi, *_: (cb + b, off // width + h))

    def lat_kv(width, off):
        return pl.BlockSpec((seq, width), lambda b, h, qi, *_: (b, off // width + h))

    def ctx_kv(width, off):
        return pl.BlockSpec((ctx, width), lambda b, h, qi, *_: (cb + b, off // width + h))

    def ctx_out(width):
        return pl.BlockSpec((tq_ctx, width), lambda b, h, qi, *_: (b, h))

    oa = pl.pallas_call(
        functools.partial(_attn_a_lat_kernel, tq=tq_gqa, seq=seq),
        out_shape=jax.ShapeDtypeStruct((nl_rows, A_Q), bf16),
        grid_spec=pltpu.PrefetchScalarGridSpec(
            num_scalar_prefetch=1,
            grid=(nbatch, A_KV_HEADS, seq // tq_gqa),
            in_specs=[lat_q(tq_gqa, gw, QK_AQ), lat_kv(HEAD_DIM, QK_AK), ctx_kv(HEAD_DIM, QK_AK),
                      lat_kv(HEAD_DIM, V_AV), ctx_kv(HEAD_DIM, V_AV)],
            out_specs=lat_q(tq_gqa, gw, 0),
        ),
        compiler_params=_cparams(sem, 40),
        name="attn_window",
    )(a_sink, qk, qk, qk, v, v)
    if with_ctx:
        oa_ctx = pl.pallas_call(
            functools.partial(_attn_a_ctx_kernel, tq=tq_ctx),
            out_shape=jax.ShapeDtypeStruct((nc_rows, A_Q), bf16),
            grid_spec=pltpu.PrefetchScalarGridSpec(
                num_scalar_prefetch=1,
                grid=(nbatch, A_KV_HEADS, 1),
                in_specs=[ctx_q(gw, QK_AQ), ctx_kv(HEAD_DIM, QK_AK), ctx_kv(HEAD_DIM, V_AV)],
                out_specs=ctx_out(gw),
            ),
            compiler_params=_cparams(sem, 32),
            name="attn_window_ctx",
        )(a_sink, qk, qk, v)
        oa = jnp.concatenate([oa, oa_ctx], axis=0)

    m_rows = GQA_GROUP * tq_gqa
    oc = pl.pallas_call(
        functools.partial(_attn_c_lat_kernel, tk=tk, n_chunks=seq // tk),
        out_shape=jax.ShapeDtypeStruct((nl_rows, C_Q), bf16),
        grid=(nbatch, C_KV_HEADS, seq // tq_gqa),
        in_specs=[lat_q(tq_gqa, gw, QK_CQ), lat_kv(HEAD_DIM, QK_CK), ctx_kv(HEAD_DIM, QK_CK),
                  lat_kv(2 * HEAD_DIM, V_CV), ctx_kv(2 * HEAD_DIM, V_CV)],
        out_specs=lat_q(tq_gqa, gw, 0),
        scratch_shapes=[pltpu.VMEM((m_rows, LANES), f32), pltpu.VMEM((m_rows, 2 * HEAD_DIM), f32)],
        compiler_params=_cparams(sem, 40),
        name="attn_global",
    )(qk, qk, qk, v, v)
    if with_ctx:
        oc_ctx = pl.pallas_call(
            _attn_c_ctx_kernel,
            out_shape=jax.ShapeDtypeStruct((nc_rows, C_Q), bf16),
            grid=(nbatch, C_KV_HEADS, 1),
            in_specs=[ctx_q(gw, QK_CQ), ctx_kv(HEAD_DIM, QK_CK),
                      pl.BlockSpec((ctx, HEAD_DIM), lambda b, h, qi: (cb + b, V_CV // HEAD_DIM + 2 * h))],
            out_specs=ctx_out(gw),
            compiler_params=_cparams(sem, 32),
            name="attn_global_ctx",
        )(qk, qk, v)
        oc = jnp.concatenate([oc, oc_ctx], axis=0)

    lam_spec = pl.BlockSpec((4, HEAD_DIM), lambda b, h, qi: (0, 0))
    g_spec = pl.BlockSpec((1, bw), lambda b, h, qi: (0, 0))
    g2d = b_subln.reshape(1, bw)
    ob = pl.pallas_call(
        functools.partial(_attn_b_lat_kernel, tk=tk, n_chunks=seq // tk, lam_init=lam_init),
        out_shape=jax.ShapeDtypeStruct((nl_rows, B_V), bf16),
        grid=(nbatch, B_HEADS, seq // tq_diff),
        in_specs=[lat_q(tq_diff, bw, QK_BQ), lat_kv(bw, QK_BK), ctx_kv(bw, QK_BK),
                  lat_kv(bw, V_BV), ctx_kv(bw, V_BV), lam_spec, g_spec],
        out_specs=lat_q(tq_diff, bw, 0),
        scratch_shapes=[pltpu.VMEM((2 * tq_diff, LANES), f32), pltpu.VMEM((2 * tq_diff, LANES), f32),
                        pltpu.VMEM((2 * tq_diff, bw), f32)],
        compiler_params=_cparams(sem, 48),
        name="attn_diff",
    )(qk, qk, qk, v, v, b_lam, g2d)
    if with_ctx:
        ob_ctx = pl.pallas_call(
            functools.partial(_attn_b_ctx_kernel, lam_init=lam_init),
            out_shape=jax.ShapeDtypeStruct((nc_rows, B_V), bf16),
            grid=(nbatch, B_HEADS, 1),
            in_specs=[ctx_q(bw, QK_BQ), ctx_kv(bw, QK_BK), ctx_kv(bw, V_BV), lam_spec, g_spec],
            out_specs=ctx_out(bw),
            compiler_params=_cparams(sem, 32),
            name="attn_diff_ctx",
        )(qk, qk, v, b_lam, g2d)
        ob = jnp.concatenate([ob, ob_ctx], axis=0)
    return oa, ob, oc


def _merge_kernel(oa_ref, ob_ref, oc_ref, w_ref, ga_ref, gb_ref, gc_ref, y_ref):
    y = (ga_ref[...].astype(f32) * _dot(oa_ref[...], w_ref[0])
         + gb_ref[...].astype(f32) * _dot(ob_ref[...], w_ref[1])
         + gc_ref[...].astype(f32) * _dot(oc_ref[...], w_ref[2]))
    y_ref[...] = y.astype(y_ref.dtype)


def _merge_call(oa, ob, oc, wb, gates, tm, tn):
    rows, bwid = oa.shape
    d = wb.shape[2]
    nj = d // tn
    o_spec = pl.BlockSpec((tm, bwid), lambda i, j: (i, 0))
    return pl.pallas_call(
        _merge_kernel,
        out_shape=jax.ShapeDtypeStruct((rows, d), bf16),
        grid=(rows // tm, nj),
        in_specs=[o_spec, o_spec, o_spec,
                  pl.BlockSpec((3, bwid, tn), lambda i, j: (0, 0, j)),
                  pl.BlockSpec((tm, tn), lambda i, j: (i, j)),
                  pl.BlockSpec((tm, tn), lambda i, j: (i, nj + j)),
                  pl.BlockSpec((tm, tn), lambda i, j: (i, 2 * nj + j))],
        out_specs=pl.BlockSpec((tm, tn), lambda i, j: (i, j)),
        compiler_params=_cparams(("arbitrary", "arbitrary"), 40),
        name="branch_merge",
    )(oa, ob, oc, wb, gates, gates, gates)


def _out_proj_kernel(y_ref, w_ref, x_ref, g1_ref, nw_ref, sc_ref, sh_ref, xo_ref, h_ref):
    xn = x_ref[...] + g1_ref[0] * _dot(y_ref[...], w_ref[...])
    xo_ref[...] = xn
    t = xn * lax.rsqrt(jnp.mean(xn * xn, axis=-1, keepdims=True) + EPS) * nw_ref[...]
    h_ref[...] = (t * (1.0 + sc_ref[0]) + sh_ref[0]).astype(h_ref.dtype)


def _out_proj_call(y, w_out, x, modl, nw2, seg_of_block, tm, h_dtype):
    rows, d = y.shape

    def mod_spec(k):
        return pl.BlockSpec((1, 1, d), lambda i: (seg_of_block(i) * 6 + k, 0, 0))

    row_spec = pl.BlockSpec((tm, d), lambda i: (i, 0))
    return pl.pallas_call(
        _out_proj_kernel,
        out_shape=(jax.ShapeDtypeStruct((rows, d), f32), jax.ShapeDtypeStruct((rows, d), h_dtype)),
        grid=(rows // tm,),
        in_specs=[row_spec, pl.BlockSpec((d, d), lambda i: (0, 0)), row_spec,
                  mod_spec(2), pl.BlockSpec((1, d), lambda i: (0, 0)), mod_spec(4), mod_spec(3)],
        out_specs=(row_spec, row_spec),
        compiler_params=_cparams(("arbitrary",), 48),
        name="out_proj_residual_norm",
    )(y, w_out, x, modl, nw2.reshape(1, d), modl, modl)


def _swiglu_stream_step(t, h_ref, w1_ref, w3_ref, w2_ref, acc_ref, a_ref, b_ref):
    for slot in (0, 1):
        @pl.when(lax.rem(t, 2) == slot)
        def _(slot=slot):
            h = h_ref[...]
            a_ref[slot] = _dot(h, w1_ref[...])
            a = a_ref[1 - slot]
            b = b_ref[1 - slot]
            g = (a * _sigmoid(a) * b).astype(bf16)
            acc_ref[...] += _dot(g, w2_ref[...])
            b_ref[slot] = _dot(h, w3_ref[...])


def _dense_ffn_kernel(h_ref, w1_ref, w3_ref, w2_ref, x_ref, g2_ref, o_ref, acc_ref, a_ref, b_ref, *, nf):
    t = pl.program_id(0)
    jp = lax.rem(jnp.maximum(t - 1, 0), nf)

    @pl.when(t == 0)
    def _():
        a_ref[1] = jnp.zeros(a_ref.shape[1:], f32)
        b_ref[1] = jnp.zeros(b_ref.shape[1:], f32)

    @pl.when(jp == 0)
    def _():
        acc_ref[...] = jnp.zeros_like(acc_ref)

    _swiglu_stream_step(t, h_ref, w1_ref, w3_ref, w2_ref, acc_ref, a_ref, b_ref)

    @pl.when(jnp.logical_and(jp == nf - 1, t > 0))
    def _():
        o_ref[...] = x_ref[...] + g2_ref[0] * acc_ref[...]


def _dense_ffn_call(h2, w1, w3, w2, x, modl, seg_of_block, tm, tf):
    rows, d = h2.shape
    ff = w1.shape[1]
    nf = ff // tf
    n_pairs = (rows // tm) * nf

    def cur(t):
        tt = jnp.minimum(t, n_pairs - 1)
        return tt // nf, tt % nf

    def prev(t):
        tp = jnp.maximum(t - 1, 0)
        return tp // nf, tp % nf

    prev_row = pl.BlockSpec((tm, d), lambda t: (prev(t)[0], 0))
    return pl.pallas_call(
        functools.partial(_dense_ffn_kernel, nf=nf),
        out_shape=jax.ShapeDtypeStruct((rows, d), f32),
        grid=(n_pairs + 1,),
        in_specs=[pl.BlockSpec((tm, d), lambda t: (cur(t)[0], 0)),
                  pl.BlockSpec((d, tf), lambda t: (0, cur(t)[1])),
                  pl.BlockSpec((d, tf), lambda t: (0, cur(t)[1])),
                  pl.BlockSpec((tf, d), lambda t: (prev(t)[1], 0)),
                  prev_row,
                  pl.BlockSpec((1, 1, d), lambda t: (seg_of_block(prev(t)[0]) * 6 + 5, 0, 0))],
        out_specs=prev_row,
        scratch_shapes=[pltpu.VMEM((tm, d), f32), pltpu.VMEM((2, tm, tf), f32), pltpu.VMEM((2, tm, tf), f32)],
        compiler_params=_cparams(("arbitrary",), 56),
        name="dense_swiglu",
    )(h2, w1, w3, w2, x, modl)


def _router_kernel(h_ref, w_ref, idx_ref, wt_ref, *, n_experts):
    logits = jnp.dot(h_ref[...], w_ref[...], preferred_element_type=f32,
                     precision=lax.Precision.HIGHEST)
    lane = lax.broadcasted_iota(jnp.int32, logits.shape, 1)
    lg = jnp.where(lane < n_experts, logits, -jnp.inf)
    m1 = _rowmax(lg)
    i1 = jnp.min(jnp.where(lg == m1, lane, LANES), axis=-1, keepdims=True)
    lg2 = jnp.where(lane == i1, -jnp.inf, lg)
    m2 = _rowmax(lg2)
    i2 = jnp.min(jnp.where(lg2 == m2, lane, LANES), axis=-1, keepdims=True)
    e2 = jnp.exp(m2 - m1)
    w1 = 1.0 / (1.0 + e2)
    w2 = e2 / (1.0 + e2)
    idx_ref[...] = jnp.where(lane == 0, i1, jnp.where(lane == 1, i2, 0))
    wt_ref[...] = jnp.where(lane == 0, w1, jnp.where(lane == 1, w2, 0.0))


def _router_call(h2, w_router, tm):
    n, d = h2.shape
    e = w_router.shape[1]
    wp = jnp.zeros((d, LANES), f32).at[:, :e].set(w_router)
    return pl.pallas_call(
        functools.partial(_router_kernel, n_experts=e),
        out_shape=(jax.ShapeDtypeStruct((n, LANES), jnp.int32), jax.ShapeDtypeStruct((n, LANES), f32)),
        grid=(n // tm,),
        in_specs=[pl.BlockSpec((tm, d), lambda i: (i, 0)), pl.BlockSpec((d, LANES), lambda i: (0, 0))],
        out_specs=(pl.BlockSpec((tm, LANES), lambda i: (i, 0)), pl.BlockSpec((tm, LANES), lambda i: (i, 0))),
        compiler_params=_cparams(("arbitrary",), 32),
        name="moe_router_top2",
    )(h2, wp)


def _gather_rows_kernel(src_ref, x_hbm, o_ref, buf, sem, *, chunk):
    base = pl.program_id(0) * chunk

    def row_copy(r):
        return pltpu.make_async_copy(x_hbm.at[pl.ds(src_ref[base + r], 1)], buf.at[pl.ds(r, 1)], sem)

    def issue(r, c):
        row_copy(r).start()
        return c

    def drain(r, c):
        row_copy(r).wait()
        return c

    lax.fori_loop(0, chunk, issue, 0, unroll=8)
    lax.fori_loop(0, chunk, drain, 0, unroll=8)
    o_ref[...] = buf[...].astype(o_ref.dtype)


def _gather_rows_call(x, src, chunk):
    cap = src.shape[0]
    d = x.shape[1]
    return pl.pallas_call(
        functools.partial(_gather_rows_kernel, chunk=chunk),
        out_shape=jax.ShapeDtypeStruct((cap, d), bf16),
        grid_spec=pltpu.PrefetchScalarGridSpec(
            num_scalar_prefetch=1,
            grid=(cap // chunk,),
            in_specs=[pl.BlockSpec(memory_space=pl.ANY)],
            out_specs=pl.BlockSpec((chunk, d), lambda i, s: (i, 0)),
            scratch_shapes=[pltpu.VMEM((chunk, d), x.dtype), pltpu.SemaphoreType.DMA(())],
        ),
        compiler_params=_cparams(("arbitrary",), 32),
        name="moe_gather_rows",
    )(src, x)


def _expert_kernel(be_ref, nu_ref, x_ref, w1_ref, w3_ref, w2_ref, o_ref, acc_ref, a_ref, b_ref, *, nf):
    t = pl.program_id(0)
    tp = jnp.maximum(t - 1, 0)
    ip = tp // nf
    jp = lax.rem(tp, nf)
    n_live = nu_ref[0] * nf
    live = t <= n_live

    @pl.when(t == 0)
    def _():
        a_ref[1] = jnp.zeros(a_ref.shape[1:], f32)
        b_ref[1] = jnp.zeros(b_ref.shape[1:], f32)

    @pl.when(jnp.logical_and(jp == 0, live))
    def _():
        acc_ref[...] = jnp.zeros_like(acc_ref)

    @pl.when(live)
    def _():
        _swiglu_stream_step(t, x_ref, w1_ref.at[0], w3_ref.at[0], w2_ref.at[0], acc_ref, a_ref, b_ref)

    last = jnp.logical_and(jp == nf - 1, t > 0)

    @pl.when(jnp.logical_and(last, ip < nu_ref[0]))
    def _():
        o_ref[...] = acc_ref[...]

    @pl.when(jnp.logical_and(last, ip >= nu_ref[0]))
    def _():
        o_ref[...] = jnp.zeros_like(o_ref)


def _expert_call(xs, blk_e, n_used, w1, w3, w2, tm, tf):
    cap, d = xs.shape
    ff = w1.shape[2]
    nf = ff // tf
    n_pairs = (cap // tm) * nf

    def pair(tt, nu):
        i, j = tt // nf, tt % nf
        return jnp.minimum(i, nu[0] - 1), jnp.where(i < nu[0], j, nf - 1)

    def cur(t, nu):
        return pair(jnp.minimum(t, n_pairs - 1), nu)

    def prev(t, nu):
        return pair(jnp.maximum(t - 1, 0), nu)

    def up_spec():
        return pl.BlockSpec((1, d, tf), lambda t, be, nu: (be[cur(t, nu)[0]], 0, cur(t, nu)[1]))

    return pl.pallas_call(
        functools.partial(_expert_kernel, nf=nf),
        out_shape=jax.ShapeDtypeStruct((cap, d), f32),
        grid_spec=pltpu.PrefetchScalarGridSpec(
            num_scalar_prefetch=2,
            grid=(n_pairs + 1,),
            in_specs=[pl.BlockSpec((tm, d), lambda t, be, nu: (cur(t, nu)[0], 0)),
                      up_spec(), up_spec(),
                      pl.BlockSpec((1, tf, d), lambda t, be, nu: (be[prev(t, nu)[0]], prev(t, nu)[1], 0))],
            out_specs=pl.BlockSpec((tm, d), lambda t, be, nu: (jnp.maximum(t - 1, 0) // nf, 0)),
            scratch_shapes=[pltpu.VMEM((tm, d), f32), pltpu.VMEM((2, tm, tf), f32),
                            pltpu.VMEM((2, tm, tf), f32)],
        ),
        compiler_params=_cparams(("arbitrary",), 56),
        name="moe_expert_swiglu",
    )(blk_e, n_used, xs, w1, w3, w2)


def _combine_kernel(dest_ref, y_hbm, x_ref, g2_ref, wt_ref, o_ref, buf0, buf1, sem, *, tm):
    base = pl.program_id(0) * tm

    def copies(r):
        s = (base + r) * TOP_K
        return (pltpu.make_async_copy(y_hbm.at[pl.ds(dest_ref[s], 1)], buf0.at[pl.ds(r, 1)], sem),
                pltpu.make_async_copy(y_hbm.at[pl.ds(dest_ref[s + 1], 1)], buf1.at[pl.ds(r, 1)], sem))

    def issue(r, c):
        c0, c1 = copies(r)
        c0.start()
        c1.start()
        return c

    def drain(r, c):
        c0, c1 = copies(r)
        c0.wait()
        c1.wait()
        return c

    lax.fori_loop(0, tm, issue, 0, unroll=8)
    lax.fori_loop(0, tm, drain, 0, unroll=8)
    wt = wt_ref[...]
    mix = wt[:, 0:1] * buf0[...] + wt[:, 1:2] * buf1[...]
    o_ref[...] = x_ref[...] + g2_ref[0] * mix


def _combine_call(yb, dest, x, modl, wts, seg_of_block, tm):
    n, d = x.shape
    return pl.pallas_call(
        functools.partial(_combine_kernel, tm=tm),
        out_shape=jax.ShapeDtypeStruct((n, d), f32),
        grid_spec=pltpu.PrefetchScalarGridSpec(
            num_scalar_prefetch=1,
            grid=(n // tm,),
            in_specs=[pl.BlockSpec(memory_space=pl.ANY),
                      pl.BlockSpec((tm, d), lambda i, dst: (i, 0)),
                      pl.BlockSpec((1, 1, d), lambda i, dst: (seg_of_block(i) * 6 + 5, 0, 0)),
                      pl.BlockSpec((tm, LANES), lambda i, dst: (i, 0))],
            out_specs=pl.BlockSpec((tm, d), lambda i, dst: (i, 0)),
            scratch_shapes=[pltpu.VMEM((tm, d), f32), pltpu.VMEM((tm, d), f32),
                            pltpu.SemaphoreType.DMA(())],
        ),
        compiler_params=_cparams(("arbitrary",), 32),
        name="moe_combine_residual",
    )(dest, yb, x, modl, wts)


def _moe_call(h2, x, modl, w_router, w1, w3, w2, seg_of_block):
    n, d = h2.shape
    n_exp = w_router.shape[1]
    tm = 512
    idx, wts = _router_call(h2, w_router, 256)
    flat_e = idx[:, :TOP_K].reshape(-1)
    n_slots = n * TOP_K
    onehot = (flat_e[:, None] == jnp.arange(n_exp, dtype=jnp.int32)[None, :]).astype(jnp.int32)
    csum = jnp.cumsum(onehot, axis=0)
    counts = csum[-1]
    rank = jnp.sum((csum - 1) * onehot, axis=1)
    padded = (counts + tm - 1) // tm * tm
    pend = jnp.cumsum(padded)
    pstart = pend - padded
    dest = (pstart[flat_e] + rank).astype(jnp.int32)
    cap = (n_slots // tm + n_exp) * tm
    src = jnp.zeros((cap,), jnp.int32).at[dest].set(jnp.arange(n_slots, dtype=jnp.int32) // TOP_K)
    n_blocks = cap // tm
    blk_start = jnp.arange(n_blocks, dtype=jnp.int32) * tm
    blk_e = jnp.minimum(jnp.searchsorted(pend, blk_start, side='right'), n_exp - 1).astype(jnp.int32)
    n_used = (pend[-1:] // tm).astype(jnp.int32)

    xs = _gather_rows_call(h2, src, 256)
    yb = _expert_call(xs, blk_e, n_used, w1, w3, w2, tm, 512)
    return _combine_call(yb, dest, x, modl, wts, seg_of_block, 256)


def _rope_tables(nbatch, seq, ctx):
    t = jnp.arange(seq, dtype=jnp.int32)
    row = (t // GRID_W).astype(f32)
    col = (t % GRID_W).astype(f32)
    inv = ROPE_THETA ** (-jnp.arange(ROPE_PAIRS, dtype=f32) / ROPE_PAIRS)
    lane = jnp.arange(HEAD_DIM)
    pos = jnp.where(((lane // ROPE_PAIRS) % 2)[None, :] == 0, row[:, None], col[:, None])
    ang = pos * inv[lane % ROPE_PAIRS][None, :]
    cos = jnp.cos(ang)
    sin = jnp.where((lane < HEAD_DIM // 2)[None, :], -jnp.sin(ang), jnp.sin(ang))
    nctx = nbatch * ctx

    def full(tab, fill):
        return jnp.concatenate([jnp.tile(tab, (nbatch, 1)), jnp.full((nctx, HEAD_DIM), fill, f32)], axis=0)

    return full(cos, 1.0), full(sin, 0.0)


def _head_layout(w):
    lead = w.shape[:-1]
    w5 = w.reshape(lead + (w.shape[-1] // HEAD_DIM, 2, 2, ROPE_PAIRS))
    return jnp.swapaxes(w5, -3, -2).reshape(w.shape)


def kernel(x, c, ctx, c_ctx, w_mod, b_mod, norm1, norm2, w_in, qk_gain, a_sink, b_lambda, b_subln,
           w_branch, w_out, dense_w1, dense_w3, dense_w2, moe_router, moe_w1, moe_w3, moe_w2):
    nbatch, seq, d = x.shape
    nctx_len = ctx.shape[1]
    depth = w_mod.shape[0]
    nl = nbatch * seq
    nc = nbatch * nctx_len
    tm = 512

    def seg_of_block_for(t):
        return lambda i: jnp.minimum(i // (seq // t), nbatch)

    xs = jnp.concatenate([x.reshape(nl, d), ctx.reshape(nc, d)], axis=0)
    cc = jnp.zeros((8, d), f32).at[:nbatch].set(c).at[nbatch].set(c_ctx)
    mod = _mod_call(cc, w_mod, b_mod)
    cos, sin = _rope_tables(nbatch, seq, nctx_len)
    ones_mat = jnp.ones((HEAD_DIM, HEAD_DIM), bf16)
    scale = HEAD_DIM ** -0.5
    ones_cols = jnp.ones((HEAD_DIM,), f32)
    zero_w = jnp.zeros((d, HEAD_DIM), bf16)

    for l in range(depth):
        with_ctx = l < depth - 1
        lam_init = 0.8 - 0.6 * math.exp(-0.3 * l)
        rows = nl + nc if with_ctx else nl
        modl = mod[l].reshape(8 * 6, 1, d)

        wl = w_in[l]
        o = 0
        parts = {}
        for name, width in (("aq", A_Q), ("ak", A_KV), ("av", A_KV), ("bq", B_QK), ("bk", B_QK),
                            ("bv", B_V), ("cq", C_Q), ("ck", C_KV), ("cv", C_KV), ("g", 3 * d)):
            parts[name] = wl[:, o:o + width].astype(bf16)
            o += width
        w_qk = _head_layout(jnp.concatenate([parts[k] for k in ("cq", "aq", "bq", "bk", "ck", "ak")], axis=1))
        cv = parts["cv"]
        cv_cols = []
        for hd in range(C_KV_HEADS):
            cv_cols += [cv[:, hd * HEAD_DIM:(hd + 1) * HEAD_DIM], zero_w]
        w_v = jnp.concatenate([parts["bv"]] + cv_cols + [parts["av"]], axis=1)
        zeros_bv = jnp.zeros((B_V,), f32)
        zeros_h = jnp.zeros((HEAD_DIM,), f32)
        bias_v = jnp.concatenate([zeros_bv] + [zeros_h, ones_cols] * C_KV_HEADS
                                 + [jnp.zeros((A_KV,), f32)]).reshape(1, V_COLS)
        w_g = parts["g"]
        g = qk_gain[l]
        gain = _head_layout(jnp.concatenate(
            [jnp.tile(g[4] * (scale * LOG2E), C_HEADS), jnp.tile(g[0] * scale, A_HEADS),
             jnp.tile(g[2] * (scale * LOG2E), 2 * B_HEADS), jnp.tile(g[3], 2 * B_HEADS),
             jnp.tile(g[5], C_KV_HEADS), jnp.tile(g[1], A_KV_HEADS)]).reshape(1, QK_COLS))

        h = _norm_call(xs, norm1[l], modl, 1, 0, seg_of_block_for(tm), tm, bf16)
        tab_spec = ((tm, HEAD_DIM), lambda i, j: (i, 0))
        qk = _deferred_mm_call(_qk_epilogue, h, w_qk, (gain, cos, sin, ones_mat),
                               [((1, 512), lambda i, j: (0, j)), tab_spec, tab_spec,
                                ((HEAD_DIM, HEAD_DIM), lambda i, j: (0, 0))],
                               nl + nc, tm, 512, "qk_proj_norm_rope")
        tn_v = V_COLS // 2
        v = _proj_call(_bias_proj_kernel, h, w_v, (bias_v,), [pl.BlockSpec((1, tn_v), lambda i, j: (0, j))],
                       nl + nc, tm, tn_v, "v_proj")
        gates = _deferred_mm_call(_gate_epilogue, h, w_g, (), [], rows, tm, 512, "gate_proj")

        oa, ob, oc = _attention(qk, v, a_sink[l], b_lambda[l], b_subln[l], nbatch=nbatch, seq=seq,
                                ctx=nctx_len, with_ctx=with_ctx, lam_init=lam_init)
        y = _merge_call(oa, ob, oc, w_branch[l].astype(bf16), gates, tm, 512)
        moe_layer = l % 2 == 1
        xn, h2 = _out_proj_call(y, w_out[l].astype(bf16), xs, modl, norm2[l], seg_of_block_for(256), 256,
                                f32 if moe_layer else bf16)
        i = l // 2
        if not moe_layer:
            xs = _dense_ffn_call(h2, dense_w1[i].astype(bf16), dense_w3[i].astype(bf16),
                                 dense_w2[i].astype(bf16), xn, modl, seg_of_block_for(tm), tm, 512)
        else:
            xs = _moe_call(h2, xn, modl, moe_router[i], moe_w1[i].astype(bf16), moe_w3[i].astype(bf16),
                           moe_w2[i].astype(bf16), seg_of_block_for(256))
    return xs[:nl].reshape(nbatch, seq, d)
```

```python
import functools
import math

import jax
import jax.numpy as jnp
from jax import lax
from jax.experimental import pallas as pl
from jax.experimental.pallas import tpu as pltpu

f32 = jnp.float32
bf16 = jnp.bfloat16

HEAD_DIM = 128
GRID_W = 64
ROPE_PAIRS = HEAD_DIM // 4
ROPE_THETA = 10000.0
EPS = 1e-6
NEG_INF = -1e30
WINDOW = 128
A_HEADS, A_KV_HEADS = 8, 2
B_HEADS = 4
C_HEADS, C_KV_HEADS = 8, 2
GQA_GROUP = 4
TOP_K = 2
LANES = 128
MXU_WIDTH = 256
MIB = 1024 * 1024
LOG2E = math.log2(math.e)

A_Q, A_KV = A_HEADS * HEAD_DIM, A_KV_HEADS * HEAD_DIM
B_QK, B_V = B_HEADS * 2 * HEAD_DIM, B_HEADS * 2 * HEAD_DIM
C_Q, C_KV = C_HEADS * HEAD_DIM, C_KV_HEADS * HEAD_DIM
QK_CQ, QK_AQ, QK_BQ, QK_BK = 0, C_Q, C_Q + A_Q, C_Q + A_Q + B_QK
QK_CK = QK_BK + B_QK
QK_AK = QK_CK + C_KV
QK_COLS = QK_AK + A_KV
V_BV, V_CV = 0, B_V
V_AV = V_CV + C_KV_HEADS * 2 * HEAD_DIM
V_COLS = V_AV + A_KV


def _cparams(sem, vmem_mib):
    return pltpu.CompilerParams(dimension_semantics=sem, vmem_limit_bytes=vmem_mib * MIB)


def _dot(a, b):
    return jnp.dot(a, b, preferred_element_type=f32)


def _dot_nt(a, b):
    return lax.dot_general(a, b, (((1,), (1,)), ((), ())), preferred_element_type=f32)


def _sigmoid(x):
    return 0.5 * jnp.tanh(0.5 * x) + 0.5


def _rowmax(s):
    return jnp.max(s, axis=-1, keepdims=True)


def _rowsum(s):
    return jnp.sum(s, axis=-1, keepdims=True)


def _lane_tile(x, n):
    return jnp.concatenate([x] * n, axis=1)


def _mod_kernel(c_ref, w_ref, b_ref, o_ref):
    c = c_ref[...]
    a = (c * _sigmoid(c)).astype(bf16)
    o_ref[0] = _dot(a, w_ref[0].astype(bf16)) + b_ref[0]


def _mod_call(cc, w_mod, b_mod):
    depth, d, n = w_mod.shape
    tn = 1024
    return pl.pallas_call(
        _mod_kernel,
        out_shape=jax.ShapeDtypeStruct((depth, 8, n), f32),
        grid=(depth, n // tn),
        in_specs=[pl.BlockSpec((8, d), lambda l, j: (0, 0)),
                  pl.BlockSpec((1, d, tn), lambda l, j: (l, 0, j)),
                  pl.BlockSpec((1, 1, tn), lambda l, j: (l, 0, j))],
        out_specs=pl.BlockSpec((1, 8, tn), lambda l, j: (l, 0, j)),
        compiler_params=_cparams(("arbitrary", "arbitrary"), 40),
        name="mod_vectors",
    )(cc, w_mod, b_mod.reshape(depth, 1, n))


def _norm_kernel(x_ref, nw_ref, sc_ref, sh_ref, o_ref):
    x = x_ref[...]
    y = x * lax.rsqrt(jnp.mean(x * x, axis=-1, keepdims=True) + EPS) * nw_ref[...]
    o_ref[...] = (y * (1.0 + sc_ref[0]) + sh_ref[0]).astype(o_ref.dtype)


def _norm_call(x, nw, modl, k_sc, k_sh, seg_of_block, tm, out_dtype):
    r, d = x.shape
    return pl.pallas_call(
        _norm_kernel,
        out_shape=jax.ShapeDtypeStruct((r, d), out_dtype),
        grid=(r // tm,),
        in_specs=[pl.BlockSpec((tm, d), lambda i: (i, 0)),
                  pl.BlockSpec((1, d), lambda i: (0, 0)),
                  pl.BlockSpec((1, 1, d), lambda i: (seg_of_block(i) * 6 + k_sc, 0, 0)),
                  pl.BlockSpec((1, 1, d), lambda i: (seg_of_block(i) * 6 + k_sh, 0, 0))],
        out_specs=pl.BlockSpec((tm, d), lambda i: (i, 0)),
        compiler_params=_cparams(("arbitrary",), 32),
        name="norm_modulate",
    )(x, nw.reshape(1, d), modl, modl)


def _qk_epilogue(acc_ref, o_ref, gain_ref, cos_ref, sin_ref, ones_ref):
    cos, sin = cos_ref[...], sin_ref[...]
    for hd in range(o_ref.shape[1] // HEAD_DIM):
        sl = slice(hd * HEAD_DIM, (hd + 1) * HEAD_DIM)
        t = acc_ref[:, sl]
        ss = _dot((t * t).astype(bf16), ones_ref[...])
        y = t * lax.rsqrt(ss * (1.0 / HEAD_DIM) + EPS) * gain_ref[:, sl]
        o_ref[:, sl] = (y * cos + pltpu.roll(y, HEAD_DIM // 2, axis=1) * sin).astype(o_ref.dtype)


def _gate_epilogue(acc_ref, o_ref):
    o_ref[...] = _sigmoid(acc_ref[...]).astype(o_ref.dtype)


def _deferred_mm_kernel(h_ref, w_ref, *rest, epilogue, n_extra):
    extra, o_ref, acc_ref = rest[:n_extra], rest[n_extra], rest[n_extra + 1]
    t = pl.program_id(0)

    @pl.when(t == 0)
    def _():
        acc_ref[1] = jnp.zeros(acc_ref.shape[1:], f32)

    for slot in (0, 1):
        @pl.when(lax.rem(t, 2) == slot)
        def _(slot=slot):
            epilogue(acc_ref.at[1 - slot], o_ref, *extra)
            acc_ref[slot] = _dot(h_ref[...], w_ref[...])


def _tile_major(w, tn):
    lead, (k, n) = w.shape[:-2], w.shape[-2:]
    return jnp.swapaxes(w.reshape(lead + (k, n // tn, tn)), -3, -2)


def _row_tile(nrows):
    return next(t for t in (1024, 768, 512) if nrows % t == 0)


def _deferred_mm_call(epilogue, h, w, extra, extra_specs, nrows, tm, name):
    k = h.shape[1]
    nj, _, tn = w.shape
    n = nj * tn
    n_tiles = (nrows // tm) * nj

    def cur(t):
        tt = jnp.minimum(t, n_tiles - 1)
        return tt // nj, tt % nj

    def prev(t):
        tp = jnp.maximum(t - 1, 0)
        return tp // nj, tp % nj

    return pl.pallas_call(
        functools.partial(_deferred_mm_kernel, epilogue=epilogue, n_extra=len(extra)),
        out_shape=jax.ShapeDtypeStruct((nrows, n), bf16),
        grid=(n_tiles + 1,),
        in_specs=[pl.BlockSpec((tm, k), lambda t: (cur(t)[0], 0)),
                  pl.BlockSpec((None, k, tn), lambda t: (cur(t)[1], 0, 0))]
                 + [pl.BlockSpec(shape, functools.partial(lambda t, f: f(*prev(t)), f=f))
                    for shape, f in extra_specs],
        out_specs=pl.BlockSpec((tm, tn), lambda t: prev(t)),
        scratch_shapes=[pltpu.VMEM((2, tm, tn), f32)],
        compiler_params=_cparams(("arbitrary",), 40),
        name=name,
    )(h, w, *extra)


def _bias_proj_kernel(h_ref, w_ref, b_ref, o_ref):
    o_ref[...] = (_dot(h_ref[...], w_ref[...]) + b_ref[...]).astype(o_ref.dtype)


def _proj_call(kernel, h, w, extra, extra_specs, nrows, tm, tn, name):
    k = h.shape[1]
    n = w.shape[1]
    return pl.pallas_call(
        kernel,
        out_shape=jax.ShapeDtypeStruct((nrows, n), bf16),
        grid=(nrows // tm, n // tn),
        in_specs=[pl.BlockSpec((tm, k), lambda i, j: (i, 0)),
                  pl.BlockSpec((k, tn), lambda i, j: (0, j))] + extra_specs,
        out_specs=pl.BlockSpec((tm, tn), lambda i, j: (i, j)),
        compiler_params=_cparams(("arbitrary", "arbitrary"), 40),
        name=name,
    )(h, w, *extra)


def _stack_heads(q, n):
    return jnp.concatenate([q[:, g * HEAD_DIM:(g + 1) * HEAD_DIM] for g in range(n)], axis=0)


def _unstack_heads(o, n):
    t = o.shape[0] // n
    return jnp.concatenate([o[g * t:(g + 1) * t] for g in range(n)], axis=1)


def _softmax_pv(s, v, ex, extra_logit=None):
    m = _rowmax(s)
    if extra_logit is not None:
        m = jnp.maximum(m, extra_logit)
    p = ex(s - m)
    l = _rowsum(p)
    if extra_logit is not None:
        l = l + ex(extra_logit - m)
    return _dot(p.astype(bf16), v) / l


def _sink_column(sink_ref, kh, tq):
    return jnp.concatenate(
        [jnp.full((tq, 1), sink_ref[kh * GQA_GROUP + g], f32) for g in range(GQA_GROUP)], axis=0)


def _diff_finish(on0, on1, lam_ref, g_ref, lam_init):
    bl = lam_ref[...]
    lam = (jnp.exp(_rowsum(bl[0:1] * bl[1:2])) - jnp.exp(_rowsum(bl[2:3] * bl[3:4])) + lam_init)
    o = on0 - lam * on1
    y = o * lax.rsqrt(jnp.mean(o * o, axis=-1, keepdims=True) + EPS) * g_ref[...]
    return y * (1.0 - lam_init)


def _attn_a_lat_kernel(sink_ref, q_ref, kl_ref, kc_ref, vl_ref, vc_ref, o_ref, *, tq, seq):
    kh = pl.program_id(1)
    qi = pl.program_id(2)
    qs = _stack_heads(q_ref[...], GQA_GROUP)
    q0 = qi * tq
    band = tq + 2 * WINDOW
    start = pl.multiple_of(jnp.clip(q0 - WINDOW, 0, seq - band), WINDOW)
    kb = kl_ref[pl.ds(start, band), :]
    vb = vl_ref[pl.ds(start, band), :]
    s_ctx = _dot_nt(qs, kc_ref[...])
    s_loc = _dot_nt(qs, kb)
    m_rows = GQA_GROUP * tq
    qpos = q0 + (lax.broadcasted_iota(jnp.int32, (m_rows, band), 0) & (tq - 1))
    kpos = start + lax.broadcasted_iota(jnp.int32, (m_rows, band), 1)
    s_loc = jnp.where(jnp.abs(qpos - kpos) <= WINDOW, s_loc, NEG_INF)
    sink = _sink_column(sink_ref, kh, tq)
    m = jnp.maximum(jnp.maximum(_rowmax(s_ctx), _rowmax(s_loc)), sink)
    p_ctx = jnp.exp(s_ctx - m)
    p_loc = jnp.exp(s_loc - m)
    l = _rowsum(p_ctx) + _rowsum(p_loc) + jnp.exp(sink - m)
    o = (_dot(p_ctx.astype(bf16), vc_ref[...]) + _dot(p_loc.astype(bf16), vb)) / l
    o_ref[...] = _unstack_heads(o, GQA_GROUP).astype(o_ref.dtype)


def _attn_a_ctx_kernel(sink_ref, q_ref, kc_ref, vc_ref, o_ref, *, tq):
    qs = _stack_heads(q_ref[...], GQA_GROUP)
    o = _softmax_pv(_dot_nt(qs, kc_ref[...]), vc_ref[...], jnp.exp,
                    _sink_column(sink_ref, pl.program_id(1), tq))
    o_ref[...] = _unstack_heads(o, GQA_GROUP).astype(o_ref.dtype)


def _attn_c_lat_kernel(q_ref, kl_ref, kc_ref, vl_ref, vc_ref, o_ref, m_ref, acc_ref, *, tk, n_chunks):
    qs = _stack_heads(q_ref[...], GQA_GROUP)
    s = _dot_nt(qs, kc_ref[...])
    m0 = _rowmax(s)
    m_ref[...] = jnp.broadcast_to(m0, m_ref.shape)
    acc_ref[...] = _dot(jnp.exp2(s - m0).astype(bf16), vc_ref[...])

    def chunk(c, carry):
        off = pl.multiple_of(c * tk, tk)
        s = _dot_nt(qs, kl_ref[pl.ds(off, tk), :])
        m_old = m_ref[...]
        m_new = jnp.maximum(m_old, _rowmax(s))
        alpha = jnp.exp2(m_old - m_new)
        p = jnp.exp2(s - _lane_tile(m_new, tk // LANES))
        acc_ref[...] = (_lane_tile(alpha, 2) * acc_ref[...]
                        + _dot(p.astype(bf16), vl_ref[pl.ds(off, tk), :]))
        m_ref[...] = m_new
        return carry

    lax.fori_loop(0, n_chunks, chunk, 0, unroll=min(n_chunks, 8 * 512 // tk))
    acc = acc_ref[...]
    o = acc[:, :HEAD_DIM] / acc[:, HEAD_DIM:]
    o_ref[...] = _unstack_heads(o, GQA_GROUP).astype(o_ref.dtype)


def _attn_c_ctx_kernel(q_ref, kc_ref, vc_ref, o_ref):
    qs = _stack_heads(q_ref[...], GQA_GROUP)
    o = _softmax_pv(_dot_nt(qs, kc_ref[...]), vc_ref[...], jnp.exp2)
    o_ref[...] = _unstack_heads(o, GQA_GROUP).astype(o_ref.dtype)


def _diff_scores(q, k):
    return jnp.concatenate([_dot_nt(q[:, :HEAD_DIM], k[:, :HEAD_DIM]),
                            _dot_nt(q[:, HEAD_DIM:], k[:, HEAD_DIM:])], axis=0)


def _attn_b_lat_kernel(q_ref, kl_ref, kc_ref, vl_ref, vc_ref, lam_ref, g_ref, o_ref,
                       m_ref, l_ref, acc_ref, *, tk, n_chunks, lam_init):
    q = q_ref[...]
    tq = q.shape[0]
    s = _diff_scores(q, kc_ref[...])
    m0 = _rowmax(s)
    p = jnp.exp2(s - m0)
    m_ref[...] = jnp.broadcast_to(m0, m_ref.shape)
    l_ref[...] = jnp.broadcast_to(_rowsum(p), l_ref.shape)
    acc_ref[...] = _dot(p.astype(bf16), vc_ref[...])

    def chunk(c, carry):
        off = pl.multiple_of(c * tk, tk)
        s = _diff_scores(q, kl_ref[pl.ds(off, tk), :])
        m_old = m_ref[...]
        m_new = jnp.maximum(m_old, _rowmax(s))
        alpha = jnp.exp2(m_old - m_new)
        p = jnp.exp2(s - _lane_tile(m_new, tk // LANES))
        l_ref[...] = alpha * l_ref[...] + _rowsum(p)
        acc_ref[...] = (_lane_tile(alpha, 2) * acc_ref[...]
                        + _dot(p.astype(bf16), vl_ref[pl.ds(off, tk), :]))
        m_ref[...] = m_new
        return carry

    lax.fori_loop(0, n_chunks, chunk, 0, unroll=min(n_chunks, 8 * 512 // tk))
    on = acc_ref[...] / _lane_tile(l_ref[...], 2)
    o_ref[...] = _diff_finish(on[:tq], on[tq:], lam_ref, g_ref, lam_init).astype(o_ref.dtype)


def _attn_b_ctx_kernel(q_ref, kc_ref, vc_ref, lam_ref, g_ref, o_ref, *, lam_init):
    q = q_ref[...]
    tq = q.shape[0]
    on = _softmax_pv(_diff_scores(q, kc_ref[...]), vc_ref[...], jnp.exp2)
    o_ref[...] = _diff_finish(on[:tq], on[tq:], lam_ref, g_ref, lam_init).astype(o_ref.dtype)


def _attention(qk, v, a_sink, b_lam, b_subln, *, nbatch, seq, ctx, with_ctx, lam_init):
    tq_gqa = 256
    tq_diff = min(512, seq)
    tq_ctx = ctx
    tk = min(512, seq)
    nl_rows = nbatch * seq
    nc_rows = nbatch * ctx
    cb = nl_rows // ctx
    gw = GQA_GROUP * HEAD_DIM
    bw = 2 * HEAD_DIM
    sem = ("arbitrary", "arbitrary", "arbitrary")

    def lat_q(tq, width, off):
        return pl.BlockSpec((tq, width), lambda b, h, qi, *_: (b * (seq // tq) + qi, off // width + h))

    def ctx_q(width, off):
        return pl.BlockSpec((tq_ctx, width), lambda b, h, qi, *_: (cb + b, off // width + h))

    def lat_kv(width, off):
        return pl.BlockSpec((seq, width), lambda b, h, qi, *_: (b, off // width + h))

    def ctx_kv(width, off):
        return pl.BlockSpec((ctx, width), lambda b, h, qi, *_: (cb + b, off // width + h))

    def ctx_out(width):
        return pl.BlockSpec((tq_ctx, width), lambda b, h, qi, *_: (b, h))

    oa = pl.pallas_call(
        functools.partial(_attn_a_lat_kernel, tq=tq_gqa, seq=seq),
        out_shape=jax.ShapeDtypeStruct((nl_rows, A_Q), bf16),
        grid_spec=pltpu.PrefetchScalarGridSpec(
            num_scalar_prefetch=1,
            grid=(nbatch, A_KV_HEADS, seq // tq_gqa),
            in_specs=[lat_q(tq_gqa, gw, QK_AQ), lat_kv(HEAD_DIM, QK_AK), ctx_kv(HEAD_DIM, QK_AK),
                      lat_kv(HEAD_DIM, V_AV), ctx_kv(HEAD_DIM, V_AV)],
            out_specs=lat_q(tq_gqa, gw, 0),
        ),
        compiler_params=_cparams(sem, 40),
        name="attn_window",
    )(a_sink, qk, qk, qk, v, v)
    if with_ctx:
        oa_ctx = pl.pallas_call(
            functools.partial(_attn_a_ctx_kernel, tq=tq_ctx),
            out_shape=jax.ShapeDtypeStruct((nc_rows, A_Q), bf16),
            grid_spec=pltpu.PrefetchScalarGridSpec(
                num_scalar_prefetch=1,
                grid=(nbatch, A_KV_HEADS, 1),
                in_specs=[ctx_q(gw, QK_AQ), ctx_kv(HEAD_DIM, QK_AK), ctx_kv(HEAD_DIM, V_AV)],
                out_specs=ctx_out(gw),
            ),
            compiler_params=_cparams(sem, 32),
            name="attn_window_ctx",
        )(a_sink, qk, qk, v)
        oa = jnp.concatenate([oa, oa_ctx], axis=0)

    m_rows = GQA_GROUP * tq_gqa
    oc = pl.pallas_call(
        functools.partial(_attn_c_lat_kernel, tk=tk, n_chunks=seq // tk),
        out_shape=jax.ShapeDtypeStruct((nl_rows, C_Q), bf16),
        grid=(nbatch, C_KV_HEADS, seq // tq_gqa),
        in_specs=[lat_q(tq_gqa, gw, QK_CQ), lat_kv(HEAD_DIM, QK_CK), ctx_kv(HEAD_DIM, QK_CK),
                  lat_kv(2 * HEAD_DIM, V_CV), ctx_kv(2 * HEAD_DIM, V_CV)],
        out_specs=lat_q(tq_gqa, gw, 0),
        scratch_shapes=[pltpu.VMEM((m_rows, LANES), f32), pltpu.VMEM((m_rows, 2 * HEAD_DIM), f32)],
        compiler_params=_cparams(sem, 40),
        name="attn_global",
    )(qk, qk, qk, v, v)
    if with_ctx:
        oc_ctx = pl.pallas_call(
            _attn_c_ctx_kernel,
            out_shape=jax.ShapeDtypeStruct((nc_rows, C_Q), bf16),
            grid=(nbatch, C_KV_HEADS, 1),
            in_specs=[ctx_q(gw, QK_CQ), ctx_kv(HEAD_DIM, QK_CK),
                      pl.BlockSpec((ctx, HEAD_DIM), lambda b, h, qi: (cb + b, V_CV // HEAD_DIM + 2 * h))],
            out_specs=ctx_out(gw),
            compiler_params=_cparams(sem, 32),
            name="attn_global_ctx",
        )(qk, qk, v)
        oc = jnp.concatenate([oc, oc_ctx], axis=0)

    lam_spec = pl.BlockSpec((4, HEAD_DIM), lambda b, h, qi: (0, 0))
    g_spec = pl.BlockSpec((1, bw), lambda b, h, qi: (0, 0))
    g2d = b_subln.reshape(1, bw)
    ob = pl.pallas_call(
        functools.partial(_attn_b_lat_kernel, tk=tk, n_chunks=seq // tk, lam_init=lam_init),
        out_shape=jax.ShapeDtypeStruct((nl_rows, B_V), bf16),
        grid=(nbatch, B_HEADS, seq // tq_diff),
        in_specs=[lat_q(tq_diff, bw, QK_BQ), lat_kv(bw, QK_BK), ctx_kv(bw, QK_BK),
                  lat_kv(bw, V_BV), ctx_kv(bw, V_BV), lam_spec, g_spec],
        out_specs=lat_q(tq_diff, bw, 0),
        scratch_shapes=[pltpu.VMEM((2 * tq_diff, LANES), f32), pltpu.VMEM((2 * tq_diff, LANES), f32),
                        pltpu.VMEM((2 * tq_diff, bw), f32)],
        compiler_params=_cparams(sem, 48),
        name="attn_diff",
    )(qk, qk, qk, v, v, b_lam, g2d)
    if with_ctx:
        ob_ctx = pl.pallas_call(
            functools.partial(_attn_b_ctx_kernel, lam_init=lam_init),
            out_shape=jax.ShapeDtypeStruct((nc_rows, B_V), bf16),
            grid=(nbatch, B_HEADS, 1),
            in_specs=[ctx_q(bw, QK_BQ), ctx_kv(bw, QK_BK), ctx_kv(bw, V_BV), lam_spec, g_spec],
            out_specs=ctx_out(bw),
            compiler_params=_cparams(sem, 32),
            name="attn_diff_ctx",
        )(qk, qk, v, b_lam, g2d)
        ob = jnp.concatenate([ob, ob_ctx], axis=0)
    return oa, ob, oc


def _merge_kernel(oa_ref, ob_ref, oc_ref, w_ref, ga_ref, gb_ref, gc_ref, y_ref):
    y = (ga_ref[...].astype(f32) * _dot(oa_ref[...], w_ref[0])
         + gb_ref[...].astype(f32) * _dot(ob_ref[...], w_ref[1])
         + gc_ref[...].astype(f32) * _dot(oc_ref[...], w_ref[2]))
    y_ref[...] = y.astype(y_ref.dtype)


def _merge_call(oa, ob, oc, wb, gates, tm):
    rows, bwid = oa.shape
    nj, _, _, tn = wb.shape
    d = nj * tn
    o_spec = pl.BlockSpec((tm, bwid), lambda i, j: (i, 0))
    return pl.pallas_call(
        _merge_kernel,
        out_shape=jax.ShapeDtypeStruct((rows, d), bf16),
        grid=(rows // tm, nj),
        in_specs=[o_spec, o_spec, o_spec,
                  pl.BlockSpec((None, 3, bwid, tn), lambda i, j: (j, 0, 0, 0)),
                  pl.BlockSpec((tm, tn), lambda i, j: (i, j)),
                  pl.BlockSpec((tm, tn), lambda i, j: (i, nj + j)),
                  pl.BlockSpec((tm, tn), lambda i, j: (i, 2 * nj + j))],
        out_specs=pl.BlockSpec((tm, tn), lambda i, j: (i, j)),
        compiler_params=_cparams(("arbitrary", "arbitrary"), 40),
        name="branch_merge",
    )(oa, ob, oc, wb, gates, gates, gates)


def _out_proj_kernel(y_ref, w_ref, x_ref, g1_ref, nw_ref, sc_ref, sh_ref, xo_ref, h_ref):
    xn = x_ref[...] + g1_ref[0] * _dot(y_ref[...], w_ref[...])
    xo_ref[...] = xn
    t = xn * lax.rsqrt(jnp.mean(xn * xn, axis=-1, keepdims=True) + EPS) * nw_ref[...]
    h_ref[...] = (t * (1.0 + sc_ref[0]) + sh_ref[0]).astype(h_ref.dtype)


def _out_proj_call(y, w_out, x, modl, nw2, seg_of_block, tm, h_dtype):
    rows, d = y.shape

    def mod_spec(k):
        return pl.BlockSpec((1, 1, d), lambda i: (seg_of_block(i) * 6 + k, 0, 0))

    row_spec = pl.BlockSpec((tm, d), lambda i: (i, 0))
    return pl.pallas_call(
        _out_proj_kernel,
        out_shape=(jax.ShapeDtypeStruct((rows, d), f32), jax.ShapeDtypeStruct((rows, d), h_dtype)),
        grid=(rows // tm,),
        in_specs=[row_spec, pl.BlockSpec((d, d), lambda i: (0, 0)), row_spec,
                  mod_spec(2), pl.BlockSpec((1, d), lambda i: (0, 0)), mod_spec(4), mod_spec(3)],
        out_specs=(row_spec, row_spec),
        compiler_params=_cparams(("arbitrary",), 48),
        name="out_proj_residual_norm",
    )(y, w_out, x, modl, nw2.reshape(1, d), modl, modl)


def _swiglu_stream_step(t, h_ref, w1_ref, w3_ref, w2_ref, acc_ref, a_ref, b_ref):
    for slot in (0, 1):
        @pl.when(lax.rem(t, 2) == slot)
        def _(slot=slot):
            h = h_ref[...]
            a_ref[slot] = _dot(h, w1_ref[...])
            a = a_ref[1 - slot]
            b = b_ref[1 - slot]
            g = (a * _sigmoid(a) * b).astype(bf16)
            acc_ref[...] += _dot(g, w2_ref[...])
            b_ref[slot] = _dot(h, w3_ref[...])


def _dense_ffn_kernel(h_ref, w1_ref, w3_ref, w2_ref, x_ref, g2_ref, o_ref, acc_ref, a_ref, b_ref, *, nf):
    t = pl.program_id(0)
    jp = lax.rem(jnp.maximum(t - 1, 0), nf)

    @pl.when(t == 0)
    def _():
        a_ref[1] = jnp.zeros(a_ref.shape[1:], f32)
        b_ref[1] = jnp.zeros(b_ref.shape[1:], f32)

    @pl.when(jp == 0)
    def _():
        acc_ref[...] = jnp.zeros_like(acc_ref)

    _swiglu_stream_step(t, h_ref, w1_ref, w3_ref, w2_ref, acc_ref, a_ref, b_ref)

    @pl.when(jnp.logical_and(jp == nf - 1, t > 0))
    def _():
        o_ref[...] = x_ref[...] + g2_ref[0] * acc_ref[...]


def _dense_ffn_call(h2, w1, w3, w2, x, modl, seg_of_block, tm):
    rows, d = h2.shape
    nf, _, tf = w1.shape
    n_pairs = (rows // tm) * nf

    def cur(t):
        tt = jnp.minimum(t, n_pairs - 1)
        return tt // nf, tt % nf

    def prev(t):
        tp = jnp.maximum(t - 1, 0)
        return tp // nf, tp % nf

    prev_row = pl.BlockSpec((tm, d), lambda t: (prev(t)[0], 0))
    return pl.pallas_call(
        functools.partial(_dense_ffn_kernel, nf=nf),
        out_shape=jax.ShapeDtypeStruct((rows, d), f32),
        grid=(n_pairs + 1,),
        in_specs=[pl.BlockSpec((tm, d), lambda t: (cur(t)[0], 0)),
                  pl.BlockSpec((None, d, tf), lambda t: (cur(t)[1], 0, 0)),
                  pl.BlockSpec((None, d, tf), lambda t: (cur(t)[1], 0, 0)),
                  pl.BlockSpec((tf, d), lambda t: (prev(t)[1], 0)),
                  prev_row,
                  pl.BlockSpec((1, 1, d), lambda t: (seg_of_block(prev(t)[0]) * 6 + 5, 0, 0))],
        out_specs=prev_row,
        scratch_shapes=[pltpu.VMEM((tm, d), f32), pltpu.VMEM((2, tm, tf), f32), pltpu.VMEM((2, tm, tf), f32)],
        compiler_params=_cparams(("arbitrary",), 56),
        name="dense_swiglu",
    )(h2, w1, w3, w2, x, modl)


def _router_kernel(h_ref, w_ref, idx_ref, wt_ref, *, n_experts):
    logits = jnp.dot(h_ref[...], w_ref[...], preferred_element_type=f32,
                     precision=lax.Precision.HIGHEST)
    lane = lax.broadcasted_iota(jnp.int32, logits.shape, 1)
    lg = jnp.where(lane < n_experts, logits, -jnp.inf)
    m1 = _rowmax(lg)
    i1 = jnp.min(jnp.where(lg == m1, lane, LANES), axis=-1, keepdims=True)
    lg2 = jnp.where(lane == i1, -jnp.inf, lg)
    m2 = _rowmax(lg2)
    i2 = jnp.min(jnp.where(lg2 == m2, lane, LANES), axis=-1, keepdims=True)
    e2 = jnp.exp(m2 - m1)
    w1 = 1.0 / (1.0 + e2)
    w2 = e2 / (1.0 + e2)
    idx_ref[...] = jnp.where(lane == 0, i1, jnp.where(lane == 1, i2, 0))
    wt_ref[...] = jnp.where(lane == 0, w1, jnp.where(lane == 1, w2, 0.0))


def _router_call(h2, w_router, tm):
    n, d = h2.shape
    e = w_router.shape[1]
    wp = jnp.zeros((d, LANES), f32).at[:, :e].set(w_router)
    return pl.pallas_call(
        functools.partial(_router_kernel, n_experts=e),
        out_shape=(jax.ShapeDtypeStruct((n, LANES), jnp.int32), jax.ShapeDtypeStruct((n, LANES), f32)),
        grid=(n // tm,),
        in_specs=[pl.BlockSpec((tm, d), lambda i: (i, 0)), pl.BlockSpec((d, LANES), lambda i: (0, 0))],
        out_specs=(pl.BlockSpec((tm, LANES), lambda i: (i, 0)), pl.BlockSpec((tm, LANES), lambda i: (i, 0))),
        compiler_params=_cparams(("arbitrary",), 32),
        name="moe_router_top2",
    )(h2, wp)


def _gather_rows_kernel(src_ref, x_hbm, o_ref, buf, sem, *, chunk):
    i = pl.program_id(0)
    slot = lax.rem(i, 2)

    def row_copy(step, sl, r):
        return pltpu.make_async_copy(x_hbm.at[pl.ds(src_ref[step * chunk + r], 1)],
                                     buf.at[sl, pl.ds(r, 1)], sem.at[sl])

    def issue(step, sl):
        def body(r, c):
            row_copy(step, sl, r).start()
            return c
        lax.fori_loop(0, chunk, body, 0, unroll=8)

    def drain(step, sl):
        def body(r, c):
            row_copy(step, sl, r).wait()
            return c
        lax.fori_loop(0, chunk, body, 0, unroll=8)

    @pl.when(i == 0)
    def _():
        issue(0, 0)

    @pl.when(i + 1 < pl.num_programs(0))
    def _():
        issue(i + 1, 1 - slot)

    drain(i, slot)
    o_ref[...] = buf[slot].astype(o_ref.dtype)


def _gather_rows_call(x, src, chunk):
    cap = src.shape[0]
    d = x.shape[1]
    return pl.pallas_call(
        functools.partial(_gather_rows_kernel, chunk=chunk),
        out_shape=jax.ShapeDtypeStruct((cap, d), bf16),
        grid_spec=pltpu.PrefetchScalarGridSpec(
            num_scalar_prefetch=1,
            grid=(cap // chunk,),
            in_specs=[pl.BlockSpec(memory_space=pl.ANY)],
            out_specs=pl.BlockSpec((chunk, d), lambda i, s: (i, 0)),
            scratch_shapes=[pltpu.VMEM((2, chunk, d), x.dtype), pltpu.SemaphoreType.DMA((2,))],
        ),
        compiler_params=_cparams(("arbitrary",), 32),
        name="moe_gather_rows",
    )(src, x)


def _expert_kernel(be_ref, nu_ref, x_ref, w1_ref, w3_ref, w2_ref, o_ref, acc_ref, a_ref, b_ref, *, nf):
    t = pl.program_id(0)
    tp = jnp.maximum(t - 1, 0)
    ip = tp // nf
    jp = lax.rem(tp, nf)
    n_live = nu_ref[0] * nf
    live = t <= n_live

    @pl.when(t == 0)
    def _():
        a_ref[1] = jnp.zeros(a_ref.shape[1:], f32)
        b_ref[1] = jnp.zeros(b_ref.shape[1:], f32)

    @pl.when(jnp.logical_and(jp == 0, live))
    def _():
        acc_ref[...] = jnp.zeros_like(acc_ref)

    @pl.when(live)
    def _():
        _swiglu_stream_step(t, x_ref, w1_ref.at[0], w3_ref.at[0], w2_ref.at[0], acc_ref, a_ref, b_ref)

    last = jnp.logical_and(jp == nf - 1, t > 0)

    @pl.when(jnp.logical_and(last, ip < nu_ref[0]))
    def _():
        o_ref[...] = acc_ref[...]

    @pl.when(jnp.logical_and(last, ip >= nu_ref[0]))
    def _():
        o_ref[...] = jnp.zeros_like(o_ref)


def _expert_call(xs, blk_e, n_used, w1, w3, w2, tm):
    cap, d = xs.shape
    _, nf, _, tf = w1.shape
    n_pairs = (cap // tm) * nf

    def pair(tt, nu):
        i, j = tt // nf, tt % nf
        return jnp.minimum(i, nu[0] - 1), jnp.where(i < nu[0], j, nf - 1)

    def cur(t, nu):
        return pair(jnp.minimum(t, n_pairs - 1), nu)

    def prev(t, nu):
        return pair(jnp.maximum(t - 1, 0), nu)

    def up_spec():
        return pl.BlockSpec((None, 1, d, tf), lambda t, be, nu: (be[cur(t, nu)[0]], cur(t, nu)[1], 0, 0))

    return pl.pallas_call(
        functools.partial(_expert_kernel, nf=nf),
        out_shape=jax.ShapeDtypeStruct((cap, d), f32),
        grid_spec=pltpu.PrefetchScalarGridSpec(
            num_scalar_prefetch=2,
            grid=(n_pairs + 1,),
            in_specs=[pl.BlockSpec((tm, d), lambda t, be, nu: (cur(t, nu)[0], 0)),
                      up_spec(), up_spec(),
                      pl.BlockSpec((1, tf, d), lambda t, be, nu: (be[prev(t, nu)[0]], prev(t, nu)[1], 0))],
            out_specs=pl.BlockSpec((tm, d), lambda t, be, nu: (jnp.maximum(t - 1, 0) // nf, 0)),
            scratch_shapes=[pltpu.VMEM((tm, d), f32), pltpu.VMEM((2, tm, tf), f32),
                            pltpu.VMEM((2, tm, tf), f32)],
        ),
        compiler_params=_cparams(("arbitrary",), 56),
        name="moe_expert_swiglu",
    )(blk_e, n_used, xs, w1, w3, w2)


def _combine_kernel(dest_ref, y_hbm, x_ref, g2_ref, wt_ref, o_ref, buf0, buf1, sem, *, tm):
    i = pl.program_id(0)
    slot = lax.rem(i, 2)

    def copies(step, sl, r):
        s = (step * tm + r) * TOP_K
        return (pltpu.make_async_copy(y_hbm.at[pl.ds(dest_ref[s], 1)], buf0.at[sl, pl.ds(r, 1)], sem.at[sl]),
                pltpu.make_async_copy(y_hbm.at[pl.ds(dest_ref[s + 1], 1)], buf1.at[sl, pl.ds(r, 1)],
                                      sem.at[sl]))

    def issue(step, sl):
        def body(r, c):
            c0, c1 = copies(step, sl, r)
            c0.start()
            c1.start()
            return c
        lax.fori_loop(0, tm, body, 0, unroll=8)

    def drain(step, sl):
        def body(r, c):
            c0, c1 = copies(step, sl, r)
            c0.wait()
            c1.wait()
            return c
        lax.fori_loop(0, tm, body, 0, unroll=8)

    @pl.when(i == 0)
    def _():
        issue(0, 0)

    @pl.when(i + 1 < pl.num_programs(0))
    def _():
        issue(i + 1, 1 - slot)

    drain(i, slot)
    wt = wt_ref[...]
    mix = wt[:, 0:1] * buf0[slot] + wt[:, 1:2] * buf1[slot]
    o_ref[...] = x_ref[...] + g2_ref[0] * mix


def _combine_call(yb, dest, x, modl, wts, seg_of_block, tm):
    n, d = x.shape
    return pl.pallas_call(
        functools.partial(_combine_kernel, tm=tm),
        out_shape=jax.ShapeDtypeStruct((n, d), f32),
        grid_spec=pltpu.PrefetchScalarGridSpec(
            num_scalar_prefetch=1,
            grid=(n // tm,),
            in_specs=[pl.BlockSpec(memory_space=pl.ANY),
                      pl.BlockSpec((tm, d), lambda i, dst: (i, 0)),
                      pl.BlockSpec((1, 1, d), lambda i, dst: (seg_of_block(i) * 6 + 5, 0, 0)),
                      pl.BlockSpec((tm, LANES), lambda i, dst: (i, 0))],
            out_specs=pl.BlockSpec((tm, d), lambda i, dst: (i, 0)),
            scratch_shapes=[pltpu.VMEM((2, tm, d), f32), pltpu.VMEM((2, tm, d), f32),
                            pltpu.SemaphoreType.DMA((2,))],
        ),
        compiler_params=_cparams(("arbitrary",), 32),
        name="moe_combine_residual",
    )(dest, yb, x, modl, wts)


def _moe_call(h2, x, modl, w_router, w1, w3, w2, seg_of_block):
    n, d = h2.shape
    n_exp = w_router.shape[1]
    tm = 512
    idx, wts = _router_call(h2, w_router, 256)
    flat_e = idx[:, :TOP_K].reshape(-1)
    n_slots = n * TOP_K
    onehot = (flat_e[:, None] == jnp.arange(n_exp, dtype=jnp.int32)[None, :]).astype(jnp.int32)
    csum = jnp.cumsum(onehot, axis=0)
    counts = csum[-1]
    rank = jnp.sum((csum - 1) * onehot, axis=1)
    padded = (counts + tm - 1) // tm * tm
    pend = jnp.cumsum(padded)
    pstart = pend - padded
    dest = (pstart[flat_e] + rank).astype(jnp.int32)
    cap = (n_slots // tm + n_exp) * tm
    src = jnp.zeros((cap,), jnp.int32).at[dest].set(jnp.arange(n_slots, dtype=jnp.int32) // TOP_K)
    n_blocks = cap // tm
    blk_start = jnp.arange(n_blocks, dtype=jnp.int32) * tm
    blk_e = jnp.minimum(jnp.searchsorted(pend, blk_start, side='right'), n_exp - 1).astype(jnp.int32)
    n_used = (pend[-1:] // tm).astype(jnp.int32)

    xs = _gather_rows_call(h2, src, 256)
    yb = _expert_call(xs, blk_e, n_used, w1, w3, w2, tm)
    return _combine_call(yb, dest, x, modl, wts, seg_of_block, 256)


def _rope_tables(nbatch, seq, ctx):
    t = jnp.arange(seq, dtype=jnp.int32)
    row = (t // GRID_W).astype(f32)
    col = (t % GRID_W).astype(f32)
    inv = ROPE_THETA ** (-jnp.arange(ROPE_PAIRS, dtype=f32) / ROPE_PAIRS)
    lane = jnp.arange(HEAD_DIM)
    pos = jnp.where(((lane // ROPE_PAIRS) % 2)[None, :] == 0, row[:, None], col[:, None])
    ang = pos * inv[lane % ROPE_PAIRS][None, :]
    cos = jnp.cos(ang)
    sin = jnp.where((lane < HEAD_DIM // 2)[None, :], -jnp.sin(ang), jnp.sin(ang))
    nctx = nbatch * ctx

    def full(tab, fill):
        return jnp.concatenate([jnp.tile(tab, (nbatch, 1)), jnp.full((nctx, HEAD_DIM), fill, f32)], axis=0)

    return full(cos, 1.0), full(sin, 0.0)


def _head_layout(w):
    lead = w.shape[:-1]
    w5 = w.reshape(lead + (w.shape[-1] // HEAD_DIM, 2, 2, ROPE_PAIRS))
    return jnp.swapaxes(w5, -3, -2).reshape(w.shape)


def kernel(x, c, ctx, c_ctx, w_mod, b_mod, norm1, norm2, w_in, qk_gain, a_sink, b_lambda, b_subln,
           w_branch, w_out, dense_w1, dense_w3, dense_w2, moe_router, moe_w1, moe_w3, moe_w2):
    nbatch, seq, d = x.shape
    nctx_len = ctx.shape[1]
    depth = w_mod.shape[0]
    nl = nbatch * seq
    nc = nbatch * nctx_len
    tm = 512
    tn_proj = 512

    def seg_of_block_for(t):
        return lambda i: jnp.minimum(i // (seq // t), nbatch)

    xs = jnp.concatenate([x.reshape(nl, d), ctx.reshape(nc, d)], axis=0)
    cc = jnp.zeros((8, d), f32).at[:nbatch].set(c).at[nbatch].set(c_ctx)
    mod = _mod_call(cc, w_mod, b_mod)
    cos, sin = _rope_tables(nbatch, seq, nctx_len)
    ones_mat = jnp.ones((HEAD_DIM, HEAD_DIM), bf16)
    scale = HEAD_DIM ** -0.5
    ones_cols = jnp.ones((HEAD_DIM,), f32)
    zero_w = jnp.zeros((d, HEAD_DIM), bf16)

    for l in range(depth):
        with_ctx = l < depth - 1
        lam_init = 0.8 - 0.6 * math.exp(-0.3 * l)
        rows = nl + nc if with_ctx else nl
        modl = mod[l].reshape(8 * 6, 1, d)

        wl = w_in[l]
        o = 0
        parts = {}
        for name, width in (("aq", A_Q), ("ak", A_KV), ("av", A_KV), ("bq", B_QK), ("bk", B_QK),
                            ("bv", B_V), ("cq", C_Q), ("ck", C_KV), ("cv", C_KV), ("g", 3 * d)):
            parts[name] = wl[:, o:o + width].astype(bf16)
            o += width
        w_qk = _head_layout(jnp.concatenate([parts[k] for k in ("cq", "aq", "bq", "bk", "ck", "ak")], axis=1))
        cv = parts["cv"]
        cv_cols = []
        for hd in range(C_KV_HEADS):
            cv_cols += [cv[:, hd * HEAD_DIM:(hd + 1) * HEAD_DIM], zero_w]
        w_v = jnp.concatenate([parts["bv"]] + cv_cols + [parts["av"]], axis=1)
        zeros_bv = jnp.zeros((B_V,), f32)
        zeros_h = jnp.zeros((HEAD_DIM,), f32)
        bias_v = jnp.concatenate([zeros_bv] + [zeros_h, ones_cols] * C_KV_HEADS
                                 + [jnp.zeros((A_KV,), f32)]).reshape(1, V_COLS)
        w_g = parts["g"]
        g = qk_gain[l]
        gain = _head_layout(jnp.concatenate(
            [jnp.tile(g[4] * (scale * LOG2E), C_HEADS), jnp.tile(g[0] * scale, A_HEADS),
             jnp.tile(g[2] * (scale * LOG2E), 2 * B_HEADS), jnp.tile(g[3], 2 * B_HEADS),
             jnp.tile(g[5], C_KV_HEADS), jnp.tile(g[1], A_KV_HEADS)]).reshape(1, QK_COLS))

        h = _norm_call(xs, norm1[l], modl, 1, 0, seg_of_block_for(tm), tm, bf16)
        tp = _row_tile(nl + nc)
        tab_spec = ((tp, HEAD_DIM), lambda i, j: (i, 0))
        qk = _deferred_mm_call(_qk_epilogue, h, _tile_major(w_qk, tn_proj), (gain, cos, sin, ones_mat),
                               [((1, tn_proj), lambda i, j: (0, j)), tab_spec, tab_spec,
                                ((HEAD_DIM, HEAD_DIM), lambda i, j: (0, 0))],
                               nl + nc, tp, "qk_proj_norm_rope")
        tn_v = V_COLS // 2
        v = _proj_call(_bias_proj_kernel, h, w_v, (bias_v,), [pl.BlockSpec((1, tn_v), lambda i, j: (0, j))],
                       nl + nc, tm, tn_v, "v_proj")
        gates = _deferred_mm_call(_gate_epilogue, h, _tile_major(w_g, tn_proj), (), [], rows,
                                  _row_tile(rows), "gate_proj")

        oa, ob, oc = _attention(qk, v, a_sink[l], b_lambda[l], b_subln[l], nbatch=nbatch, seq=seq,
                                ctx=nctx_len, with_ctx=with_ctx, lam_init=lam_init)
        y = _merge_call(oa, ob, oc, jnp.swapaxes(_tile_major(w_branch[l].astype(bf16), tn_proj), 0, 1),
                        gates, tm)
        moe_layer = l % 2 == 1
        xn, h2 = _out_proj_call(y, w_out[l].astype(bf16), xs, modl, norm2[l], seg_of_block_for(256), 256,
                                f32 if moe_layer else bf16)
        i = l // 2
        if not moe_layer:
            xs = _dense_ffn_call(h2, _tile_major(dense_w1[i].astype(bf16), tn_proj),
                                 _tile_major(dense_w3[i].astype(bf16), tn_proj),
                                 dense_w2[i].astype(bf16), xn, modl, seg_of_block_for(tm), tm)
        else:
            xs = _moe_call(h2, xn, modl, moe_router[i], _tile_major(moe_w1[i].astype(bf16), tn_proj),
                           _tile_major(moe_w3[i].astype(bf16), tn_proj), moe_w2[i].astype(bf16),
                           seg_of_block_for(256))
    return xs[:nl].reshape(nbatch, seq, d)
```

```python
import functools
import math

import jax
import jax.numpy as jnp
from jax import lax
from jax.experimental import pallas as pl
from jax.experimental.pallas import tpu as pltpu

f32 = jnp.float32
bf16 = jnp.bfloat16

HEAD_DIM = 128
GRID_W = 64
ROPE_PAIRS = HEAD_DIM // 4
ROPE_THETA = 10000.0
EPS = 1e-6
NEG_INF = -1e30
WINDOW = 128
A_HEADS, A_KV_HEADS = 8, 2
B_HEADS = 4
C_HEADS, C_KV_HEADS = 8, 2
GQA_GROUP = 4
TOP_K = 2
LANES = 128
MXU_WIDTH = 256
MIB = 1024 * 1024
LOG2E = math.log2(math.e)

A_Q, A_KV = A_HEADS * HEAD_DIM, A_KV_HEADS * HEAD_DIM
B_QK, B_V = B_HEADS * 2 * HEAD_DIM, B_HEADS * 2 * HEAD_DIM
C_Q, C_KV = C_HEADS * HEAD_DIM, C_KV_HEADS * HEAD_DIM
QK_CQ, QK_AQ, QK_BQ, QK_BK = 0, C_Q, C_Q + A_Q, C_Q + A_Q + B_QK
QK_CK = QK_BK + B_QK
QK_AK = QK_CK + C_KV
QK_COLS = QK_AK + A_KV
V_BV, V_CV = 0, B_V
V_AV = V_CV + C_KV_HEADS * 2 * HEAD_DIM
V_COLS = V_AV + A_KV


def _cparams(sem, vmem_mib):
    return pltpu.CompilerParams(dimension_semantics=sem, vmem_limit_bytes=vmem_mib * MIB)


def _dot(a, b):
    return jnp.dot(a, b, preferred_element_type=f32)


def _dot_nt(a, b):
    return lax.dot_general(a, b, (((1,), (1,)), ((), ())), preferred_element_type=f32)


def _sigmoid(x):
    return 0.5 * jnp.tanh(0.5 * x) + 0.5


def _rowmax(s):
    return jnp.max(s, axis=-1, keepdims=True)


def _rowsum(s):
    return jnp.sum(s, axis=-1, keepdims=True)


def _lane_tile(x, n):
    return jnp.concatenate([x] * n, axis=1)


def _mod_kernel(c_ref, w_ref, b_ref, o_ref):
    c = c_ref[...]
    a = (c * _sigmoid(c)).astype(bf16)
    o_ref[0] = _dot(a, w_ref[0].astype(bf16)) + b_ref[0]


def _mod_call(cc, w_mod, b_mod):
    depth, d, n = w_mod.shape
    tn = 1024
    return pl.pallas_call(
        _mod_kernel,
        out_shape=jax.ShapeDtypeStruct((depth, 8, n), f32),
        grid=(depth, n // tn),
        in_specs=[pl.BlockSpec((8, d), lambda l, j: (0, 0)),
                  pl.BlockSpec((1, d, tn), lambda l, j: (l, 0, j)),
                  pl.BlockSpec((1, 1, tn), lambda l, j: (l, 0, j))],
        out_specs=pl.BlockSpec((1, 8, tn), lambda l, j: (l, 0, j)),
        compiler_params=_cparams(("arbitrary", "arbitrary"), 40),
        name="mod_vectors",
    )(cc, w_mod, b_mod.reshape(depth, 1, n))


def _norm_kernel(x_ref, nw_ref, sc_ref, sh_ref, o_ref):
    x = x_ref[...]
    y = x * lax.rsqrt(jnp.mean(x * x, axis=-1, keepdims=True) + EPS) * nw_ref[...]
    o_ref[...] = (y * (1.0 + sc_ref[0]) + sh_ref[0]).astype(o_ref.dtype)


def _norm_call(x, nw, modl, k_sc, k_sh, seg_of_block, tm, out_dtype):
    r, d = x.shape
    return pl.pallas_call(
        _norm_kernel,
        out_shape=jax.ShapeDtypeStruct((r, d), out_dtype),
        grid=(r // tm,),
        in_specs=[pl.BlockSpec((tm, d), lambda i: (i, 0)),
                  pl.BlockSpec((1, d), lambda i: (0, 0)),
                  pl.BlockSpec((1, 1, d), lambda i: (seg_of_block(i) * 6 + k_sc, 0, 0)),
                  pl.BlockSpec((1, 1, d), lambda i: (seg_of_block(i) * 6 + k_sh, 0, 0))],
        out_specs=pl.BlockSpec((tm, d), lambda i: (i, 0)),
        compiler_params=_cparams(("arbitrary",), 32),
        name="norm_modulate",
    )(x, nw.reshape(1, d), modl, modl)


def _qk_epilogue(acc_ref, o_ref, gain_ref, cos_ref, sin_ref, ones_ref):
    cos, sin = cos_ref[...], sin_ref[...]
    for hd in range(o_ref.shape[1] // HEAD_DIM):
        sl = slice(hd * HEAD_DIM, (hd + 1) * HEAD_DIM)
        t = acc_ref[:, sl]
        ss = _dot((t * t).astype(bf16), ones_ref[...])
        y = t * lax.rsqrt(ss * (1.0 / HEAD_DIM) + EPS) * gain_ref[:, sl]
        o_ref[:, sl] = (y * cos + pltpu.roll(y, HEAD_DIM // 2, axis=1) * sin).astype(o_ref.dtype)


def _gate_epilogue(acc_ref, o_ref):
    o_ref[...] = _sigmoid(acc_ref[...]).astype(o_ref.dtype)


def _deferred_mm_kernel(h_ref, w_ref, *rest, epilogue, n_extra):
    extra, o_ref, acc_ref = rest[:n_extra], rest[n_extra], rest[n_extra + 1]
    t = pl.program_id(0)

    @pl.when(t == 0)
    def _():
        acc_ref[1] = jnp.zeros(acc_ref.shape[1:], f32)

    for slot in (0, 1):
        @pl.when(lax.rem(t, 2) == slot)
        def _(slot=slot):
            epilogue(acc_ref.at[1 - slot], o_ref, *extra)
            acc_ref[slot] = _dot(h_ref[...], w_ref[...])


def _tile_major(w, tn):
    lead, (k, n) = w.shape[:-2], w.shape[-2:]
    return jnp.swapaxes(w.reshape(lead + (k, n // tn, tn)), -3, -2)


def _row_tile(nrows):
    return next(t for t in (1024, 768, 512) if nrows % t == 0)


def _deferred_mm_call(epilogue, h, w, extra, extra_specs, nrows, tm, name):
    k = h.shape[1]
    nj, _, tn = w.shape
    n = nj * tn
    n_tiles = (nrows // tm) * nj

    def cur(t):
        tt = jnp.minimum(t, n_tiles - 1)
        return tt // nj, tt % nj

    def prev(t):
        tp = jnp.maximum(t - 1, 0)
        return tp // nj, tp % nj

    return pl.pallas_call(
        functools.partial(_deferred_mm_kernel, epilogue=epilogue, n_extra=len(extra)),
        out_shape=jax.ShapeDtypeStruct((nrows, n), bf16),
        grid=(n_tiles + 1,),
        in_specs=[pl.BlockSpec((tm, k), lambda t: (cur(t)[0], 0)),
                  pl.BlockSpec((None, k, tn), lambda t: (cur(t)[1], 0, 0))]
                 + [pl.BlockSpec(shape, functools.partial(lambda t, f: f(*prev(t)), f=f))
                    for shape, f in extra_specs],
        out_specs=pl.BlockSpec((tm, tn), lambda t: prev(t)),
        scratch_shapes=[pltpu.VMEM((2, tm, tn), f32)],
        compiler_params=_cparams(("arbitrary",), 40),
        name=name,
    )(h, w, *extra)


def _bias_proj_kernel(h_ref, w_ref, b_ref, o_ref):
    o_ref[...] = (_dot(h_ref[...], w_ref[...]) + b_ref[...]).astype(o_ref.dtype)


def _proj_call(kernel, h, w, extra, extra_specs, nrows, tm, tn, name):
    k = h.shape[1]
    n = w.shape[1]
    return pl.pallas_call(
        kernel,
        out_shape=jax.ShapeDtypeStruct((nrows, n), bf16),
        grid=(nrows // tm, n // tn),
        in_specs=[pl.BlockSpec((tm, k), lambda i, j: (i, 0)),
                  pl.BlockSpec((k, tn), lambda i, j: (0, j))] + extra_specs,
        out_specs=pl.BlockSpec((tm, tn), lambda i, j: (i, j)),
        compiler_params=_cparams(("arbitrary", "arbitrary"), 40),
        name=name,
    )(h, w, *extra)


def _stack_heads(q, n):
    return jnp.concatenate([q[:, g * HEAD_DIM:(g + 1) * HEAD_DIM] for g in range(n)], axis=0)


def _unstack_heads(o, n):
    t = o.shape[0] // n
    return jnp.concatenate([o[g * t:(g + 1) * t] for g in range(n)], axis=1)


def _softmax_pv(s, v, ex, extra_logit=None):
    m = _rowmax(s)
    if extra_logit is not None:
        m = jnp.maximum(m, extra_logit)
    p = ex(s - m)
    l = _rowsum(p)
    if extra_logit is not None:
        l = l + ex(extra_logit - m)
    return _dot(p.astype(bf16), v) / l


def _sink_column(sink_ref, kh, tq):
    return jnp.concatenate(
        [jnp.full((tq, 1), sink_ref[kh * GQA_GROUP + g], f32) for g in range(GQA_GROUP)], axis=0)


def _diff_finish(on0, on1, lam_ref, g_ref, lam_init):
    bl = lam_ref[...]
    lam = (jnp.exp(_rowsum(bl[0:1] * bl[1:2])) - jnp.exp(_rowsum(bl[2:3] * bl[3:4])) + lam_init)
    o = on0 - lam * on1
    y = o * lax.rsqrt(jnp.mean(o * o, axis=-1, keepdims=True) + EPS) * g_ref[...]
    return y * (1.0 - lam_init)


def _attn_a_lat_kernel(sink_ref, q_ref, kl_ref, kc_ref, vl_ref, vc_ref, o_ref, *, tq, seq):
    kh = pl.program_id(1)
    qi = pl.program_id(2)
    qs = _stack_heads(q_ref[...], GQA_GROUP)
    q0 = qi * tq
    band = tq + 2 * WINDOW
    start = pl.multiple_of(jnp.clip(q0 - WINDOW, 0, seq - band), WINDOW)
    kb = kl_ref[pl.ds(start, band), :]
    vb = vl_ref[pl.ds(start, band), :]
    s_ctx = _dot_nt(qs, kc_ref[...])
    s_loc = _dot_nt(qs, kb)
    m_rows = GQA_GROUP * tq
    qpos = q0 + (lax.broadcasted_iota(jnp.int32, (m_rows, band), 0) & (tq - 1))
    kpos = start + lax.broadcasted_iota(jnp.int32, (m_rows, band), 1)
    s_loc = jnp.where(jnp.abs(qpos - kpos) <= WINDOW, s_loc, NEG_INF)
    sink = _sink_column(sink_ref, kh, tq)
    m = jnp.maximum(jnp.maximum(_rowmax(s_ctx), _rowmax(s_loc)), sink)
    p_ctx = jnp.exp(s_ctx - m)
    p_loc = jnp.exp(s_loc - m)
    l = _rowsum(p_ctx) + _rowsum(p_loc) + jnp.exp(sink - m)
    o = (_dot(p_ctx.astype(bf16), vc_ref[...]) + _dot(p_loc.astype(bf16), vb)) / l
    o_ref[...] = _unstack_heads(o, GQA_GROUP).astype(o_ref.dtype)


def _attn_a_ctx_kernel(sink_ref, q_ref, kc_ref, vc_ref, o_ref, *, tq):
    qs = _stack_heads(q_ref[...], GQA_GROUP)
    o = _softmax_pv(_dot_nt(qs, kc_ref[...]), vc_ref[...], jnp.exp,
                    _sink_column(sink_ref, pl.program_id(1), tq))
    o_ref[...] = _unstack_heads(o, GQA_GROUP).astype(o_ref.dtype)


def _attn_c_lat_kernel(q_ref, kl_ref, kc_ref, vl_ref, vc_ref, o_ref, m_ref, acc_ref, *, tk, n_chunks):
    qs = _stack_heads(q_ref[...], GQA_GROUP)
    s = _dot_nt(qs, kc_ref[...])
    m0 = _rowmax(s)
    m_ref[...] = jnp.broadcast_to(m0, m_ref.shape)
    acc_ref[...] = _dot(jnp.exp2(s - m0).astype(bf16), vc_ref[...])

    def chunk(c, carry):
        off = pl.multiple_of(c * tk, tk)
        s = _dot_nt(qs, kl_ref[pl.ds(off, tk), :])
        m_old = m_ref[...]
        m_new = jnp.maximum(m_old, _rowmax(s))
        alpha = jnp.exp2(m_old - m_new)
        p = jnp.exp2(s - _lane_tile(m_new, tk // LANES))
        acc_ref[...] = (_lane_tile(alpha, 2) * acc_ref[...]
                        + _dot(p.astype(bf16), vl_ref[pl.ds(off, tk), :]))
        m_ref[...] = m_new
        return carry

    lax.fori_loop(0, n_chunks, chunk, 0, unroll=min(n_chunks, 8 * 512 // tk))
    acc = acc_ref[...]
    o = acc[:, :HEAD_DIM] / acc[:, HEAD_DIM:]
    o_ref[...] = _unstack_heads(o, GQA_GROUP).astype(o_ref.dtype)


def _attn_c_ctx_kernel(q_ref, kc_ref, vc_ref, o_ref):
    qs = _stack_heads(q_ref[...], GQA_GROUP)
    o = _softmax_pv(_dot_nt(qs, kc_ref[...]), vc_ref[...], jnp.exp2)
    o_ref[...] = _unstack_heads(o, GQA_GROUP).astype(o_ref.dtype)


def _diff_scores(q, k):
    return jnp.concatenate([_dot_nt(q[:, :HEAD_DIM], k[:, :HEAD_DIM]),
                            _dot_nt(q[:, HEAD_DIM:], k[:, HEAD_DIM:])], axis=0)


def _attn_b_lat_kernel(q_ref, kl_ref, kc_ref, vl_ref, vc_ref, lam_ref, g_ref, o_ref,
                       m_ref, l_ref, acc_ref, *, tk, n_chunks, lam_init):
    q = q_ref[...]
    tq = q.shape[0]
    s = _diff_scores(q, kc_ref[...])
    m0 = _rowmax(s)
    p = jnp.exp2(s - m0)
    m_ref[...] = jnp.broadcast_to(m0, m_ref.shape)
    l_ref[...] = jnp.broadcast_to(_rowsum(p), l_ref.shape)
    acc_ref[...] = _dot(p.astype(bf16), vc_ref[...])

    def chunk(c, carry):
        off = pl.multiple_of(c * tk, tk)
        s = _diff_scores(q, kl_ref[pl.ds(off, tk), :])
        m_old = m_ref[...]
        m_new = jnp.maximum(m_old, _rowmax(s))
        alpha = jnp.exp2(m_old - m_new)
        p = jnp.exp2(s - _lane_tile(m_new, tk // LANES))
        l_ref[...] = alpha * l_ref[...] + _rowsum(p)
        acc_ref[...] = (_lane_tile(alpha, 2) * acc_ref[...]
                        + _dot(p.astype(bf16), vl_ref[pl.ds(off, tk), :]))
        m_ref[...] = m_new
        return carry

    lax.fori_loop(0, n_chunks, chunk, 0, unroll=min(n_chunks, 8 * 512 // tk))
    on = acc_ref[...] / _lane_tile(l_ref[...], 2)
    o_ref[...] = _diff_finish(on[:tq], on[tq:], lam_ref, g_ref, lam_init).astype(o_ref.dtype)


def _attn_b_ctx_kernel(q_ref, kc_ref, vc_ref, lam_ref, g_ref, o_ref, *, lam_init):
    q = q_ref[...]
    tq = q.shape[0]
    on = _softmax_pv(_diff_scores(q, kc_ref[...]), vc_ref[...], jnp.exp2)
    o_ref[...] = _diff_finish(on[:tq], on[tq:], lam_ref, g_ref, lam_init).astype(o_ref.dtype)


def _attention(qk, v, a_sink, b_lam, b_subln, *, nbatch, seq, ctx, with_ctx, lam_init):
    tq_win = 256
    tq_gqa = 256
    tq_diff = min(512, seq)
    tq_ctx = ctx
    tk = min(512, seq)
    nl_rows = nbatch * seq
    nc_rows = nbatch * ctx
    cb = nl_rows // ctx
    gw = GQA_GROUP * HEAD_DIM
    bw = 2 * HEAD_DIM
    sem = ("arbitrary", "arbitrary", "arbitrary")

    def lat_q(tq, width, off):
        return pl.BlockSpec((tq, width), lambda b, h, qi, *_: (b * (seq // tq) + qi, off // width + h))

    def ctx_q(width, off):
        return pl.BlockSpec((tq_ctx, width), lambda b, h, qi, *_: (cb + b, off // width + h))

    def lat_kv(width, off):
        return pl.BlockSpec((seq, width), lambda b, h, qi, *_: (b, off // width + h))

    def ctx_kv(width, off):
        return pl.BlockSpec((ctx, width), lambda b, h, qi, *_: (cb + b, off // width + h))

    def ctx_out(width):
        return pl.BlockSpec((tq_ctx, width), lambda b, h, qi, *_: (b, h))

    oa = pl.pallas_call(
        functools.partial(_attn_a_lat_kernel, tq=tq_win, seq=seq),
        out_shape=jax.ShapeDtypeStruct((nl_rows, A_Q), bf16),
        grid_spec=pltpu.PrefetchScalarGridSpec(
            num_scalar_prefetch=1,
            grid=(nbatch, A_KV_HEADS, seq // tq_win),
            in_specs=[lat_q(tq_win, gw, QK_AQ), lat_kv(HEAD_DIM, QK_AK), ctx_kv(HEAD_DIM, QK_AK),
                      lat_kv(HEAD_DIM, V_AV), ctx_kv(HEAD_DIM, V_AV)],
            out_specs=lat_q(tq_win, gw, 0),
        ),
        compiler_params=_cparams(sem, 40),
        name="attn_window",
    )(a_sink, qk, qk, qk, v, v)
    if with_ctx:
        oa_ctx = pl.pallas_call(
            functools.partial(_attn_a_ctx_kernel, tq=tq_ctx),
            out_shape=jax.ShapeDtypeStruct((nc_rows, A_Q), bf16),
            grid_spec=pltpu.PrefetchScalarGridSpec(
                num_scalar_prefetch=1,
                grid=(nbatch, A_KV_HEADS, 1),
                in_specs=[ctx_q(gw, QK_AQ), ctx_kv(HEAD_DIM, QK_AK), ctx_kv(HEAD_DIM, V_AV)],
                out_specs=ctx_out(gw),
            ),
            compiler_params=_cparams(sem, 32),
            name="attn_window_ctx",
        )(a_sink, qk, qk, v)
        oa = jnp.concatenate([oa, oa_ctx], axis=0)

    m_rows = GQA_GROUP * tq_gqa
    oc = pl.pallas_call(
        functools.partial(_attn_c_lat_kernel, tk=tk, n_chunks=seq // tk),
        out_shape=jax.ShapeDtypeStruct((nl_rows, C_Q), bf16),
        grid=(nbatch, C_KV_HEADS, seq // tq_gqa),
        in_specs=[lat_q(tq_gqa, gw, QK_CQ), lat_kv(HEAD_DIM, QK_CK), ctx_kv(HEAD_DIM, QK_CK),
                  lat_kv(2 * HEAD_DIM, V_CV), ctx_kv(2 * HEAD_DIM, V_CV)],
        out_specs=lat_q(tq_gqa, gw, 0),
        scratch_shapes=[pltpu.VMEM((m_rows, LANES), f32), pltpu.VMEM((m_rows, 2 * HEAD_DIM), f32)],
        compiler_params=_cparams(sem, 56),
        name="attn_global",
    )(qk, qk, qk, v, v)
    if with_ctx:
        oc_ctx = pl.pallas_call(
            _attn_c_ctx_kernel,
            out_shape=jax.ShapeDtypeStruct((nc_rows, C_Q), bf16),
            grid=(nbatch, C_KV_HEADS, 1),
            in_specs=[ctx_q(gw, QK_CQ), ctx_kv(HEAD_DIM, QK_CK),
                      pl.BlockSpec((ctx, HEAD_DIM), lambda b, h, qi: (cb + b, V_CV // HEAD_DIM + 2 * h))],
            out_specs=ctx_out(gw),
            compiler_params=_cparams(sem, 32),
            name="attn_global_ctx",
        )(qk, qk, v)
        oc = jnp.concatenate([oc, oc_ctx], axis=0)

    lam_spec = pl.BlockSpec((4, HEAD_DIM), lambda b, h, qi: (0, 0))
    g_spec = pl.BlockSpec((1, bw), lambda b, h, qi: (0, 0))
    g2d = b_subln.reshape(1, bw)
    ob = pl.pallas_call(
        functools.partial(_attn_b_lat_kernel, tk=tk, n_chunks=seq // tk, lam_init=lam_init),
        out_shape=jax.ShapeDtypeStruct((nl_rows, B_V), bf16),
        grid=(nbatch, B_HEADS, seq // tq_diff),
        in_specs=[lat_q(tq_diff, bw, QK_BQ), lat_kv(bw, QK_BK), ctx_kv(bw, QK_BK),
                  lat_kv(bw, V_BV), ctx_kv(bw, V_BV), lam_spec, g_spec],
        out_specs=lat_q(tq_diff, bw, 0),
        scratch_shapes=[pltpu.VMEM((2 * tq_diff, LANES), f32), pltpu.VMEM((2 * tq_diff, LANES), f32),
                        pltpu.VMEM((2 * tq_diff, bw), f32)],
        compiler_params=_cparams(sem, 56),
        name="attn_diff",
    )(qk, qk, qk, v, v, b_lam, g2d)
    if with_ctx:
        ob_ctx = pl.pallas_call(
            functools.partial(_attn_b_ctx_kernel, lam_init=lam_init),
            out_shape=jax.ShapeDtypeStruct((nc_rows, B_V), bf16),
            grid=(nbatch, B_HEADS, 1),
            in_specs=[ctx_q(bw, QK_BQ), ctx_kv(bw, QK_BK), ctx_kv(bw, V_BV), lam_spec, g_spec],
            out_specs=ctx_out(bw),
            compiler_params=_cparams(sem, 32),
            name="attn_diff_ctx",
        )(qk, qk, v, b_lam, g2d)
        ob = jnp.concatenate([ob, ob_ctx], axis=0)
    return oa, ob, oc


def _merge_kernel(oa_ref, ob_ref, oc_ref, w_ref, ga_ref, gb_ref, gc_ref, y_ref):
    y = (ga_ref[...].astype(f32) * _dot(oa_ref[...], w_ref[0])
         + gb_ref[...].astype(f32) * _dot(ob_ref[...], w_ref[1])
         + gc_ref[...].astype(f32) * _dot(oc_ref[...], w_ref[2]))
    y_ref[...] = y.astype(y_ref.dtype)


def _merge_call(oa, ob, oc, wb, gates, tm):
    rows, bwid = oa.shape
    nj, _, _, tn = wb.shape
    d = nj * tn
    o_spec = pl.BlockSpec((tm, bwid), lambda i, j: (i, 0))
    return pl.pallas_call(
        _merge_kernel,
        out_shape=jax.ShapeDtypeStruct((rows, d), bf16),
        grid=(rows // tm, nj),
        in_specs=[o_spec, o_spec, o_spec,
                  pl.BlockSpec((None, 3, bwid, tn), lambda i, j: (j, 0, 0, 0)),
                  pl.BlockSpec((tm, tn), lambda i, j: (i, j)),
                  pl.BlockSpec((tm, tn), lambda i, j: (i, nj + j)),
                  pl.BlockSpec((tm, tn), lambda i, j: (i, 2 * nj + j))],
        out_specs=pl.BlockSpec((tm, tn), lambda i, j: (i, j)),
        compiler_params=_cparams(("arbitrary", "arbitrary"), 40),
        name="branch_merge",
    )(oa, ob, oc, wb, gates, gates, gates)


def _out_proj_kernel(y_ref, w_ref, x_ref, g1_ref, nw_ref, sc_ref, sh_ref, xo_ref, h_ref):
    xn = x_ref[...] + g1_ref[0] * _dot(y_ref[...], w_ref[...])
    xo_ref[...] = xn
    t = xn * lax.rsqrt(jnp.mean(xn * xn, axis=-1, keepdims=True) + EPS) * nw_ref[...]
    h_ref[...] = (t * (1.0 + sc_ref[0]) + sh_ref[0]).astype(h_ref.dtype)


def _out_proj_call(y, w_out, x, modl, nw2, seg_of_block, tm, h_dtype):
    rows, d = y.shape

    def mod_spec(k):
        return pl.BlockSpec((1, 1, d), lambda i: (seg_of_block(i) * 6 + k, 0, 0))

    row_spec = pl.BlockSpec((tm, d), lambda i: (i, 0))
    return pl.pallas_call(
        _out_proj_kernel,
        out_shape=(jax.ShapeDtypeStruct((rows, d), f32), jax.ShapeDtypeStruct((rows, d), h_dtype)),
        grid=(rows // tm,),
        in_specs=[row_spec, pl.BlockSpec((d, d), lambda i: (0, 0)), row_spec,
                  mod_spec(2), pl.BlockSpec((1, d), lambda i: (0, 0)), mod_spec(4), mod_spec(3)],
        out_specs=(row_spec, row_spec),
        compiler_params=_cparams(("arbitrary",), 48),
        name="out_proj_residual_norm",
    )(y, w_out, x, modl, nw2.reshape(1, d), modl, modl)


def _swiglu_stream_step(t, h_ref, w1_ref, w3_ref, w2_ref, acc_ref, a_ref, b_ref):
    for slot in (0, 1):
        @pl.when(lax.rem(t, 2) == slot)
        def _(slot=slot):
            h = h_ref[...]
            a_ref[slot] = _dot(h, w1_ref[...].astype(bf16))
            a = a_ref[1 - slot]
            b = b_ref[1 - slot]
            g = (a * _sigmoid(a) * b).astype(bf16)
            acc_ref[...] += _dot(g, w2_ref[...])
            b_ref[slot] = _dot(h, w3_ref[...].astype(bf16))


def _dense_ffn_kernel(h_ref, w1_ref, w3_ref, w2_ref, x_ref, g2_ref, o_ref, acc_ref, a_ref, b_ref, *, nf):
    t = pl.program_id(0)
    jp = lax.rem(jnp.maximum(t - 1, 0), nf)

    @pl.when(t == 0)
    def _():
        a_ref[1] = jnp.zeros(a_ref.shape[1:], f32)
        b_ref[1] = jnp.zeros(b_ref.shape[1:], f32)

    @pl.when(jp == 0)
    def _():
        acc_ref[...] = jnp.zeros_like(acc_ref)

    _swiglu_stream_step(t, h_ref, w1_ref, w3_ref, w2_ref, acc_ref, a_ref, b_ref)

    @pl.when(jnp.logical_and(jp == nf - 1, t > 0))
    def _():
        o_ref[...] = x_ref[...] + g2_ref[0] * acc_ref[...]


def _dense_ffn_call(h2, w1, w3, w2, x, modl, seg_of_block, tm):
    rows, d = h2.shape
    nf, _, tf = w1.shape
    n_pairs = (rows // tm) * nf

    def cur(t):
        tt = jnp.minimum(t, n_pairs - 1)
        return tt // nf, tt % nf

    def prev(t):
        tp = jnp.maximum(t - 1, 0)
        return tp // nf, tp % nf

    prev_row = pl.BlockSpec((tm, d), lambda t: (prev(t)[0], 0))
    return pl.pallas_call(
        functools.partial(_dense_ffn_kernel, nf=nf),
        out_shape=jax.ShapeDtypeStruct((rows, d), f32),
        grid=(n_pairs + 1,),
        in_specs=[pl.BlockSpec((tm, d), lambda t: (cur(t)[0], 0)),
                  pl.BlockSpec((None, d, tf), lambda t: (cur(t)[1], 0, 0)),
                  pl.BlockSpec((None, d, tf), lambda t: (cur(t)[1], 0, 0)),
                  pl.BlockSpec((tf, d), lambda t: (prev(t)[1], 0)),
                  prev_row,
                  pl.BlockSpec((1, 1, d), lambda t: (seg_of_block(prev(t)[0]) * 6 + 5, 0, 0))],
        out_specs=prev_row,
        scratch_shapes=[pltpu.VMEM((tm, d), f32), pltpu.VMEM((2, tm, tf), f32), pltpu.VMEM((2, tm, tf), f32)],
        compiler_params=_cparams(("arbitrary",), 56),
        name="dense_swiglu",
    )(h2, w1, w3, w2, x, modl)


def _router_kernel(h_ref, w_ref, idx_ref, wt_ref, *, n_experts):
    logits = jnp.dot(h_ref[...], w_ref[...], preferred_element_type=f32,
                     precision=lax.Precision.HIGHEST)
    lane = lax.broadcasted_iota(jnp.int32, logits.shape, 1)
    lg = jnp.where(lane < n_experts, logits, -jnp.inf)
    m1 = _rowmax(lg)
    i1 = jnp.min(jnp.where(lg == m1, lane, LANES), axis=-1, keepdims=True)
    lg2 = jnp.where(lane == i1, -jnp.inf, lg)
    m2 = _rowmax(lg2)
    i2 = jnp.min(jnp.where(lg2 == m2, lane, LANES), axis=-1, keepdims=True)
    e2 = jnp.exp(m2 - m1)
    w1 = 1.0 / (1.0 + e2)
    w2 = e2 / (1.0 + e2)
    idx_ref[...] = jnp.where(lane == 0, i1, jnp.where(lane == 1, i2, 0))
    wt_ref[...] = jnp.where(lane == 0, w1, jnp.where(lane == 1, w2, 0.0))


def _router_call(h2, w_router, tm):
    n, d = h2.shape
    e = w_router.shape[1]
    wp = jnp.zeros((d, LANES), f32).at[:, :e].set(w_router)
    return pl.pallas_call(
        functools.partial(_router_kernel, n_experts=e),
        out_shape=(jax.ShapeDtypeStruct((n, LANES), jnp.int32), jax.ShapeDtypeStruct((n, LANES), f32)),
        grid=(n // tm,),
        in_specs=[pl.BlockSpec((tm, d), lambda i: (i, 0)), pl.BlockSpec((d, LANES), lambda i: (0, 0))],
        out_specs=(pl.BlockSpec((tm, LANES), lambda i: (i, 0)), pl.BlockSpec((tm, LANES), lambda i: (i, 0))),
        compiler_params=_cparams(("arbitrary",), 32),
        name="moe_router_top2",
    )(h2, wp)


def _gather_rows_kernel(src_ref, x_hbm, o_ref, buf, sem, *, chunk):
    i = pl.program_id(0)
    slot = lax.rem(i, 2)

    def row_copy(step, sl, r):
        return pltpu.make_async_copy(x_hbm.at[pl.ds(src_ref[step * chunk + r], 1)],
                                     buf.at[sl, pl.ds(r, 1)], sem.at[sl])

    def issue(step, sl):
        def body(r, c):
            row_copy(step, sl, r).start()
            return c
        lax.fori_loop(0, chunk, body, 0, unroll=8)

    def drain(step, sl):
        def body(r, c):
            row_copy(step, sl, r).wait()
            return c
        lax.fori_loop(0, chunk, body, 0, unroll=8)

    @pl.when(i == 0)
    def _():
        issue(0, 0)

    @pl.when(i + 1 < pl.num_programs(0))
    def _():
        issue(i + 1, 1 - slot)

    drain(i, slot)
    o_ref[...] = buf[slot].astype(o_ref.dtype)


def _gather_rows_call(x, src, chunk):
    cap = src.shape[0]
    d = x.shape[1]
    return pl.pallas_call(
        functools.partial(_gather_rows_kernel, chunk=chunk),
        out_shape=jax.ShapeDtypeStruct((cap, d), bf16),
        grid_spec=pltpu.PrefetchScalarGridSpec(
            num_scalar_prefetch=1,
            grid=(cap // chunk,),
            in_specs=[pl.BlockSpec(memory_space=pl.ANY)],
            out_specs=pl.BlockSpec((chunk, d), lambda i, s: (i, 0)),
            scratch_shapes=[pltpu.VMEM((2, chunk, d), x.dtype), pltpu.SemaphoreType.DMA((2,))],
        ),
        compiler_params=_cparams(("arbitrary",), 32),
        name="moe_gather_rows",
    )(src, x)


def _expert_kernel(be_ref, nu_ref, x_ref, w1_ref, w3_ref, w2_ref, o_ref, acc_ref, a_ref, b_ref, *, nf):
    t = pl.program_id(0)
    tp = jnp.maximum(t - 1, 0)
    ip = tp // nf
    jp = lax.rem(tp, nf)
    n_live = nu_ref[0] * nf
    live = t <= n_live

    @pl.when(t == 0)
    def _():
        a_ref[1] = jnp.zeros(a_ref.shape[1:], f32)
        b_ref[1] = jnp.zeros(b_ref.shape[1:], f32)

    @pl.when(jnp.logical_and(jp == 0, live))
    def _():
        acc_ref[...] = jnp.zeros_like(acc_ref)

    @pl.when(live)
    def _():
        _swiglu_stream_step(t, x_ref, w1_ref.at[0], w3_ref.at[0], w2_ref.at[0], acc_ref, a_ref, b_ref)

    last = jnp.logical_and(jp == nf - 1, t > 0)

    @pl.when(jnp.logical_and(last, ip < nu_ref[0]))
    def _():
        o_ref[...] = acc_ref[...]

    @pl.when(jnp.logical_and(last, ip >= nu_ref[0]))
    def _():
        o_ref[...] = jnp.zeros_like(o_ref)


def _expert_call(xs, blk_e, n_used, w1, w3, w2, tm, tf):
    cap, d = xs.shape
    nf = w1.shape[2] // tf
    n_pairs = (cap // tm) * nf

    def pair(tt, nu):
        i, j = tt // nf, tt % nf
        return jnp.minimum(i, nu[0] - 1), jnp.where(i < nu[0], j, nf - 1)

    def cur(t, nu):
        return pair(jnp.minimum(t, n_pairs - 1), nu)

    def prev(t, nu):
        return pair(jnp.maximum(t - 1, 0), nu)

    def up_spec():
        return pl.BlockSpec((1, d, tf), lambda t, be, nu: (be[cur(t, nu)[0]], 0, cur(t, nu)[1]))

    return pl.pallas_call(
        functools.partial(_expert_kernel, nf=nf),
        out_shape=jax.ShapeDtypeStruct((cap, d), f32),
        grid_spec=pltpu.PrefetchScalarGridSpec(
            num_scalar_prefetch=2,
            grid=(n_pairs + 1,),
            in_specs=[pl.BlockSpec((tm, d), lambda t, be, nu: (cur(t, nu)[0], 0)),
                      up_spec(), up_spec(),
                      pl.BlockSpec((1, tf, d), lambda t, be, nu: (be[prev(t, nu)[0]], prev(t, nu)[1], 0))],
            out_specs=pl.BlockSpec((tm, d), lambda t, be, nu: (jnp.maximum(t - 1, 0) // nf, 0)),
            scratch_shapes=[pltpu.VMEM((tm, d), f32), pltpu.VMEM((2, tm, tf), f32),
                            pltpu.VMEM((2, tm, tf), f32)],
        ),
        compiler_params=_cparams(("arbitrary",), 56),
        name="moe_expert_swiglu",
    )(blk_e, n_used, xs, w1, w3, w2)


def _combine_kernel(dest_ref, y_hbm, x_ref, g2_ref, wt_ref, o_ref, buf0, buf1, sem, *, tm):
    i = pl.program_id(0)
    slot = lax.rem(i, 2)

    def copies(step, sl, r):
        s = (step * tm + r) * TOP_K
        return (pltpu.make_async_copy(y_hbm.at[pl.ds(dest_ref[s], 1)], buf0.at[sl, pl.ds(r, 1)], sem.at[sl]),
                pltpu.make_async_copy(y_hbm.at[pl.ds(dest_ref[s + 1], 1)], buf1.at[sl, pl.ds(r, 1)],
                                      sem.at[sl]))

    def issue(step, sl):
        def body(r, c):
            c0, c1 = copies(step, sl, r)
            c0.start()
            c1.start()
            return c
        lax.fori_loop(0, tm, body, 0, unroll=8)

    def drain(step, sl):
        def body(r, c):
            c0, c1 = copies(step, sl, r)
            c0.wait()
            c1.wait()
            return c
        lax.fori_loop(0, tm, body, 0, unroll=8)

    @pl.when(i == 0)
    def _():
        issue(0, 0)

    @pl.when(i + 1 < pl.num_programs(0))
    def _():
        issue(i + 1, 1 - slot)

    drain(i, slot)
    wt = wt_ref[...]
    mix = wt[:, 0:1] * buf0[slot] + wt[:, 1:2] * buf1[slot]
    o_ref[...] = x_ref[...] + g2_ref[0] * mix


def _combine_call(yb, dest, x, modl, wts, seg_of_block, tm):
    n, d = x.shape
    return pl.pallas_call(
        functools.partial(_combine_kernel, tm=tm),
        out_shape=jax.ShapeDtypeStruct((n, d), f32),
        grid_spec=pltpu.PrefetchScalarGridSpec(
            num_scalar_prefetch=1,
            grid=(n // tm,),
            in_specs=[pl.BlockSpec(memory_space=pl.ANY),
                      pl.BlockSpec((tm, d), lambda i, dst: (i, 0)),
                      pl.BlockSpec((1, 1, d), lambda i, dst: (seg_of_block(i) * 6 + 5, 0, 0)),
                      pl.BlockSpec((tm, LANES), lambda i, dst: (i, 0))],
            out_specs=pl.BlockSpec((tm, d), lambda i, dst: (i, 0)),
            scratch_shapes=[pltpu.VMEM((2, tm, d), f32), pltpu.VMEM((2, tm, d), f32),
                            pltpu.SemaphoreType.DMA((2,))],
        ),
        compiler_params=_cparams(("arbitrary",), 32),
        name="moe_combine_residual",
    )(dest, yb, x, modl, wts)


def _moe_call(h2, x, modl, w_router, w1, w3, w2, seg_of_block):
    n, d = h2.shape
    n_exp = w_router.shape[1]
    tm = 512
    idx, wts = _router_call(h2, w_router, 256)
    flat_e = idx[:, :TOP_K].reshape(-1)
    n_slots = n * TOP_K
    onehot = (flat_e[:, None] == jnp.arange(n_exp, dtype=jnp.int32)[None, :]).astype(jnp.int32)
    csum = jnp.cumsum(onehot, axis=0)
    counts = csum[-1]
    rank = jnp.sum((csum - 1) * onehot, axis=1)
    padded = (counts + tm - 1) // tm * tm
    pend = jnp.cumsum(padded)
    pstart = pend - padded
    dest = (pstart[flat_e] + rank).astype(jnp.int32)
    cap = (n_slots // tm + n_exp) * tm
    src = jnp.zeros((cap,), jnp.int32).at[dest].set(jnp.arange(n_slots, dtype=jnp.int32) // TOP_K)
    n_blocks = cap // tm
    blk_start = jnp.arange(n_blocks, dtype=jnp.int32) * tm
    blk_e = jnp.minimum(jnp.searchsorted(pend, blk_start, side='right'), n_exp - 1).astype(jnp.int32)
    n_used = (pend[-1:] // tm).astype(jnp.int32)

    xs = _gather_rows_call(h2, src, 256)
    yb = _expert_call(xs, blk_e, n_used, w1, w3, w2, tm, 512)
    return _combine_call(yb, dest, x, modl, wts, seg_of_block, 256)


def _rope_tables(nbatch, seq, ctx):
    t = jnp.arange(seq, dtype=jnp.int32)
    row = (t // GRID_W).astype(f32)
    col = (t % GRID_W).astype(f32)
    inv = ROPE_THETA ** (-jnp.arange(ROPE_PAIRS, dtype=f32) / ROPE_PAIRS)
    lane = jnp.arange(HEAD_DIM)
    pos = jnp.where(((lane // ROPE_PAIRS) % 2)[None, :] == 0, row[:, None], col[:, None])
    ang = pos * inv[lane % ROPE_PAIRS][None, :]
    cos = jnp.cos(ang)
    sin = jnp.where((lane < HEAD_DIM // 2)[None, :], -jnp.sin(ang), jnp.sin(ang))
    nctx = nbatch * ctx

    def full(tab, fill):
        return jnp.concatenate([jnp.tile(tab, (nbatch, 1)), jnp.full((nctx, HEAD_DIM), fill, f32)], axis=0)

    return full(cos, 1.0), full(sin, 0.0)


def _head_layout(w):
    lead = w.shape[:-1]
    w5 = w.reshape(lead + (w.shape[-1] // HEAD_DIM, 2, 2, ROPE_PAIRS))
    return jnp.swapaxes(w5, -3, -2).reshape(w.shape)


def kernel(x, c, ctx, c_ctx, w_mod, b_mod, norm1, norm2, w_in, qk_gain, a_sink, b_lambda, b_subln,
           w_branch, w_out, dense_w1, dense_w3, dense_w2, moe_router, moe_w1, moe_w3, moe_w2):
    nbatch, seq, d = x.shape
    nctx_len = ctx.shape[1]
    depth = w_mod.shape[0]
    nl = nbatch * seq
    nc = nbatch * nctx_len
    tm = 512
    tn_proj = 512

    def seg_of_block_for(t):
        return lambda i: jnp.minimum(i // (seq // t), nbatch)

    xs = jnp.concatenate([x.reshape(nl, d), ctx.reshape(nc, d)], axis=0)
    cc = jnp.zeros((8, d), f32).at[:nbatch].set(c).at[nbatch].set(c_ctx)
    mod = _mod_call(cc, w_mod, b_mod)
    cos, sin = _rope_tables(nbatch, seq, nctx_len)
    ones_mat = jnp.ones((HEAD_DIM, HEAD_DIM), bf16)
    scale = HEAD_DIM ** -0.5
    ones_cols = jnp.ones((HEAD_DIM,), f32)
    zero_w = jnp.zeros((d, HEAD_DIM), bf16)

    for l in range(depth):
        with_ctx = l < depth - 1
        lam_init = 0.8 - 0.6 * math.exp(-0.3 * l)
        rows = nl + nc if with_ctx else nl
        modl = mod[l].reshape(8 * 6, 1, d)

        wl = w_in[l]
        o = 0
        parts = {}
        for name, width in (("aq", A_Q), ("ak", A_KV), ("av", A_KV), ("bq", B_QK), ("bk", B_QK),
                            ("bv", B_V), ("cq", C_Q), ("ck", C_KV), ("cv", C_KV), ("g", 3 * d)):
            parts[name] = wl[:, o:o + width].astype(bf16)
            o += width
        w_qk = _head_layout(jnp.concatenate([parts[k] for k in ("cq", "aq", "bq", "bk", "ck", "ak")], axis=1))
        cv = parts["cv"]
        cv_cols = []
        for hd in range(C_KV_HEADS):
            cv_cols += [cv[:, hd * HEAD_DIM:(hd + 1) * HEAD_DIM], zero_w]
        w_v = jnp.concatenate([parts["bv"]] + cv_cols + [parts["av"]], axis=1)
        zeros_bv = jnp.zeros((B_V,), f32)
        zeros_h = jnp.zeros((HEAD_DIM,), f32)
        bias_v = jnp.concatenate([zeros_bv] + [zeros_h, ones_cols] * C_KV_HEADS
                                 + [jnp.zeros((A_KV,), f32)]).reshape(1, V_COLS)
        w_g = parts["g"]
        g = qk_gain[l]
        gain = _head_layout(jnp.concatenate(
            [jnp.tile(g[4] * (scale * LOG2E), C_HEADS), jnp.tile(g[0] * scale, A_HEADS),
             jnp.tile(g[2] * (scale * LOG2E), 2 * B_HEADS), jnp.tile(g[3], 2 * B_HEADS),
             jnp.tile(g[5], C_KV_HEADS), jnp.tile(g[1], A_KV_HEADS)]).reshape(1, QK_COLS))

        h = _norm_call(xs, norm1[l], modl, 1, 0, seg_of_block_for(tm), tm, bf16)
        tp = _row_tile(nl + nc)
        tab_spec = ((tp, HEAD_DIM), lambda i, j: (i, 0))
        qk = _deferred_mm_call(_qk_epilogue, h, _tile_major(w_qk, tn_proj), (gain, cos, sin, ones_mat),
                               [((1, tn_proj), lambda i, j: (0, j)), tab_spec, tab_spec,
                                ((HEAD_DIM, HEAD_DIM), lambda i, j: (0, 0))],
                               nl + nc, tp, "qk_proj_norm_rope")
        tn_v = V_COLS // 2
        v = _proj_call(_bias_proj_kernel, h, w_v, (bias_v,), [pl.BlockSpec((1, tn_v), lambda i, j: (0, j))],
                       nl + nc, tm, tn_v, "v_proj")
        gates = _deferred_mm_call(_gate_epilogue, h, _tile_major(w_g, tn_proj), (), [], rows,
                                  _row_tile(rows), "gate_proj")

        oa, ob, oc = _attention(qk, v, a_sink[l], b_lambda[l], b_subln[l], nbatch=nbatch, seq=seq,
                                ctx=nctx_len, with_ctx=with_ctx, lam_init=lam_init)
        y = _merge_call(oa, ob, oc, jnp.swapaxes(_tile_major(w_branch[l].astype(bf16), tn_proj), 0, 1),
                        gates, tm)
        moe_layer = l % 2 == 1
        xn, h2 = _out_proj_call(y, w_out[l].astype(bf16), xs, modl, norm2[l], seg_of_block_for(256), 256,
                                f32 if moe_layer else bf16)
        i = l // 2
        if not moe_layer:
            xs = _dense_ffn_call(h2, _tile_major(dense_w1[i].astype(bf16), tn_proj),
                                 _tile_major(dense_w3[i].astype(bf16), tn_proj),
                                 dense_w2[i].astype(bf16), xn, modl, seg_of_block_for(tm), tm)
        else:
            xs = _moe_call(h2, xn, modl, moe_router[i], moe_w1[i], moe_w3[i],
                           moe_w2[i].astype(bf16), seg_of_block_for(256))
    return xs[:nl].reshape(nbatch, seq, d)
```

```python
import functools
import math

import jax
import jax.numpy as jnp
from jax import lax
from jax.experimental import pallas as pl
from jax.experimental.pallas import tpu as pltpu

f32 = jnp.float32
bf16 = jnp.bfloat16

HEAD_DIM = 128
GRID_W = 64
ROPE_PAIRS = HEAD_DIM // 4
ROPE_THETA = 10000.0
EPS = 1e-6
NEG_INF = -1e30
WINDOW = 128
A_HEADS, A_KV_HEADS = 8, 2
B_HEADS = 4
C_HEADS, C_KV_HEADS = 8, 2
GQA_GROUP = 4
TOP_K = 2
LANES = 128
MXU_WIDTH = 256
MIB = 1024 * 1024
LOG2E = math.log2(math.e)

A_Q, A_KV = A_HEADS * HEAD_DIM, A_KV_HEADS * HEAD_DIM
B_QK, B_V = B_HEADS * 2 * HEAD_DIM, B_HEADS * 2 * HEAD_DIM
C_Q, C_KV = C_HEADS * HEAD_DIM, C_KV_HEADS * HEAD_DIM
QK_CQ, QK_AQ, QK_BQ, QK_BK = 0, C_Q, C_Q + A_Q, C_Q + A_Q + B_QK
QK_CK = QK_BK + B_QK
QK_AK = QK_CK + C_KV
QK_COLS = QK_AK + A_KV
V_BV, V_CV = 0, B_V
V_AV = V_CV + C_KV_HEADS * 2 * HEAD_DIM
V_COLS = V_AV + A_KV


def _cparams(sem, vmem_mib):
    return pltpu.CompilerParams(dimension_semantics=sem, vmem_limit_bytes=vmem_mib * MIB)


def _dot(a, b):
    return jnp.dot(a, b, preferred_element_type=f32)


def _dot_nt(a, b):
    return lax.dot_general(a, b, (((1,), (1,)), ((), ())), preferred_element_type=f32)


def _sigmoid(x):
    return 0.5 * jnp.tanh(0.5 * x) + 0.5


def _rowmax(s):
    return jnp.max(s, axis=-1, keepdims=True)


def _rowsum(s):
    return jnp.sum(s, axis=-1, keepdims=True)


def _lane_tile(x, n):
    return jnp.concatenate([x] * n, axis=1)


def _mod_kernel(c_ref, w_ref, b_ref, o_ref):
    c = c_ref[...]
    a = (c * _sigmoid(c)).astype(bf16)
    o_ref[0] = _dot(a, w_ref[0].astype(bf16)) + b_ref[0]


def _mod_call(cc, w_mod, b_mod):
    depth, d, n = w_mod.shape
    tn = 1024
    return pl.pallas_call(
        _mod_kernel,
        out_shape=jax.ShapeDtypeStruct((depth, 8, n), f32),
        grid=(depth, n // tn),
        in_specs=[pl.BlockSpec((8, d), lambda l, j: (0, 0)),
                  pl.BlockSpec((1, d, tn), lambda l, j: (l, 0, j)),
                  pl.BlockSpec((1, 1, tn), lambda l, j: (l, 0, j))],
        out_specs=pl.BlockSpec((1, 8, tn), lambda l, j: (l, 0, j)),
        compiler_params=_cparams(("arbitrary", "arbitrary"), 40),
        name="mod_vectors",
    )(cc, w_mod, b_mod.reshape(depth, 1, n))


def _norm_kernel(x_ref, nw_ref, sc_ref, sh_ref, o_ref):
    x = x_ref[...]
    y = x * lax.rsqrt(jnp.mean(x * x, axis=-1, keepdims=True) + EPS) * nw_ref[...]
    o_ref[...] = (y * (1.0 + sc_ref[0]) + sh_ref[0]).astype(o_ref.dtype)


def _norm_call(x, nw, modl, k_sc, k_sh, seg_of_block, tm, out_dtype):
    r, d = x.shape
    return pl.pallas_call(
        _norm_kernel,
        out_shape=jax.ShapeDtypeStruct((r, d), out_dtype),
        grid=(r // tm,),
        in_specs=[pl.BlockSpec((tm, d), lambda i: (i, 0)),
                  pl.BlockSpec((1, d), lambda i: (0, 0)),
                  pl.BlockSpec((1, 1, d), lambda i: (seg_of_block(i) * 6 + k_sc, 0, 0)),
                  pl.BlockSpec((1, 1, d), lambda i: (seg_of_block(i) * 6 + k_sh, 0, 0))],
        out_specs=pl.BlockSpec((tm, d), lambda i: (i, 0)),
        compiler_params=_cparams(("arbitrary",), 32),
        name="norm_modulate",
    )(x, nw.reshape(1, d), modl, modl)


def _qk_epilogue(acc_ref, o_ref, gain_ref, cos_ref, sin_ref, ones_ref):
    cos, sin = cos_ref[...], sin_ref[...]
    for hd in range(o_ref.shape[1] // HEAD_DIM):
        sl = slice(hd * HEAD_DIM, (hd + 1) * HEAD_DIM)
        t = acc_ref[:, sl]
        ss = _dot((t * t).astype(bf16), ones_ref[...])
        y = t * lax.rsqrt(ss * (1.0 / HEAD_DIM) + EPS) * gain_ref[:, sl]
        o_ref[:, sl] = (y * cos + pltpu.roll(y, HEAD_DIM // 2, axis=1) * sin).astype(o_ref.dtype)


def _gate_epilogue(acc_ref, o_ref):
    o_ref[...] = _sigmoid(acc_ref[...]).astype(o_ref.dtype)


def _deferred_mm_kernel(h_ref, w_ref, *rest, epilogue, n_extra):
    extra, o_ref, acc_ref = rest[:n_extra], rest[n_extra], rest[n_extra + 1]
    t = pl.program_id(0)

    @pl.when(t == 0)
    def _():
        acc_ref[1] = jnp.zeros(acc_ref.shape[1:], f32)

    for slot in (0, 1):
        @pl.when(lax.rem(t, 2) == slot)
        def _(slot=slot):
            epilogue(acc_ref.at[1 - slot], o_ref, *extra)
            acc_ref[slot] = _dot(h_ref[...], w_ref[...])


def _tile_major(w, tn):
    lead, (k, n) = w.shape[:-2], w.shape[-2:]
    return jnp.swapaxes(w.reshape(lead + (k, n // tn, tn)), -3, -2)


def _row_tile(nrows):
    return next(t for t in (1024, 768, 512) if nrows % t == 0)


def _deferred_mm_call(epilogue, h, w, extra, extra_specs, nrows, tm, name):
    k = h.shape[1]
    nj, _, tn = w.shape
    n = nj * tn
    n_tiles = (nrows // tm) * nj

    def cur(t):
        tt = jnp.minimum(t, n_tiles - 1)
        return tt // nj, tt % nj

    def prev(t):
        tp = jnp.maximum(t - 1, 0)
        return tp // nj, tp % nj

    return pl.pallas_call(
        functools.partial(_deferred_mm_kernel, epilogue=epilogue, n_extra=len(extra)),
        out_shape=jax.ShapeDtypeStruct((nrows, n), bf16),
        grid=(n_tiles + 1,),
        in_specs=[pl.BlockSpec((tm, k), lambda t: (cur(t)[0], 0)),
                  pl.BlockSpec((None, k, tn), lambda t: (cur(t)[1], 0, 0))]
                 + [pl.BlockSpec(shape, functools.partial(lambda t, f: f(*prev(t)), f=f))
                    for shape, f in extra_specs],
        out_specs=pl.BlockSpec((tm, tn), lambda t: prev(t)),
        scratch_shapes=[pltpu.VMEM((2, tm, tn), f32)],
        compiler_params=_cparams(("arbitrary",), 40),
        name=name,
    )(h, w, *extra)


def _bias_proj_kernel(h_ref, w_ref, b_ref, o_ref):
    o_ref[...] = (_dot(h_ref[...], w_ref[...]) + b_ref[...]).astype(o_ref.dtype)


def _proj_call(kernel, h, w, extra, extra_specs, nrows, tm, tn, name):
    k = h.shape[1]
    n = w.shape[1]
    return pl.pallas_call(
        kernel,
        out_shape=jax.ShapeDtypeStruct((nrows, n), bf16),
        grid=(nrows // tm, n // tn),
        in_specs=[pl.BlockSpec((tm, k), lambda i, j: (i, 0)),
                  pl.BlockSpec((k, tn), lambda i, j: (0, j))] + extra_specs,
        out_specs=pl.BlockSpec((tm, tn), lambda i, j: (i, j)),
        compiler_params=_cparams(("arbitrary", "arbitrary"), 40),
        name=name,
    )(h, w, *extra)


def _stack_heads(q, n):
    return jnp.concatenate([q[:, g * HEAD_DIM:(g + 1) * HEAD_DIM] for g in range(n)], axis=0)


def _unstack_heads(o, n):
    t = o.shape[0] // n
    return jnp.concatenate([o[g * t:(g + 1) * t] for g in range(n)], axis=1)


def _softmax_pv(s, v, ex, extra_logit=None):
    m = _rowmax(s)
    if extra_logit is not None:
        m = jnp.maximum(m, extra_logit)
    p = ex(s - m)
    l = _rowsum(p)
    if extra_logit is not None:
        l = l + ex(extra_logit - m)
    return _dot(p.astype(bf16), v) / l


def _sink_column(sink_ref, kh, tq):
    return jnp.concatenate(
        [jnp.full((tq, 1), sink_ref[kh * GQA_GROUP + g], f32) for g in range(GQA_GROUP)], axis=0)


def _diff_finish(on0, on1, lam_ref, g_ref, lam_init):
    bl = lam_ref[...]
    lam = (jnp.exp(_rowsum(bl[0:1] * bl[1:2])) - jnp.exp(_rowsum(bl[2:3] * bl[3:4])) + lam_init)
    o = on0 - lam * on1
    y = o * lax.rsqrt(jnp.mean(o * o, axis=-1, keepdims=True) + EPS) * g_ref[...]
    return y * (1.0 - lam_init)


def _attn_a_lat_kernel(sink_ref, q_ref, kl_ref, kc_ref, vl_ref, vc_ref, rel_ref, o_ref, *, tq, nsub, seq):
    kh = pl.program_id(1)
    qi = pl.program_id(2)
    band = tq + 2 * WINDOW
    sink = _sink_column(sink_ref, kh, tq)
    scores, probs = [], []
    for sub in range(nsub):
        qs = _stack_heads(q_ref[sub * tq:(sub + 1) * tq, :], GQA_GROUP)
        q0 = (qi * nsub + sub) * tq
        start = pl.multiple_of(jnp.clip(q0 - WINDOW, 0, seq - band), WINDOW)
        s_ctx = _dot_nt(qs, kc_ref[...])
        s_loc = _dot_nt(qs, kl_ref[pl.ds(start, band), :])
        s_loc = s_loc + rel_ref[(q0 - start) // WINDOW]
        scores.append((s_ctx, s_loc, start))
    for s_ctx, s_loc, start in scores:
        m = jnp.maximum(jnp.maximum(_rowmax(s_ctx), _rowmax(s_loc)), sink)
        p_ctx = jnp.exp2(s_ctx - m)
        p_loc = jnp.exp2(s_loc - m)
        l = _rowsum(p_ctx) + _rowsum(p_loc) + jnp.exp2(sink - m)
        probs.append((p_ctx.astype(bf16), p_loc.astype(bf16), l, start))
    for sub, (p_ctx, p_loc, l, start) in enumerate(probs):
        o = (_dot(p_ctx, vc_ref[...]) + _dot(p_loc, vl_ref[pl.ds(start, band), :])) / l
        o_ref[sub * tq:(sub + 1) * tq, :] = _unstack_heads(o, GQA_GROUP).astype(o_ref.dtype)


def _attn_a_ctx_kernel(sink_ref, q_ref, kc_ref, vc_ref, o_ref, *, tq):
    qs = _stack_heads(q_ref[...], GQA_GROUP)
    o = _softmax_pv(_dot_nt(qs, kc_ref[...]), vc_ref[...], jnp.exp2,
                    _sink_column(sink_ref, pl.program_id(1), tq))
    o_ref[...] = _unstack_heads(o, GQA_GROUP).astype(o_ref.dtype)


def _attn_c_lat_kernel(q_ref, kl_ref, kc_ref, vl_ref, vc_ref, o_ref, m_ref, acc_ref, *, tk, n_chunks):
    qs = _stack_heads(q_ref[...], GQA_GROUP)
    s = _dot_nt(qs, kc_ref[...])
    m0 = _rowmax(s)
    m_ref[...] = jnp.broadcast_to(m0, m_ref.shape)
    acc_ref[...] = _dot(jnp.exp2(s - m0).astype(bf16), vc_ref[...])

    def chunk(c, carry):
        off = pl.multiple_of(c * tk, tk)
        s = _dot_nt(qs, kl_ref[pl.ds(off, tk), :])
        m_old = m_ref[...]
        m_new = jnp.maximum(m_old, _rowmax(s))
        alpha = jnp.exp2(m_old - m_new)
        p = jnp.exp2(s - _lane_tile(m_new, tk // LANES))
        acc_ref[...] = (_lane_tile(alpha, 2) * acc_ref[...]
                        + _dot(p.astype(bf16), vl_ref[pl.ds(off, tk), :]))
        m_ref[...] = m_new
        return carry

    lax.fori_loop(0, n_chunks, chunk, 0, unroll=min(n_chunks, 8 * 512 // tk))
    acc = acc_ref[...]
    o = acc[:, :HEAD_DIM] / acc[:, HEAD_DIM:]
    o_ref[...] = _unstack_heads(o, GQA_GROUP).astype(o_ref.dtype)


def _attn_c_ctx_kernel(q_ref, kc_ref, vc_ref, o_ref):
    qs = _stack_heads(q_ref[...], GQA_GROUP)
    o = _softmax_pv(_dot_nt(qs, kc_ref[...]), vc_ref[...], jnp.exp2)
    o_ref[...] = _unstack_heads(o, GQA_GROUP).astype(o_ref.dtype)


def _diff_scores(q, k):
    return jnp.concatenate([_dot_nt(q[:, :HEAD_DIM], k[:, :HEAD_DIM]),
                            _dot_nt(q[:, HEAD_DIM:], k[:, HEAD_DIM:])], axis=0)


def _attn_b_lat_kernel(q_ref, kl_ref, kc_ref, vl_ref, vc_ref, lam_ref, g_ref, o_ref,
                       m_ref, l_ref, acc_ref, *, tk, n_chunks, lam_init):
    q = q_ref[...]
    tq = q.shape[0]
    s = _diff_scores(q, kc_ref[...])
    m0 = _rowmax(s)
    p = jnp.exp2(s - m0)
    m_ref[...] = jnp.broadcast_to(m0, m_ref.shape)
    l_ref[...] = jnp.broadcast_to(_rowsum(p), l_ref.shape)
    acc_ref[...] = _dot(p.astype(bf16), vc_ref[...])

    def chunk(c, carry):
        off = pl.multiple_of(c * tk, tk)
        s = _diff_scores(q, kl_ref[pl.ds(off, tk), :])
        m_old = m_ref[...]
        m_new = jnp.maximum(m_old, _rowmax(s))
        alpha = jnp.exp2(m_old - m_new)
        p = jnp.exp2(s - _lane_tile(m_new, tk // LANES))
        l_ref[...] = alpha * l_ref[...] + _rowsum(p)
        acc_ref[...] = (_lane_tile(alpha, 2) * acc_ref[...]
                        + _dot(p.astype(bf16), vl_ref[pl.ds(off, tk), :]))
        m_ref[...] = m_new
        return carry

    lax.fori_loop(0, n_chunks, chunk, 0, unroll=min(n_chunks, 8 * 512 // tk))
    on = acc_ref[...] / _lane_tile(l_ref[...], 2)
    o_ref[...] = _diff_finish(on[:tq], on[tq:], lam_ref, g_ref, lam_init).astype(o_ref.dtype)


def _attn_b_ctx_kernel(q_ref, kc_ref, vc_ref, lam_ref, g_ref, o_ref, *, lam_init):
    q = q_ref[...]
    tq = q.shape[0]
    on = _softmax_pv(_diff_scores(q, kc_ref[...]), vc_ref[...], jnp.exp2)
    o_ref[...] = _diff_finish(on[:tq], on[tq:], lam_ref, g_ref, lam_init).astype(o_ref.dtype)


def _attention(qk, v, a_sink, b_lam, b_subln, *, nbatch, seq, ctx, with_ctx, lam_init):
    tq_win = 256
    tq_gqa = 256
    tq_diff = min(512, seq)
    tq_ctx = ctx
    tk = min(512, seq)
    nl_rows = nbatch * seq
    nc_rows = nbatch * ctx
    cb = nl_rows // ctx
    gw = GQA_GROUP * HEAD_DIM
    bw = 2 * HEAD_DIM
    sem = ("arbitrary", "arbitrary", "arbitrary")

    def lat_q(tq, width, off):
        return pl.BlockSpec((tq, width), lambda b, h, qi, *_: (b * (seq // tq) + qi, off // width + h))

    def ctx_q(width, off):
        return pl.BlockSpec((tq_ctx, width), lambda b, h, qi, *_: (cb + b, off // width + h))

    def lat_kv(width, off):
        return pl.BlockSpec((seq, width), lambda b, h, qi, *_: (b, off // width + h))

    def ctx_kv(width, off):
        return pl.BlockSpec((ctx, width), lambda b, h, qi, *_: (cb + b, off // width + h))

    def ctx_out(width):
        return pl.BlockSpec((tq_ctx, width), lambda b, h, qi, *_: (b, h))

    nsub = next(n for n in (4, 2, 1) if seq % (n * tq_win) == 0)
    band = tq_win + 2 * WINDOW
    m_win = GQA_GROUP * tq_win
    dist = ((jnp.arange(m_win, dtype=jnp.int32) % tq_win)[None, :, None]
            - jnp.arange(band, dtype=jnp.int32)[None, None, :]
            + WINDOW * jnp.arange(3, dtype=jnp.int32)[:, None, None])
    rel = jnp.where(jnp.abs(dist) <= WINDOW, 0.0, NEG_INF).astype(f32)
    a_sink = a_sink * LOG2E
    oa = pl.pallas_call(
        functools.partial(_attn_a_lat_kernel, tq=tq_win, nsub=nsub, seq=seq),
        out_shape=jax.ShapeDtypeStruct((nl_rows, A_Q), bf16),
        grid_spec=pltpu.PrefetchScalarGridSpec(
            num_scalar_prefetch=1,
            grid=(nbatch, A_KV_HEADS, seq // (nsub * tq_win)),
            in_specs=[lat_q(nsub * tq_win, gw, QK_AQ), lat_kv(HEAD_DIM, QK_AK), ctx_kv(HEAD_DIM, QK_AK),
                      lat_kv(HEAD_DIM, V_AV), ctx_kv(HEAD_DIM, V_AV),
                      pl.BlockSpec((3, m_win, band), lambda b, h, qi, *_: (0, 0, 0))],
            out_specs=lat_q(nsub * tq_win, gw, 0),
        ),
        compiler_params=_cparams(sem, 48),
        name="attn_window",
    )(a_sink, qk, qk, qk, v, v, rel)
    if with_ctx:
        oa_ctx = pl.pallas_call(
            functools.partial(_attn_a_ctx_kernel, tq=tq_ctx),
            out_shape=jax.ShapeDtypeStruct((nc_rows, A_Q), bf16),
            grid_spec=pltpu.PrefetchScalarGridSpec(
                num_scalar_prefetch=1,
                grid=(nbatch, A_KV_HEADS, 1),
                in_specs=[ctx_q(gw, QK_AQ), ctx_kv(HEAD_DIM, QK_AK), ctx_kv(HEAD_DIM, V_AV)],
                out_specs=ctx_out(gw),
            ),
            compiler_params=_cparams(sem, 32),
            name="attn_window_ctx",
        )(a_sink, qk, qk, v)
        oa = jnp.concatenate([oa, oa_ctx], axis=0)

    m_rows = GQA_GROUP * tq_gqa
    oc = pl.pallas_call(
        functools.partial(_attn_c_lat_kernel, tk=tk, n_chunks=seq // tk),
        out_shape=jax.ShapeDtypeStruct((nl_rows, C_Q), bf16),
        grid=(nbatch, C_KV_HEADS, seq // tq_gqa),
        in_specs=[lat_q(tq_gqa, gw, QK_CQ), lat_kv(HEAD_DIM, QK_CK), ctx_kv(HEAD_DIM, QK_CK),
                  lat_kv(2 * HEAD_DIM, V_CV), ctx_kv(2 * HEAD_DIM, V_CV)],
        out_specs=lat_q(tq_gqa, gw, 0),
        scratch_shapes=[pltpu.VMEM((m_rows, LANES), f32), pltpu.VMEM((m_rows, 2 * HEAD_DIM), f32)],
        compiler_params=_cparams(sem, 56),
        name="attn_global",
    )(qk, qk, qk, v, v)
    if with_ctx:
        oc_ctx = pl.pallas_call(
            _attn_c_ctx_kernel,
            out_shape=jax.ShapeDtypeStruct((nc_rows, C_Q), bf16),
            grid=(nbatch, C_KV_HEADS, 1),
            in_specs=[ctx_q(gw, QK_CQ), ctx_kv(HEAD_DIM, QK_CK),
                      pl.BlockSpec((ctx, HEAD_DIM), lambda b, h, qi: (cb + b, V_CV // HEAD_DIM + 2 * h))],
            out_specs=ctx_out(gw),
            compiler_params=_cparams(sem, 32),
            name="attn_global_ctx",
        )(qk, qk, v)
        oc = jnp.concatenate([oc, oc_ctx], axis=0)

    lam_spec = pl.BlockSpec((4, HEAD_DIM), lambda b, h, qi: (0, 0))
    g_spec = pl.BlockSpec((1, bw), lambda b, h, qi: (0, 0))
    g2d = b_subln.reshape(1, bw)
    ob = pl.pallas_call(
        functools.partial(_attn_b_lat_kernel, tk=tk, n_chunks=seq // tk, lam_init=lam_init),
        out_shape=jax.ShapeDtypeStruct((nl_rows, B_V), bf16),
        grid=(nbatch, B_HEADS, seq // tq_diff),
        in_specs=[lat_q(tq_diff, bw, QK_BQ), lat_kv(bw, QK_BK), ctx_kv(bw, QK_BK),
                  lat_kv(bw, V_BV), ctx_kv(bw, V_BV), lam_spec, g_spec],
        out_specs=lat_q(tq_diff, bw, 0),
        scratch_shapes=[pltpu.VMEM((2 * tq_diff, LANES), f32), pltpu.VMEM((2 * tq_diff, LANES), f32),
                        pltpu.VMEM((2 * tq_diff, bw), f32)],
        compiler_params=_cparams(sem, 56),
        name="attn_diff",
    )(qk, qk, qk, v, v, b_lam, g2d)
    if with_ctx:
        ob_ctx = pl.pallas_call(
            functools.partial(_attn_b_ctx_kernel, lam_init=lam_init),
            out_shape=jax.ShapeDtypeStruct((nc_rows, B_V), bf16),
            grid=(nbatch, B_HEADS, 1),
            in_specs=[ctx_q(bw, QK_BQ), ctx_kv(bw, QK_BK), ctx_kv(bw, V_BV), lam_spec, g_spec],
            out_specs=ctx_out(bw),
            compiler_params=_cparams(sem, 32),
            name="attn_diff_ctx",
        )(qk, qk, v, b_lam, g2d)
        ob = jnp.concatenate([ob, ob_ctx], axis=0)
    return oa, ob, oc


def _merge_kernel(oa_ref, ob_ref, oc_ref, w_ref, ga_ref, gb_ref, gc_ref, y_ref):
    y = (ga_ref[...].astype(f32) * _dot(oa_ref[...], w_ref[0])
         + gb_ref[...].astype(f32) * _dot(ob_ref[...], w_ref[1])
         + gc_ref[...].astype(f32) * _dot(oc_ref[...], w_ref[2]))
    y_ref[...] = y.astype(y_ref.dtype)


def _merge_call(oa, ob, oc, wb, gates, tm):
    rows, bwid = oa.shape
    nj, _, _, tn = wb.shape
    d = nj * tn
    o_spec = pl.BlockSpec((tm, bwid), lambda i, j: (i, 0))
    return pl.pallas_call(
        _merge_kernel,
        out_shape=jax.ShapeDtypeStruct((rows, d), bf16),
        grid=(rows // tm, nj),
        in_specs=[o_spec, o_spec, o_spec,
                  pl.BlockSpec((None, 3, bwid, tn), lambda i, j: (j, 0, 0, 0)),
                  pl.BlockSpec((tm, tn), lambda i, j: (i, j)),
                  pl.BlockSpec((tm, tn), lambda i, j: (i, nj + j)),
                  pl.BlockSpec((tm, tn), lambda i, j: (i, 2 * nj + j))],
        out_specs=pl.BlockSpec((tm, tn), lambda i, j: (i, j)),
        compiler_params=_cparams(("arbitrary", "arbitrary"), 40),
        name="branch_merge",
    )(oa, ob, oc, wb, gates, gates, gates)


def _out_proj_kernel(y_ref, w_ref, x_ref, g1_ref, nw_ref, sc_ref, sh_ref, xo_ref, h_ref):
    xn = x_ref[...] + g1_ref[0] * _dot(y_ref[...], w_ref[...])
    xo_ref[...] = xn
    t = xn * lax.rsqrt(jnp.mean(xn * xn, axis=-1, keepdims=True) + EPS) * nw_ref[...]
    h_ref[...] = (t * (1.0 + sc_ref[0]) + sh_ref[0]).astype(h_ref.dtype)


def _out_proj_call(y, w_out, x, modl, nw2, seg_of_block, tm, h_dtype):
    rows, d = y.shape

    def mod_spec(k):
        return pl.BlockSpec((1, 1, d), lambda i: (seg_of_block(i) * 6 + k, 0, 0))

    row_spec = pl.BlockSpec((tm, d), lambda i: (i, 0))
    return pl.pallas_call(
        _out_proj_kernel,
        out_shape=(jax.ShapeDtypeStruct((rows, d), f32), jax.ShapeDtypeStruct((rows, d), h_dtype)),
        grid=(rows // tm,),
        in_specs=[row_spec, pl.BlockSpec((d, d), lambda i: (0, 0)), row_spec,
                  mod_spec(2), pl.BlockSpec((1, d), lambda i: (0, 0)), mod_spec(4), mod_spec(3)],
        out_specs=(row_spec, row_spec),
        compiler_params=_cparams(("arbitrary",), 48),
        name="out_proj_residual_norm",
    )(y, w_out, x, modl, nw2.reshape(1, d), modl, modl)


def _swiglu_stream_step(t, h_ref, w1_ref, w3_ref, w2_ref, acc_ref, a_ref, b_ref):
    for slot in (0, 1):
        @pl.when(lax.rem(t, 2) == slot)
        def _(slot=slot):
            h = h_ref[...]
            a_ref[slot] = _dot(h, w1_ref[...].astype(bf16))
            a = a_ref[1 - slot]
            b = b_ref[1 - slot]
            g = (a * _sigmoid(a) * b).astype(bf16)
            acc_ref[...] += _dot(g, w2_ref[...])
            b_ref[slot] = _dot(h, w3_ref[...].astype(bf16))


def _dense_ffn_kernel(h_ref, w1_ref, w3_ref, w2_ref, x_ref, g2_ref, o_ref, acc_ref, a_ref, b_ref, *, nf):
    t = pl.program_id(0)
    jp = lax.rem(jnp.maximum(t - 1, 0), nf)

    @pl.when(t == 0)
    def _():
        a_ref[1] = jnp.zeros(a_ref.shape[1:], f32)
        b_ref[1] = jnp.zeros(b_ref.shape[1:], f32)

    @pl.when(jp == 0)
    def _():
        acc_ref[...] = jnp.zeros_like(acc_ref)

    _swiglu_stream_step(t, h_ref, w1_ref, w3_ref, w2_ref, acc_ref, a_ref, b_ref)

    @pl.when(jnp.logical_and(jp == nf - 1, t > 0))
    def _():
        o_ref[...] = x_ref[...] + g2_ref[0] * acc_ref[...]


def _dense_ffn_call(h2, w1, w3, w2, x, modl, seg_of_block, tm):
    rows, d = h2.shape
    nf, _, tf = w1.shape
    n_pairs = (rows // tm) * nf

    def cur(t):
        tt = jnp.minimum(t, n_pairs - 1)
        return tt // nf, tt % nf

    def prev(t):
        tp = jnp.maximum(t - 1, 0)
        return tp // nf, tp % nf

    prev_row = pl.BlockSpec((tm, d), lambda t: (prev(t)[0], 0))
    return pl.pallas_call(
        functools.partial(_dense_ffn_kernel, nf=nf),
        out_shape=jax.ShapeDtypeStruct((rows, d), f32),
        grid=(n_pairs + 1,),
        in_specs=[pl.BlockSpec((tm, d), lambda t: (cur(t)[0], 0)),
                  pl.BlockSpec((None, d, tf), lambda t: (cur(t)[1], 0, 0)),
                  pl.BlockSpec((None, d, tf), lambda t: (cur(t)[1], 0, 0)),
                  pl.BlockSpec((tf, d), lambda t: (prev(t)[1], 0)),
                  prev_row,
                  pl.BlockSpec((1, 1, d), lambda t: (seg_of_block(prev(t)[0]) * 6 + 5, 0, 0))],
        out_specs=prev_row,
        scratch_shapes=[pltpu.VMEM((tm, d), f32), pltpu.VMEM((2, tm, tf), f32), pltpu.VMEM((2, tm, tf), f32)],
        compiler_params=_cparams(("arbitrary",), 56),
        name="dense_swiglu",
    )(h2, w1, w3, w2, x, modl)


def _router_kernel(h_ref, w_ref, idx_ref, wt_ref, *, n_experts):
    logits = jnp.dot(h_ref[...], w_ref[...], preferred_element_type=f32,
                     precision=lax.Precision.HIGHEST)
    lane = lax.broadcasted_iota(jnp.int32, logits.shape, 1)
    lg = jnp.where(lane < n_experts, logits, -jnp.inf)
    m1 = _rowmax(lg)
    i1 = jnp.min(jnp.where(lg == m1, lane, LANES), axis=-1, keepdims=True)
    lg2 = jnp.where(lane == i1, -jnp.inf, lg)
    m2 = _rowmax(lg2)
    i2 = jnp.min(jnp.where(lg2 == m2, lane, LANES), axis=-1, keepdims=True)
    e2 = jnp.exp(m2 - m1)
    w1 = 1.0 / (1.0 + e2)
    w2 = e2 / (1.0 + e2)
    idx_ref[...] = jnp.where(lane == 0, i1, jnp.where(lane == 1, i2, 0))
    wt_ref[...] = jnp.where(lane == 0, w1, jnp.where(lane == 1, w2, 0.0))


def _router_call(h2, w_router, tm):
    n, d = h2.shape
    e = w_router.shape[1]
    wp = jnp.zeros((d, LANES), f32).at[:, :e].set(w_router)
    return pl.pallas_call(
        functools.partial(_router_kernel, n_experts=e),
        out_shape=(jax.ShapeDtypeStruct((n, LANES), jnp.int32), jax.ShapeDtypeStruct((n, LANES), f32)),
        grid=(n // tm,),
        in_specs=[pl.BlockSpec((tm, d), lambda i: (i, 0)), pl.BlockSpec((d, LANES), lambda i: (0, 0))],
        out_specs=(pl.BlockSpec((tm, LANES), lambda i: (i, 0)), pl.BlockSpec((tm, LANES), lambda i: (i, 0))),
        compiler_params=_cparams(("arbitrary",), 32),
        name="moe_router_top2",
    )(h2, wp)


def _gather_rows_kernel(src_ref, x_hbm, o_ref, buf, sem, *, chunk):
    i = pl.program_id(0)
    slot = lax.rem(i, 2)

    def row_copy(step, sl, r):
        return pltpu.make_async_copy(x_hbm.at[pl.ds(src_ref[step * chunk + r], 1)],
                                     buf.at[sl, pl.ds(r, 1)], sem.at[sl])

    def issue(step, sl):
        def body(r, c):
            row_copy(step, sl, r).start()
            return c
        lax.fori_loop(0, chunk, body, 0, unroll=8)

    def drain(step, sl):
        def body(r, c):
            row_copy(step, sl, r).wait()
            return c
        lax.fori_loop(0, chunk, body, 0, unroll=8)

    @pl.when(i == 0)
    def _():
        issue(0, 0)

    @pl.when(i + 1 < pl.num_programs(0))
    def _():
        issue(i + 1, 1 - slot)

    drain(i, slot)
    o_ref[...] = buf[slot].astype(o_ref.dtype)


def _gather_rows_call(x, src, chunk):
    cap = src.shape[0]
    d = x.shape[1]
    return pl.pallas_call(
        functools.partial(_gather_rows_kernel, chunk=chunk),
        out_shape=jax.ShapeDtypeStruct((cap, d), bf16),
        grid_spec=pltpu.PrefetchScalarGridSpec(
            num_scalar_prefetch=1,
            grid=(cap // chunk,),
            in_specs=[pl.BlockSpec(memory_space=pl.ANY)],
            out_specs=pl.BlockSpec((chunk, d), lambda i, s: (i, 0)),
            scratch_shapes=[pltpu.VMEM((2, chunk, d), x.dtype), pltpu.SemaphoreType.DMA((2,))],
        ),
        compiler_params=_cparams(("arbitrary",), 32),
        name="moe_gather_rows",
    )(src, x)


def _expert_kernel(be_ref, nu_ref, x_ref, w1_ref, w3_ref, w2_ref, o_ref, acc_ref, a_ref, b_ref, *, nf):
    t = pl.program_id(0)
    tp = jnp.maximum(t - 1, 0)
    ip = tp // nf
    jp = lax.rem(tp, nf)
    n_live = nu_ref[0] * nf
    live = t <= n_live

    @pl.when(t == 0)
    def _():
        a_ref[1] = jnp.zeros(a_ref.shape[1:], f32)
        b_ref[1] = jnp.zeros(b_ref.shape[1:], f32)

    @pl.when(jnp.logical_and(jp == 0, live))
    def _():
        acc_ref[...] = jnp.zeros_like(acc_ref)

    @pl.when(live)
    def _():
        _swiglu_stream_step(t, x_ref, w1_ref.at[0], w3_ref.at[0], w2_ref.at[0], acc_ref, a_ref, b_ref)

    last = jnp.logical_and(jp == nf - 1, t > 0)

    @pl.when(jnp.logical_and(last, ip < nu_ref[0]))
    def _():
        o_ref[...] = acc_ref[...]

    @pl.when(jnp.logical_and(last, ip >= nu_ref[0]))
    def _():
        o_ref[...] = jnp.zeros_like(o_ref)


def _expert_call(xs, blk_e, n_used, w1, w3, w2, tm, tf):
    cap, d = xs.shape
    nf = w1.shape[2] // tf
    n_pairs = (cap // tm) * nf

    def pair(tt, nu):
        i, j = tt // nf, tt % nf
        return jnp.minimum(i, nu[0] - 1), jnp.where(i < nu[0], j, nf - 1)

    def cur(t, nu):
        return pair(jnp.minimum(t, n_pairs - 1), nu)

    def prev(t, nu):
        return pair(jnp.maximum(t - 1, 0), nu)

    def up_spec():
        return pl.BlockSpec((1, d, tf), lambda t, be, nu: (be[cur(t, nu)[0]], 0, cur(t, nu)[1]))

    return pl.pallas_call(
        functools.partial(_expert_kernel, nf=nf),
        out_shape=jax.ShapeDtypeStruct((cap, d), f32),
        grid_spec=pltpu.PrefetchScalarGridSpec(
            num_scalar_prefetch=2,
            grid=(n_pairs + 1,),
            in_specs=[pl.BlockSpec((tm, d), lambda t, be, nu: (cur(t, nu)[0], 0)),
                      up_spec(), up_spec(),
                      pl.BlockSpec((1, tf, d), lambda t, be, nu: (be[prev(t, nu)[0]], prev(t, nu)[1], 0))],
            out_specs=pl.BlockSpec((tm, d), lambda t, be, nu: (jnp.maximum(t - 1, 0) // nf, 0)),
            scratch_shapes=[pltpu.VMEM((tm, d), f32), pltpu.VMEM((2, tm, tf), f32),
                            pltpu.VMEM((2, tm, tf), f32)],
        ),
        compiler_params=_cparams(("arbitrary",), 56),
        name="moe_expert_swiglu",
    )(blk_e, n_used, xs, w1, w3, w2)


def _combine_kernel(dest_ref, y_hbm, x_ref, g2_ref, wt_ref, o_ref, buf0, buf1, sem, *, tm):
    i = pl.program_id(0)
    slot = lax.rem(i, 2)

    def copies(step, sl, r):
        s = (step * tm + r) * TOP_K
        return (pltpu.make_async_copy(y_hbm.at[pl.ds(dest_ref[s], 1)], buf0.at[sl, pl.ds(r, 1)], sem.at[sl]),
                pltpu.make_async_copy(y_hbm.at[pl.ds(dest_ref[s + 1], 1)], buf1.at[sl, pl.ds(r, 1)],
                                      sem.at[sl]))

    def issue(step, sl):
        def body(r, c):
            c0, c1 = copies(step, sl, r)
            c0.start()
            c1.start()
            return c
        lax.fori_loop(0, tm, body, 0, unroll=8)

    def drain(step, sl):
        def body(r, c):
            c0, c1 = copies(step, sl, r)
            c0.wait()
            c1.wait()
            return c
        lax.fori_loop(0, tm, body, 0, unroll=8)

    @pl.when(i == 0)
    def _():
        issue(0, 0)

    @pl.when(i + 1 < pl.num_programs(0))
    def _():
        issue(i + 1, 1 - slot)

    drain(i, slot)
    wt = wt_ref[...]
    mix = wt[:, 0:1] * buf0[slot] + wt[:, 1:2] * buf1[slot]
    o_ref[...] = x_ref[...] + g2_ref[0] * mix


def _combine_call(yb, dest, x, modl, wts, seg_of_block, tm):
    n, d = x.shape
    return pl.pallas_call(
        functools.partial(_combine_kernel, tm=tm),
        out_shape=jax.ShapeDtypeStruct((n, d), f32),
        grid_spec=pltpu.PrefetchScalarGridSpec(
            num_scalar_prefetch=1,
            grid=(n // tm,),
            in_specs=[pl.BlockSpec(memory_space=pl.ANY),
                      pl.BlockSpec((tm, d), lambda i, dst: (i, 0)),
                      pl.BlockSpec((1, 1, d), lambda i, dst: (seg_of_block(i) * 6 + 5, 0, 0)),
                      pl.BlockSpec((tm, LANES), lambda i, dst: (i, 0))],
            out_specs=pl.BlockSpec((tm, d), lambda i, dst: (i, 0)),
            scratch_shapes=[pltpu.VMEM((2, tm, d), f32), pltpu.VMEM((2, tm, d), f32),
                            pltpu.SemaphoreType.DMA((2,))],
        ),
        compiler_params=_cparams(("arbitrary",), 32),
        name="moe_combine_residual",
    )(dest, yb, x, modl, wts)


def _moe_call(h2, x, modl, w_router, w1, w3, w2, seg_of_block):
    n, d = h2.shape
    n_exp = w_router.shape[1]
    tm = 512
    idx, wts = _router_call(h2, w_router, 256)
    flat_e = idx[:, :TOP_K].reshape(-1)
    n_slots = n * TOP_K
    onehot = (flat_e[:, None] == jnp.arange(n_exp, dtype=jnp.int32)[None, :]).astype(jnp.int32)
    csum = jnp.cumsum(onehot, axis=0)
    counts = csum[-1]
    rank = jnp.sum((csum - 1) * onehot, axis=1)
    padded = (counts + tm - 1) // tm * tm
    pend = jnp.cumsum(padded)
    pstart = pend - padded
    dest = (pstart[flat_e] + rank).astype(jnp.int32)
    cap = (n_slots // tm + n_exp) * tm
    src = jnp.zeros((cap,), jnp.int32).at[dest].set(jnp.arange(n_slots, dtype=jnp.int32) // TOP_K)
    n_blocks = cap // tm
    blk_start = jnp.arange(n_blocks, dtype=jnp.int32) * tm
    blk_e = jnp.minimum(jnp.searchsorted(pend, blk_start, side='right'), n_exp - 1).astype(jnp.int32)
    n_used = (pend[-1:] // tm).astype(jnp.int32)

    xs = _gather_rows_call(h2, src, 256)
    yb = _expert_call(xs, blk_e, n_used, w1, w3, w2, tm, 512)
    return _combine_call(yb, dest, x, modl, wts, seg_of_block, 256)


def _rope_tables(nbatch, seq, ctx):
    t = jnp.arange(seq, dtype=jnp.int32)
    row = (t // GRID_W).astype(f32)
    col = (t % GRID_W).astype(f32)
    inv = ROPE_THETA ** (-jnp.arange(ROPE_PAIRS, dtype=f32) / ROPE_PAIRS)
    lane = jnp.arange(HEAD_DIM)
    pos = jnp.where(((lane // ROPE_PAIRS) % 2)[None, :] == 0, row[:, None], col[:, None])
    ang = pos * inv[lane % ROPE_PAIRS][None, :]
    cos = jnp.cos(ang)
    sin = jnp.where((lane < HEAD_DIM // 2)[None, :], -jnp.sin(ang), jnp.sin(ang))
    nctx = nbatch * ctx

    def full(tab, fill):
        return jnp.concatenate([jnp.tile(tab, (nbatch, 1)), jnp.full((nctx, HEAD_DIM), fill, f32)], axis=0)

    return full(cos, 1.0), full(sin, 0.0)


def _head_layout(w):
    lead = w.shape[:-1]
    w5 = w.reshape(lead + (w.shape[-1] // HEAD_DIM, 2, 2, ROPE_PAIRS))
    return jnp.swapaxes(w5, -3, -2).reshape(w.shape)


def kernel(x, c, ctx, c_ctx, w_mod, b_mod, norm1, norm2, w_in, qk_gain, a_sink, b_lambda, b_subln,
           w_branch, w_out, dense_w1, dense_w3, dense_w2, moe_router, moe_w1, moe_w3, moe_w2):
    nbatch, seq, d = x.shape
    nctx_len = ctx.shape[1]
    depth = w_mod.shape[0]
    nl = nbatch * seq
    nc = nbatch * nctx_len
    tm = 512
    tn_proj = 512

    def seg_of_block_for(t):
        return lambda i: jnp.minimum(i // (seq // t), nbatch)

    xs = jnp.concatenate([x.reshape(nl, d), ctx.reshape(nc, d)], axis=0)
    cc = jnp.zeros((8, d), f32).at[:nbatch].set(c).at[nbatch].set(c_ctx)
    mod = _mod_call(cc, w_mod, b_mod)
    cos, sin = _rope_tables(nbatch, seq, nctx_len)
    ones_mat = jnp.ones((HEAD_DIM, HEAD_DIM), bf16)
    scale = HEAD_DIM ** -0.5
    ones_cols = jnp.ones((HEAD_DIM,), f32)
    zero_w = jnp.zeros((d, HEAD_DIM), bf16)

    for l in range(depth):
        with_ctx = l < depth - 1
        lam_init = 0.8 - 0.6 * math.exp(-0.3 * l)
        rows = nl + nc if with_ctx else nl
        modl = mod[l].reshape(8 * 6, 1, d)

        wl = w_in[l]
        o = 0
        parts = {}
        for name, width in (("aq", A_Q), ("ak", A_KV), ("av", A_KV), ("bq", B_QK), ("bk", B_QK),
                            ("bv", B_V), ("cq", C_Q), ("ck", C_KV), ("cv", C_KV), ("g", 3 * d)):
            parts[name] = wl[:, o:o + width].astype(bf16)
            o += width
        w_qk = _head_layout(jnp.concatenate([parts[k] for k in ("cq", "aq", "bq", "bk", "ck", "ak")], axis=1))
        cv = parts["cv"]
        cv_cols = []
        for hd in range(C_KV_HEADS):
            cv_cols += [cv[:, hd * HEAD_DIM:(hd + 1) * HEAD_DIM], zero_w]
        w_v = jnp.concatenate([parts["bv"]] + cv_cols + [parts["av"]], axis=1)
        zeros_bv = jnp.zeros((B_V,), f32)
        zeros_h = jnp.zeros((HEAD_DIM,), f32)
        bias_v = jnp.concatenate([zeros_bv] + [zeros_h, ones_cols] * C_KV_HEADS
                                 + [jnp.zeros((A_KV,), f32)]).reshape(1, V_COLS)
        w_g = parts["g"]
        g = qk_gain[l]
        gain = _head_layout(jnp.concatenate(
            [jnp.tile(g[4] * (scale * LOG2E), C_HEADS), jnp.tile(g[0] * (scale * LOG2E), A_HEADS),
             jnp.tile(g[2] * (scale * LOG2E), 2 * B_HEADS), jnp.tile(g[3], 2 * B_HEADS),
             jnp.tile(g[5], C_KV_HEADS), jnp.tile(g[1], A_KV_HEADS)]).reshape(1, QK_COLS))

        h = _norm_call(xs, norm1[l], modl, 1, 0, seg_of_block_for(tm), tm, bf16)
        tp = _row_tile(nl + nc)
        tab_spec = ((tp, HEAD_DIM), lambda i, j: (i, 0))
        qk = _deferred_mm_call(_qk_epilogue, h, _tile_major(w_qk, tn_proj), (gain, cos, sin, ones_mat),
                               [((1, tn_proj), lambda i, j: (0, j)), tab_spec, tab_spec,
                                ((HEAD_DIM, HEAD_DIM), lambda i, j: (0, 0))],
                               nl + nc, tp, "qk_proj_norm_rope")
        tn_v = V_COLS // 2
        v = _proj_call(_bias_proj_kernel, h, w_v, (bias_v,), [pl.BlockSpec((1, tn_v), lambda i, j: (0, j))],
                       nl + nc, tm, tn_v, "v_proj")
        gates = _deferred_mm_call(_gate_epilogue, h, _tile_major(w_g, tn_proj), (), [], rows,
                                  _row_tile(rows), "gate_proj")

        oa, ob, oc = _attention(qk, v, a_sink[l], b_lambda[l], b_subln[l], nbatch=nbatch, seq=seq,
                                ctx=nctx_len, with_ctx=with_ctx, lam_init=lam_init)
        y = _merge_call(oa, ob, oc, jnp.swapaxes(_tile_major(w_branch[l].astype(bf16), tn_proj), 0, 1),
                        gates, tm)
        moe_layer = l % 2 == 1
        xn, h2 = _out_proj_call(y, w_out[l].astype(bf16), xs, modl, norm2[l], seg_of_block_for(256), 256,
                                f32 if moe_layer else bf16)
        i = l // 2
        if not moe_layer:
            xs = _dense_ffn_call(h2, _tile_major(dense_w1[i].astype(bf16), tn_proj),
                                 _tile_major(dense_w3[i].astype(bf16), tn_proj),
                                 dense_w2[i].astype(bf16), xn, modl, seg_of_block_for(tm), tm)
        else:
            xs = _moe_call(h2, xn, modl, moe_router[i], moe_w1[i], moe_w3[i].astype(bf16),
                           moe_w2[i].astype(bf16), seg_of_block_for(256))
    return xs[:nl].reshape(nbatch, seq, d)
```

```python
import functools
import math

import jax
import jax.numpy as jnp
from jax import lax
from jax.experimental import pallas as pl
from jax.experimental.pallas import tpu as pltpu

f32 = jnp.float32
bf16 = jnp.bfloat16

HEAD_DIM = 128
GRID_W = 64
ROPE_PAIRS = HEAD_DIM // 4
ROPE_THETA = 10000.0
EPS = 1e-6
NEG_INF = -1e30
WINDOW = 128
A_HEADS, A_KV_HEADS = 8, 2
B_HEADS = 4
C_HEADS, C_KV_HEADS = 8, 2
GQA_GROUP = 4
TOP_K = 2
LANES = 128
MXU_WIDTH = 256
CHUNK_GROUP = 8
MIB = 1024 * 1024
LOG2E = math.log2(math.e)

A_Q, A_KV = A_HEADS * HEAD_DIM, A_KV_HEADS * HEAD_DIM
B_QK, B_V = B_HEADS * 2 * HEAD_DIM, B_HEADS * 2 * HEAD_DIM
C_Q, C_KV = C_HEADS * HEAD_DIM, C_KV_HEADS * HEAD_DIM
QK_CQ, QK_AQ, QK_BQ, QK_BK = 0, C_Q, C_Q + A_Q, C_Q + A_Q + B_QK
QK_CK = QK_BK + B_QK
QK_AK = QK_CK + C_KV
QK_COLS = QK_AK + A_KV
V_BV, V_CV = 0, B_V
V_AV = V_CV + C_KV_HEADS * 2 * HEAD_DIM
V_COLS = V_AV + A_KV


def _cparams(sem, vmem_mib):
    return pltpu.CompilerParams(dimension_semantics=sem, vmem_limit_bytes=vmem_mib * MIB)


def _dot(a, b):
    return jnp.dot(a, b, preferred_element_type=f32)


def _dot_nt(a, b):
    return lax.dot_general(a, b, (((1,), (1,)), ((), ())), preferred_element_type=f32)


def _sigmoid(x):
    return 0.5 * jnp.tanh(0.5 * x) + 0.5


def _rowmax(s):
    return jnp.max(s, axis=-1, keepdims=True)


def _rowsum(s):
    return jnp.sum(s, axis=-1, keepdims=True)


def _lane_tile(x, n):
    return jnp.concatenate([x] * n, axis=1)


def _mod_kernel(c_ref, w_ref, b_ref, o_ref):
    c = c_ref[...]
    a = (c * _sigmoid(c)).astype(bf16)
    o_ref[0] = _dot(a, w_ref[0].astype(bf16)) + b_ref[0]


def _mod_call(cc, w_mod, b_mod):
    depth, d, n = w_mod.shape
    tn = 1024
    return pl.pallas_call(
        _mod_kernel,
        out_shape=jax.ShapeDtypeStruct((depth, 8, n), f32),
        grid=(depth, n // tn),
        in_specs=[pl.BlockSpec((8, d), lambda l, j: (0, 0)),
                  pl.BlockSpec((1, d, tn), lambda l, j: (l, 0, j)),
                  pl.BlockSpec((1, 1, tn), lambda l, j: (l, 0, j))],
        out_specs=pl.BlockSpec((1, 8, tn), lambda l, j: (l, 0, j)),
        compiler_params=_cparams(("arbitrary", "arbitrary"), 40),
        name="mod_vectors",
    )(cc, w_mod, b_mod.reshape(depth, 1, n))


def _norm_kernel(x_ref, nw_ref, sc_ref, sh_ref, o_ref):
    x = x_ref[...]
    y = x * lax.rsqrt(jnp.mean(x * x, axis=-1, keepdims=True) + EPS) * nw_ref[...]
    o_ref[...] = (y * (1.0 + sc_ref[0]) + sh_ref[0]).astype(o_ref.dtype)


def _norm_call(x, nw, modl, k_sc, k_sh, seg_of_block, tm, out_dtype):
    r, d = x.shape
    return pl.pallas_call(
        _norm_kernel,
        out_shape=jax.ShapeDtypeStruct((r, d), out_dtype),
        grid=(r // tm,),
        in_specs=[pl.BlockSpec((tm, d), lambda i: (i, 0)),
                  pl.BlockSpec((1, d), lambda i: (0, 0)),
                  pl.BlockSpec((1, 1, d), lambda i: (seg_of_block(i) * 6 + k_sc, 0, 0)),
                  pl.BlockSpec((1, 1, d), lambda i: (seg_of_block(i) * 6 + k_sh, 0, 0))],
        out_specs=pl.BlockSpec((tm, d), lambda i: (i, 0)),
        compiler_params=_cparams(("arbitrary",), 32),
        name="norm_modulate",
    )(x, nw.reshape(1, d), modl, modl)


def _qk_epilogue(acc_ref, o_ref, gain_ref, cos_ref, sin_ref, ones_ref):
    cos, sin = cos_ref[...], sin_ref[...]
    for hd in range(o_ref.shape[1] // HEAD_DIM):
        sl = slice(hd * HEAD_DIM, (hd + 1) * HEAD_DIM)
        t = acc_ref[:, sl]
        ss = _dot((t * t).astype(bf16), ones_ref[...])
        y = t * lax.rsqrt(ss * (1.0 / HEAD_DIM) + EPS) * gain_ref[:, sl]
        o_ref[:, sl] = (y * cos + pltpu.roll(y, HEAD_DIM // 2, axis=1) * sin).astype(o_ref.dtype)


def _gate_epilogue(acc_ref, o_ref):
    o_ref[...] = _sigmoid(acc_ref[...]).astype(o_ref.dtype)


def _deferred_mm_kernel(h_ref, w_ref, *rest, epilogue, n_extra):
    extra, o_ref, acc_ref = rest[:n_extra], rest[n_extra], rest[n_extra + 1]
    t = pl.program_id(0)

    @pl.when(t == 0)
    def _():
        acc_ref[1] = jnp.zeros(acc_ref.shape[1:], f32)

    for slot in (0, 1):
        @pl.when(lax.rem(t, 2) == slot)
        def _(slot=slot):
            epilogue(acc_ref.at[1 - slot], o_ref, *extra)
            acc_ref[slot] = _dot(h_ref[...], w_ref[...])


def _tile_major(w, tn):
    lead, (k, n) = w.shape[:-2], w.shape[-2:]
    return jnp.swapaxes(w.reshape(lead + (k, n // tn, tn)), -3, -2)


def _row_tile(nrows):
    return next(t for t in (1024, 768, 512) if nrows % t == 0)


def _deferred_mm_call(epilogue, h, w, extra, extra_specs, nrows, tm, name):
    k = h.shape[1]
    nj, _, tn = w.shape
    n = nj * tn
    n_tiles = (nrows // tm) * nj

    def cur(t):
        tt = jnp.minimum(t, n_tiles - 1)
        return tt // nj, tt % nj

    def prev(t):
        tp = jnp.maximum(t - 1, 0)
        return tp // nj, tp % nj

    return pl.pallas_call(
        functools.partial(_deferred_mm_kernel, epilogue=epilogue, n_extra=len(extra)),
        out_shape=jax.ShapeDtypeStruct((nrows, n), bf16),
        grid=(n_tiles + 1,),
        in_specs=[pl.BlockSpec((tm, k), lambda t: (cur(t)[0], 0)),
                  pl.BlockSpec((None, k, tn), lambda t: (cur(t)[1], 0, 0))]
                 + [pl.BlockSpec(shape, functools.partial(lambda t, f: f(*prev(t)), f=f))
                    for shape, f in extra_specs],
        out_specs=pl.BlockSpec((tm, tn), lambda t: prev(t)),
        scratch_shapes=[pltpu.VMEM((2, tm, tn), f32)],
        compiler_params=_cparams(("arbitrary",), 40),
        name=name,
    )(h, w, *extra)


def _bias_proj_kernel(h_ref, w_ref, b_ref, o_ref):
    o_ref[...] = (_dot(h_ref[...], w_ref[...]) + b_ref[...]).astype(o_ref.dtype)


def _proj_call(kernel, h, w, extra, extra_specs, nrows, tm, tn, name):
    k = h.shape[1]
    n = w.shape[1]
    return pl.pallas_call(
        kernel,
        out_shape=jax.ShapeDtypeStruct((nrows, n), bf16),
        grid=(nrows // tm, n // tn),
        in_specs=[pl.BlockSpec((tm, k), lambda i, j: (i, 0)),
                  pl.BlockSpec((k, tn), lambda i, j: (0, j))] + extra_specs,
        out_specs=pl.BlockSpec((tm, tn), lambda i, j: (i, j)),
        compiler_params=_cparams(("arbitrary", "arbitrary"), 40),
        name=name,
    )(h, w, *extra)


def _stack_heads(q, n):
    return jnp.concatenate([q[:, g * HEAD_DIM:(g + 1) * HEAD_DIM] for g in range(n)], axis=0)


def _unstack_heads(o, n):
    t = o.shape[0] // n
    return jnp.concatenate([o[g * t:(g + 1) * t] for g in range(n)], axis=1)


def _softmax_pv(s, v, ex, extra_logit=None):
    m = _rowmax(s)
    if extra_logit is not None:
        m = jnp.maximum(m, extra_logit)
    p = ex(s - m)
    l = _rowsum(p)
    if extra_logit is not None:
        l = l + ex(extra_logit - m)
    return _dot(p.astype(bf16), v) / l


def _online_softmax_sweep(scores, v_ref, m_ref, l_ref, acc_ref, *, tk, n_chunks):
    group = next(g for g in (CHUNK_GROUP, 4, 2, 1) if n_chunks % g == 0)
    width = acc_ref.shape[1] // LANES

    def body(gi, carry):
        base = pl.multiple_of(gi * (group * tk), group * tk)
        s_next = scores(base)
        for c in range(group):
            off = base + c * tk
            s = s_next
            if c + 1 < group:
                s_next = scores(off + tk)
            m_old = m_ref[...]
            m_new = jnp.maximum(m_old, _rowmax(s))
            alpha = jnp.exp2(m_old - m_new)
            p = jnp.exp2(s - _lane_tile(m_new, tk // LANES))
            if l_ref is not None:
                part = p[:, :LANES]
                for j in range(1, tk // LANES):
                    part = part + p[:, j * LANES:(j + 1) * LANES]
                l_ref[...] = alpha * l_ref[...] + part
            acc_ref[...] = (_lane_tile(alpha, width) * acc_ref[...]
                            + _dot(p.astype(bf16), v_ref[pl.ds(off, tk), :]))
            m_ref[...] = m_new
        return carry

    lax.fori_loop(0, n_chunks // group, body, 0)


def _sink_column(sink_ref, kh, tq):
    return jnp.concatenate(
        [jnp.full((tq, 1), sink_ref[kh * GQA_GROUP + g], f32) for g in range(GQA_GROUP)], axis=0)


def _diff_finish(on0, on1, lam_ref, g_ref, lam_init):
    bl = lam_ref[...]
    lam = (jnp.exp(_rowsum(bl[0:1] * bl[1:2])) - jnp.exp(_rowsum(bl[2:3] * bl[3:4])) + lam_init)
    o = on0 - lam * on1
    y = o * lax.rsqrt(jnp.mean(o * o, axis=-1, keepdims=True) + EPS) * g_ref[...]
    return y * (1.0 - lam_init)


def _attn_a_lat_kernel(sink_ref, q_ref, kl_ref, kc_ref, vl_ref, vc_ref, rel_ref, o_ref, *, tq, nsub, seq):
    kh = pl.program_id(1)
    qi = pl.program_id(2)
    band = tq + 2 * WINDOW
    sink = _sink_column(sink_ref, kh, tq)
    scores, probs = [], []
    for sub in range(nsub):
        qs = _stack_heads(q_ref[sub * tq:(sub + 1) * tq, :], GQA_GROUP)
        q0 = (qi * nsub + sub) * tq
        start = pl.multiple_of(jnp.clip(q0 - WINDOW, 0, seq - band), WINDOW)
        s_ctx = _dot_nt(qs, kc_ref[...])
        s_loc = _dot_nt(qs, kl_ref[pl.ds(start, band), :])
        s_loc = s_loc + rel_ref[(q0 - start) // WINDOW]
        scores.append((s_ctx, s_loc, start))
    for s_ctx, s_loc, start in scores:
        m = jnp.maximum(jnp.maximum(_rowmax(s_ctx), _rowmax(s_loc)), sink)
        p_ctx = jnp.exp2(s_ctx - m)
        p_loc = jnp.exp2(s_loc - m)
        l = _rowsum(p_ctx) + _rowsum(p_loc) + jnp.exp2(sink - m)
        probs.append((p_ctx.astype(bf16), p_loc.astype(bf16), l, start))
    for sub, (p_ctx, p_loc, l, start) in enumerate(probs):
        o = (_dot(p_ctx, vc_ref[...]) + _dot(p_loc, vl_ref[pl.ds(start, band), :])) / l
        o_ref[sub * tq:(sub + 1) * tq, :] = _unstack_heads(o, GQA_GROUP).astype(o_ref.dtype)


def _attn_a_ctx_kernel(sink_ref, q_ref, kc_ref, vc_ref, o_ref, *, tq):
    qs = _stack_heads(q_ref[...], GQA_GROUP)
    o = _softmax_pv(_dot_nt(qs, kc_ref[...]), vc_ref[...], jnp.exp2,
                    _sink_column(sink_ref, pl.program_id(1), tq))
    o_ref[...] = _unstack_heads(o, GQA_GROUP).astype(o_ref.dtype)


def _attn_c_lat_kernel(q_ref, kl_ref, kc_ref, vl_ref, vc_ref, o_ref, m_ref, acc_ref, *, tk, n_chunks):
    qs = _stack_heads(q_ref[...], GQA_GROUP)
    s = _dot_nt(qs, kc_ref[...])
    m0 = _rowmax(s)
    m_ref[...] = jnp.broadcast_to(m0, m_ref.shape)
    acc_ref[...] = _dot(jnp.exp2(s - m0).astype(bf16), vc_ref[...])

    _online_softmax_sweep(lambda off: _dot_nt(qs, kl_ref[pl.ds(off, tk), :]), vl_ref, m_ref, None, acc_ref,
                          tk=tk, n_chunks=n_chunks)
    acc = acc_ref[...]
    o = acc[:, :HEAD_DIM] / acc[:, HEAD_DIM:]
    o_ref[...] = _unstack_heads(o, GQA_GROUP).astype(o_ref.dtype)


def _attn_c_ctx_kernel(q_ref, kc_ref, vc_ref, o_ref):
    qs = _stack_heads(q_ref[...], GQA_GROUP)
    o = _softmax_pv(_dot_nt(qs, kc_ref[...]), vc_ref[...], jnp.exp2)
    o_ref[...] = _unstack_heads(o, GQA_GROUP).astype(o_ref.dtype)


def _diff_scores(q, k):
    return jnp.concatenate([_dot_nt(q[:, :HEAD_DIM], k[:, :HEAD_DIM]),
                            _dot_nt(q[:, HEAD_DIM:], k[:, HEAD_DIM:])], axis=0)


def _attn_b_lat_kernel(q_ref, kl_ref, kc_ref, vl_ref, vc_ref, lam_ref, g_ref, o_ref,
                       m_ref, l_ref, acc_ref, *, tk, n_chunks, lam_init):
    q = q_ref[...]
    tq = q.shape[0]
    s = _diff_scores(q, kc_ref[...])
    m0 = _rowmax(s)
    p = jnp.exp2(s - m0)
    m_ref[...] = jnp.broadcast_to(m0, m_ref.shape)
    part = p[:, :LANES]
    for j in range(1, p.shape[1] // LANES):
        part = part + p[:, j * LANES:(j + 1) * LANES]
    l_ref[...] = part
    acc_ref[...] = _dot(p.astype(bf16), vc_ref[...])

    _online_softmax_sweep(lambda off: _diff_scores(q, kl_ref[pl.ds(off, tk), :]), vl_ref, m_ref, l_ref, acc_ref,
                          tk=tk, n_chunks=n_chunks)
    on = acc_ref[...] / _rowsum(l_ref[...])
    o_ref[...] = _diff_finish(on[:tq], on[tq:], lam_ref, g_ref, lam_init).astype(o_ref.dtype)


def _attn_b_ctx_kernel(q_ref, kc_ref, vc_ref, lam_ref, g_ref, o_ref, *, lam_init):
    q = q_ref[...]
    tq = q.shape[0]
    on = _softmax_pv(_diff_scores(q, kc_ref[...]), vc_ref[...], jnp.exp2)
    o_ref[...] = _diff_finish(on[:tq], on[tq:], lam_ref, g_ref, lam_init).astype(o_ref.dtype)


def _attention(qk, v, a_sink, b_lam, b_subln, *, nbatch, seq, ctx, with_ctx, lam_init):
    tq_win = 256
    tq_gqa = 256
    tq_diff = min(512, seq)
    tq_ctx = ctx
    tk = min(512, seq)
    nl_rows = nbatch * seq
    nc_rows = nbatch * ctx
    cb = nl_rows // ctx
    gw = GQA_GROUP * HEAD_DIM
    bw = 2 * HEAD_DIM
    sem = ("arbitrary", "arbitrary", "arbitrary")

    def lat_q(tq, width, off):
        return pl.BlockSpec((tq, width), lambda b, h, qi, *_: (b * (seq // tq) + qi, off // width + h))

    def ctx_q(width, off):
        return pl.BlockSpec((tq_ctx, width), lambda b, h, qi, *_: (cb + b, off // width + h))

    def lat_kv(width, off):
        return pl.BlockSpec((seq, width), lambda b, h, qi, *_: (b, off // width + h))

    def ctx_kv(width, off):
        return pl.BlockSpec((ctx, width), lambda b, h, qi, *_: (cb + b, off // width + h))

    def ctx_out(width):
        return pl.BlockSpec((tq_ctx, width), lambda b, h, qi, *_: (b, h))

    nsub = next(n for n in (4, 2, 1) if seq % (n * tq_win) == 0)
    band = tq_win + 2 * WINDOW
    m_win = GQA_GROUP * tq_win
    dist = ((jnp.arange(m_win, dtype=jnp.int32) % tq_win)[None, :, None]
            - jnp.arange(band, dtype=jnp.int32)[None, None, :]
            + WINDOW * jnp.arange(3, dtype=jnp.int32)[:, None, None])
    rel = jnp.where(jnp.abs(dist) <= WINDOW, 0.0, NEG_INF).astype(f32)
    a_sink = a_sink * LOG2E
    oa = pl.pallas_call(
        functools.partial(_attn_a_lat_kernel, tq=tq_win, nsub=nsub, seq=seq),
        out_shape=jax.ShapeDtypeStruct((nl_rows, A_Q), bf16),
        grid_spec=pltpu.PrefetchScalarGridSpec(
            num_scalar_prefetch=1,
            grid=(nbatch, A_KV_HEADS, seq // (nsub * tq_win)),
            in_specs=[lat_q(nsub * tq_win, gw, QK_AQ), lat_kv(HEAD_DIM, QK_AK), ctx_kv(HEAD_DIM, QK_AK),
                      lat_kv(HEAD_DIM, V_AV), ctx_kv(HEAD_DIM, V_AV),
                      pl.BlockSpec((3, m_win, band), lambda b, h, qi, *_: (0, 0, 0))],
            out_specs=lat_q(nsub * tq_win, gw, 0),
        ),
        compiler_params=_cparams(sem, 48),
        name="attn_window",
    )(a_sink, qk, qk, qk, v, v, rel)
    if with_ctx:
        oa_ctx = pl.pallas_call(
            functools.partial(_attn_a_ctx_kernel, tq=tq_ctx),
            out_shape=jax.ShapeDtypeStruct((nc_rows, A_Q), bf16),
            grid_spec=pltpu.PrefetchScalarGridSpec(
                num_scalar_prefetch=1,
                grid=(nbatch, A_KV_HEADS, 1),
                in_specs=[ctx_q(gw, QK_AQ), ctx_kv(HEAD_DIM, QK_AK), ctx_kv(HEAD_DIM, V_AV)],
                out_specs=ctx_out(gw),
            ),
            compiler_params=_cparams(sem, 32),
            name="attn_window_ctx",
        )(a_sink, qk, qk, v)
        oa = jnp.concatenate([oa, oa_ctx], axis=0)

    m_rows = GQA_GROUP * tq_gqa
    oc = pl.pallas_call(
        functools.partial(_attn_c_lat_kernel, tk=tk, n_chunks=seq // tk),
        out_shape=jax.ShapeDtypeStruct((nl_rows, C_Q), bf16),
        grid=(nbatch, C_KV_HEADS, seq // tq_gqa),
        in_specs=[lat_q(tq_gqa, gw, QK_CQ), lat_kv(HEAD_DIM, QK_CK), ctx_kv(HEAD_DIM, QK_CK),
                  lat_kv(2 * HEAD_DIM, V_CV), ctx_kv(2 * HEAD_DIM, V_CV)],
        out_specs=lat_q(tq_gqa, gw, 0),
        scratch_shapes=[pltpu.VMEM((m_rows, LANES), f32), pltpu.VMEM((m_rows, 2 * HEAD_DIM), f32)],
        compiler_params=_cparams(sem, 56),
        name="attn_global",
    )(qk, qk, qk, v, v)
    if with_ctx:
        oc_ctx = pl.pallas_call(
            _attn_c_ctx_kernel,
            out_shape=jax.ShapeDtypeStruct((nc_rows, C_Q), bf16),
            grid=(nbatch, C_KV_HEADS, 1),
            in_specs=[ctx_q(gw, QK_CQ), ctx_kv(HEAD_DIM, QK_CK),
                      pl.BlockSpec((ctx, HEAD_DIM), lambda b, h, qi: (cb + b, V_CV // HEAD_DIM + 2 * h))],
            out_specs=ctx_out(gw),
            compiler_params=_cparams(sem, 32),
            name="attn_global_ctx",
        )(qk, qk, v)
        oc = jnp.concatenate([oc, oc_ctx], axis=0)

    lam_spec = pl.BlockSpec((4, HEAD_DIM), lambda b, h, qi: (0, 0))
    g_spec = pl.BlockSpec((1, bw), lambda b, h, qi: (0, 0))
    g2d = b_subln.reshape(1, bw)
    ob = pl.pallas_call(
        functools.partial(_attn_b_lat_kernel, tk=tk, n_chunks=seq // tk, lam_init=lam_init),
        out_shape=jax.ShapeDtypeStruct((nl_rows, B_V), bf16),
        grid=(nbatch, B_HEADS, seq // tq_diff),
        in_specs=[lat_q(tq_diff, bw, QK_BQ), lat_kv(bw, QK_BK), ctx_kv(bw, QK_BK),
                  lat_kv(bw, V_BV), ctx_kv(bw, V_BV), lam_spec, g_spec],
        out_specs=lat_q(tq_diff, bw, 0),
        scratch_shapes=[pltpu.VMEM((2 * tq_diff, LANES), f32), pltpu.VMEM((2 * tq_diff, LANES), f32),
                        pltpu.VMEM((2 * tq_diff, bw), f32)],
        compiler_params=_cparams(sem, 56),
        name="attn_diff",
    )(qk, qk, qk, v, v, b_lam, g2d)
    if with_ctx:
        ob_ctx = pl.pallas_call(
            functools.partial(_attn_b_ctx_kernel, lam_init=lam_init),
            out_shape=jax.ShapeDtypeStruct((nc_rows, B_V), bf16),
            grid=(nbatch, B_HEADS, 1),
            in_specs=[ctx_q(bw, QK_BQ), ctx_kv(bw, QK_BK), ctx_kv(bw, V_BV), lam_spec, g_spec],
            out_specs=ctx_out(bw),
            compiler_params=_cparams(sem, 32),
            name="attn_diff_ctx",
        )(qk, qk, v, b_lam, g2d)
        ob = jnp.concatenate([ob, ob_ctx], axis=0)
    return oa, ob, oc


def _merge_kernel(oa_ref, ob_ref, oc_ref, w_ref, ga_ref, gb_ref, gc_ref, y_ref):
    y = (ga_ref[...].astype(f32) * _dot(oa_ref[...], w_ref[0])
         + gb_ref[...].astype(f32) * _dot(ob_ref[...], w_ref[1])
         + gc_ref[...].astype(f32) * _dot(oc_ref[...], w_ref[2]))
    y_ref[...] = y.astype(y_ref.dtype)


def _merge_call(oa, ob, oc, wb, gates, tm):
    rows, bwid = oa.shape
    nj, _, _, tn = wb.shape
    d = nj * tn
    o_spec = pl.BlockSpec((tm, bwid), lambda i, j: (i, 0))
    return pl.pallas_call(
        _merge_kernel,
        out_shape=jax.ShapeDtypeStruct((rows, d), bf16),
        grid=(rows // tm, nj),
        in_specs=[o_spec, o_spec, o_spec,
                  pl.BlockSpec((None, 3, bwid, tn), lambda i, j: (j, 0, 0, 0)),
                  pl.BlockSpec((tm, tn), lambda i, j: (i, j)),
                  pl.BlockSpec((tm, tn), lambda i, j: (i, nj + j)),
                  pl.BlockSpec((tm, tn), lambda i, j: (i, 2 * nj + j))],
        out_specs=pl.BlockSpec((tm, tn), lambda i, j: (i, j)),
        compiler_params=_cparams(("arbitrary", "arbitrary"), 40),
        name="branch_merge",
    )(oa, ob, oc, wb, gates, gates, gates)


def _out_proj_kernel(y_ref, w_ref, x_ref, g1_ref, nw_ref, sc_ref, sh_ref, xo_ref, h_ref):
    xn = x_ref[...] + g1_ref[0] * _dot(y_ref[...], w_ref[...])
    xo_ref[...] = xn
    t = xn * lax.rsqrt(jnp.mean(xn * xn, axis=-1, keepdims=True) + EPS) * nw_ref[...]
    h_ref[...] = (t * (1.0 + sc_ref[0]) + sh_ref[0]).astype(h_ref.dtype)


def _out_proj_call(y, w_out, x, modl, nw2, seg_of_block, tm, h_dtype):
    rows, d = y.shape

    def mod_spec(k):
        return pl.BlockSpec((1, 1, d), lambda i: (seg_of_block(i) * 6 + k, 0, 0))

    row_spec = pl.BlockSpec((tm, d), lambda i: (i, 0))
    return pl.pallas_call(
        _out_proj_kernel,
        out_shape=(jax.ShapeDtypeStruct((rows, d), f32), jax.ShapeDtypeStruct((rows, d), h_dtype)),
        grid=(rows // tm,),
        in_specs=[row_spec, pl.BlockSpec((d, d), lambda i: (0, 0)), row_spec,
                  mod_spec(2), pl.BlockSpec((1, d), lambda i: (0, 0)), mod_spec(4), mod_spec(3)],
        out_specs=(row_spec, row_spec),
        compiler_params=_cparams(("arbitrary",), 48),
        name="out_proj_residual_norm",
    )(y, w_out, x, modl, nw2.reshape(1, d), modl, modl)


def _swiglu_stream_step(t, h_ref, w1_ref, w3_ref, w2_ref, acc_ref, a_ref, b_ref):
    for slot in (0, 1):
        @pl.when(lax.rem(t, 2) == slot)
        def _(slot=slot):
            h = h_ref[...]
            a_ref[slot] = _dot(h, w1_ref[...].astype(bf16))
            a = a_ref[1 - slot]
            b = b_ref[1 - slot]
            g = (a * _sigmoid(a) * b).astype(bf16)
            acc_ref[...] += _dot(g, w2_ref[...])
            b_ref[slot] = _dot(h, w3_ref[...].astype(bf16))


def _dense_ffn_kernel(h_ref, w1_ref, w3_ref, w2_ref, x_ref, g2_ref, o_ref, acc_ref, a_ref, b_ref, *, nf):
    t = pl.program_id(0)
    jp = lax.rem(jnp.maximum(t - 1, 0), nf)

    @pl.when(t == 0)
    def _():
        a_ref[1] = jnp.zeros(a_ref.shape[1:], f32)
        b_ref[1] = jnp.zeros(b_ref.shape[1:], f32)

    @pl.when(jp == 0)
    def _():
        acc_ref[...] = jnp.zeros_like(acc_ref)

    _swiglu_stream_step(t, h_ref, w1_ref, w3_ref, w2_ref, acc_ref, a_ref, b_ref)

    @pl.when(jnp.logical_and(jp == nf - 1, t > 0))
    def _():
        o_ref[...] = x_ref[...] + g2_ref[0] * acc_ref[...]


def _dense_ffn_call(h2, w1, w3, w2, x, modl, seg_of_block, tm):
    rows, d = h2.shape
    nf, _, tf = w1.shape
    n_pairs = (rows // tm) * nf

    def cur(t):
        tt = jnp.minimum(t, n_pairs - 1)
        return tt // nf, tt % nf

    def prev(t):
        tp = jnp.maximum(t - 1, 0)
        return tp // nf, tp % nf

    prev_row = pl.BlockSpec((tm, d), lambda t: (prev(t)[0], 0))
    return pl.pallas_call(
        functools.partial(_dense_ffn_kernel, nf=nf),
        out_shape=jax.ShapeDtypeStruct((rows, d), f32),
        grid=(n_pairs + 1,),
        in_specs=[pl.BlockSpec((tm, d), lambda t: (cur(t)[0], 0)),
                  pl.BlockSpec((None, d, tf), lambda t: (cur(t)[1], 0, 0)),
                  pl.BlockSpec((None, d, tf), lambda t: (cur(t)[1], 0, 0)),
                  pl.BlockSpec((tf, d), lambda t: (prev(t)[1], 0)),
                  prev_row,
                  pl.BlockSpec((1, 1, d), lambda t: (seg_of_block(prev(t)[0]) * 6 + 5, 0, 0))],
        out_specs=prev_row,
        scratch_shapes=[pltpu.VMEM((tm, d), f32), pltpu.VMEM((2, tm, tf), f32), pltpu.VMEM((2, tm, tf), f32)],
        compiler_params=_cparams(("arbitrary",), 56),
        name="dense_swiglu",
    )(h2, w1, w3, w2, x, modl)


def _router_kernel(h_ref, w_ref, idx_ref, wt_ref, *, n_experts):
    logits = jnp.dot(h_ref[...], w_ref[...], preferred_element_type=f32,
                     precision=lax.Precision.HIGHEST)
    lane = lax.broadcasted_iota(jnp.int32, logits.shape, 1)
    lg = jnp.where(lane < n_experts, logits, -jnp.inf)
    m1 = _rowmax(lg)
    i1 = jnp.min(jnp.where(lg == m1, lane, LANES), axis=-1, keepdims=True)
    lg2 = jnp.where(lane == i1, -jnp.inf, lg)
    m2 = _rowmax(lg2)
    i2 = jnp.min(jnp.where(lg2 == m2, lane, LANES), axis=-1, keepdims=True)
    e2 = jnp.exp(m2 - m1)
    w1 = 1.0 / (1.0 + e2)
    w2 = e2 / (1.0 + e2)
    idx_ref[...] = jnp.where(lane == 0, i1, jnp.where(lane == 1, i2, 0))
    wt_ref[...] = jnp.where(lane == 0, w1, jnp.where(lane == 1, w2, 0.0))


def _router_call(h2, w_router, tm):
    n, d = h2.shape
    e = w_router.shape[1]
    wp = jnp.zeros((d, LANES), f32).at[:, :e].set(w_router)
    return pl.pallas_call(
        functools.partial(_router_kernel, n_experts=e),
        out_shape=(jax.ShapeDtypeStruct((n, LANES), jnp.int32), jax.ShapeDtypeStruct((n, LANES), f32)),
        grid=(n // tm,),
        in_specs=[pl.BlockSpec((tm, d), lambda i: (i, 0)), pl.BlockSpec((d, LANES), lambda i: (0, 0))],
        out_specs=(pl.BlockSpec((tm, LANES), lambda i: (i, 0)), pl.BlockSpec((tm, LANES), lambda i: (i, 0))),
        compiler_params=_cparams(("arbitrary",), 32),
        name="moe_router_top2",
    )(h2, wp)


def _gather_rows_kernel(src_ref, x_hbm, o_ref, buf, sem, *, chunk):
    i = pl.program_id(0)
    slot = lax.rem(i, 2)

    def row_copy(step, sl, r):
        return pltpu.make_async_copy(x_hbm.at[pl.ds(src_ref[step * chunk + r], 1)],
                                     buf.at[sl, pl.ds(r, 1)], sem.at[sl])

    def issue(step, sl):
        def body(r, c):
            row_copy(step, sl, r).start()
            return c
        lax.fori_loop(0, chunk, body, 0, unroll=8)

    def drain(step, sl):
        def body(r, c):
            row_copy(step, sl, r).wait()
            return c
        lax.fori_loop(0, chunk, body, 0, unroll=8)

    @pl.when(i == 0)
    def _():
        issue(0, 0)

    @pl.when(i + 1 < pl.num_programs(0))
    def _():
        issue(i + 1, 1 - slot)

    drain(i, slot)
    o_ref[...] = buf[slot].astype(o_ref.dtype)


def _gather_rows_call(x, src, chunk):
    cap = src.shape[0]
    d = x.shape[1]
    return pl.pallas_call(
        functools.partial(_gather_rows_kernel, chunk=chunk),
        out_shape=jax.ShapeDtypeStruct((cap, d), bf16),
        grid_spec=pltpu.PrefetchScalarGridSpec(
            num_scalar_prefetch=1,
            grid=(cap // chunk,),
            in_specs=[pl.BlockSpec(memory_space=pl.ANY)],
            out_specs=pl.BlockSpec((chunk, d), lambda i, s: (i, 0)),
            scratch_shapes=[pltpu.VMEM((2, chunk, d), x.dtype), pltpu.SemaphoreType.DMA((2,))],
        ),
        compiler_params=_cparams(("arbitrary",), 32),
        name="moe_gather_rows",
    )(src, x)


def _expert_kernel(be_ref, nu_ref, x_ref, w1_ref, w3_ref, w2_ref, o_ref, acc_ref, a_ref, b_ref, *, nf):
    t = pl.program_id(0)
    tp = jnp.maximum(t - 1, 0)
    ip = tp // nf
    jp = lax.rem(tp, nf)
    n_live = nu_ref[0] * nf
    live = t <= n_live

    @pl.when(t == 0)
    def _():
        a_ref[1] = jnp.zeros(a_ref.shape[1:], f32)
        b_ref[1] = jnp.zeros(b_ref.shape[1:], f32)

    @pl.when(jnp.logical_and(jp == 0, live))
    def _():
        acc_ref[...] = jnp.zeros_like(acc_ref)

    @pl.when(live)
    def _():
        _swiglu_stream_step(t, x_ref, w1_ref.at[0], w3_ref.at[0], w2_ref.at[0], acc_ref, a_ref, b_ref)

    last = jnp.logical_and(jp == nf - 1, t > 0)

    @pl.when(jnp.logical_and(last, ip < nu_ref[0]))
    def _():
        o_ref[...] = acc_ref[...]

    @pl.when(jnp.logical_and(last, ip >= nu_ref[0]))
    def _():
        o_ref[...] = jnp.zeros_like(o_ref)


def _expert_call(xs, blk_e, n_used, w1, w3, w2, tm, tf):
    cap, d = xs.shape
    nf = w1.shape[2] // tf
    n_pairs = (cap // tm) * nf

    def pair(tt, nu):
        i, j = tt // nf, tt % nf
        return jnp.minimum(i, nu[0] - 1), jnp.where(i < nu[0], j, nf - 1)

    def cur(t, nu):
        return pair(jnp.minimum(t, n_pairs - 1), nu)

    def prev(t, nu):
        return pair(jnp.maximum(t - 1, 0), nu)

    def up_spec():
        return pl.BlockSpec((1, d, tf), lambda t, be, nu: (be[cur(t, nu)[0]], 0, cur(t, nu)[1]))

    return pl.pallas_call(
        functools.partial(_expert_kernel, nf=nf),
        out_shape=jax.ShapeDtypeStruct((cap, d), f32),
        grid_spec=pltpu.PrefetchScalarGridSpec(
            num_scalar_prefetch=2,
            grid=(n_pairs + 1,),
            in_specs=[pl.BlockSpec((tm, d), lambda t, be, nu: (cur(t, nu)[0], 0)),
                      up_spec(), up_spec(),
                      pl.BlockSpec((1, tf, d), lambda t, be, nu: (be[prev(t, nu)[0]], prev(t, nu)[1], 0))],
            out_specs=pl.BlockSpec((tm, d), lambda t, be, nu: (jnp.maximum(t - 1, 0) // nf, 0)),
            scratch_shapes=[pltpu.VMEM((tm, d), f32), pltpu.VMEM((2, tm, tf), f32),
                            pltpu.VMEM((2, tm, tf), f32)],
        ),
        compiler_params=_cparams(("arbitrary",), 56),
        name="moe_expert_swiglu",
    )(blk_e, n_used, xs, w1, w3, w2)


def _combine_kernel(dest_ref, y_hbm, x_ref, g2_ref, wt_ref, o_ref, buf0, buf1, sem, *, tm):
    i = pl.program_id(0)
    slot = lax.rem(i, 2)

    def copies(step, sl, r):
        s = (step * tm + r) * TOP_K
        return (pltpu.make_async_copy(y_hbm.at[pl.ds(dest_ref[s], 1)], buf0.at[sl, pl.ds(r, 1)], sem.at[sl]),
                pltpu.make_async_copy(y_hbm.at[pl.ds(dest_ref[s + 1], 1)], buf1.at[sl, pl.ds(r, 1)],
                                      sem.at[sl]))

    def issue(step, sl):
        def body(r, c):
            c0, c1 = copies(step, sl, r)
            c0.start()
            c1.start()
            return c
        lax.fori_loop(0, tm, body, 0, unroll=8)

    def drain(step, sl):
        def body(r, c):
            c0, c1 = copies(step, sl, r)
            c0.wait()
            c1.wait()
            return c
        lax.fori_loop(0, tm, body, 0, unroll=8)

    @pl.when(i == 0)
    def _():
        issue(0, 0)

    @pl.when(i + 1 < pl.num_programs(0))
    def _():
        issue(i + 1, 1 - slot)

    drain(i, slot)
    wt = wt_ref[...]
    mix = wt[:, 0:1] * buf0[slot] + wt[:, 1:2] * buf1[slot]
    o_ref[...] = x_ref[...] + g2_ref[0] * mix


def _combine_call(yb, dest, x, modl, wts, seg_of_block, tm):
    n, d = x.shape
    return pl.pallas_call(
        functools.partial(_combine_kernel, tm=tm),
        out_shape=jax.ShapeDtypeStruct((n, d), f32),
        grid_spec=pltpu.PrefetchScalarGridSpec(
            num_scalar_prefetch=1,
            grid=(n // tm,),
            in_specs=[pl.BlockSpec(memory_space=pl.ANY),
                      pl.BlockSpec((tm, d), lambda i, dst: (i, 0)),
                      pl.BlockSpec((1, 1, d), lambda i, dst: (seg_of_block(i) * 6 + 5, 0, 0)),
                      pl.BlockSpec((tm, LANES), lambda i, dst: (i, 0))],
            out_specs=pl.BlockSpec((tm, d), lambda i, dst: (i, 0)),
            scratch_shapes=[pltpu.VMEM((2, tm, d), f32), pltpu.VMEM((2, tm, d), f32),
                            pltpu.SemaphoreType.DMA((2,))],
        ),
        compiler_params=_cparams(("arbitrary",), 32),
        name="moe_combine_residual",
    )(dest, yb, x, modl, wts)


def _moe_call(h2, x, modl, w_router, w1, w3, w2, seg_of_block):
    n, d = h2.shape
    n_exp = w_router.shape[1]
    tm = 512
    idx, wts = _router_call(h2, w_router, 256)
    flat_e = idx[:, :TOP_K].reshape(-1)
    n_slots = n * TOP_K
    onehot = (flat_e[:, None] == jnp.arange(n_exp, dtype=jnp.int32)[None, :]).astype(jnp.int32)
    csum = jnp.cumsum(onehot, axis=0)
    counts = csum[-1]
    rank = jnp.sum((csum - 1) * onehot, axis=1)
    padded = (counts + tm - 1) // tm * tm
    pend = jnp.cumsum(padded)
    pstart = pend - padded
    dest = (pstart[flat_e] + rank).astype(jnp.int32)
    cap = (n_slots // tm + n_exp) * tm
    src = jnp.zeros((cap,), jnp.int32).at[dest].set(jnp.arange(n_slots, dtype=jnp.int32) // TOP_K)
    n_blocks = cap // tm
    blk_start = jnp.arange(n_blocks, dtype=jnp.int32) * tm
    blk_e = jnp.minimum(jnp.searchsorted(pend, blk_start, side='right'), n_exp - 1).astype(jnp.int32)
    n_used = (pend[-1:] // tm).astype(jnp.int32)

    xs = _gather_rows_call(h2, src, 256)
    yb = _expert_call(xs, blk_e, n_used, w1, w3, w2, tm, 512)
    return _combine_call(yb, dest, x, modl, wts, seg_of_block, 256)


def _rope_tables(nbatch, seq, ctx):
    t = jnp.arange(seq, dtype=jnp.int32)
    row = (t // GRID_W).astype(f32)
    col = (t % GRID_W).astype(f32)
    inv = ROPE_THETA ** (-jnp.arange(ROPE_PAIRS, dtype=f32) / ROPE_PAIRS)
    lane = jnp.arange(HEAD_DIM)
    pos = jnp.where(((lane // ROPE_PAIRS) % 2)[None, :] == 0, row[:, None], col[:, None])
    ang = pos * inv[lane % ROPE_PAIRS][None, :]
    cos = jnp.cos(ang)
    sin = jnp.where((lane < HEAD_DIM // 2)[None, :], -jnp.sin(ang), jnp.sin(ang))
    nctx = nbatch * ctx

    def full(tab, fill):
        return jnp.concatenate([jnp.tile(tab, (nbatch, 1)), jnp.full((nctx, HEAD_DIM), fill, f32)], axis=0)

    return full(cos, 1.0), full(sin, 0.0)


def _head_layout(w):
    lead = w.shape[:-1]
    w5 = w.reshape(lead + (w.shape[-1] // HEAD_DIM, 2, 2, ROPE_PAIRS))
    return jnp.swapaxes(w5, -3, -2).reshape(w.shape)


def kernel(x, c, ctx, c_ctx, w_mod, b_mod, norm1, norm2, w_in, qk_gain, a_sink, b_lambda, b_subln,
           w_branch, w_out, dense_w1, dense_w3, dense_w2, moe_router, moe_w1, moe_w3, moe_w2):
    nbatch, seq, d = x.shape
    nctx_len = ctx.shape[1]
    depth = w_mod.shape[0]
    nl = nbatch * seq
    nc = nbatch * nctx_len
    tm = 512
    tn_proj = 512

    def seg_of_block_for(t):
        return lambda i: jnp.minimum(i // (seq // t), nbatch)

    xs = jnp.concatenate([x.reshape(nl, d), ctx.reshape(nc, d)], axis=0)
    cc = jnp.zeros((8, d), f32).at[:nbatch].set(c).at[nbatch].set(c_ctx)
    mod = _mod_call(cc, w_mod, b_mod)
    cos, sin = _rope_tables(nbatch, seq, nctx_len)
    ones_mat = jnp.ones((HEAD_DIM, HEAD_DIM), bf16)
    scale = HEAD_DIM ** -0.5
    ones_cols = jnp.ones((HEAD_DIM,), f32)
    zero_w = jnp.zeros((d, HEAD_DIM), bf16)

    for l in range(depth):
        with_ctx = l < depth - 1
        lam_init = 0.8 - 0.6 * math.exp(-0.3 * l)
        rows = nl + nc if with_ctx else nl
        modl = mod[l].reshape(8 * 6, 1, d)

        wl = w_in[l]
        o = 0
        parts = {}
        for name, width in (("aq", A_Q), ("ak", A_KV), ("av", A_KV), ("bq", B_QK), ("bk", B_QK),
                            ("bv", B_V), ("cq", C_Q), ("ck", C_KV), ("cv", C_KV), ("g", 3 * d)):
            parts[name] = wl[:, o:o + width].astype(bf16)
            o += width
        w_qk = _head_layout(jnp.concatenate([parts[k] for k in ("cq", "aq", "bq", "bk", "ck", "ak")], axis=1))
        cv = parts["cv"]
        cv_cols = []
        for hd in range(C_KV_HEADS):
            cv_cols += [cv[:, hd * HEAD_DIM:(hd + 1) * HEAD_DIM], zero_w]
        w_v = jnp.concatenate([parts["bv"]] + cv_cols + [parts["av"]], axis=1)
        zeros_bv = jnp.zeros((B_V,), f32)
        zeros_h = jnp.zeros((HEAD_DIM,), f32)
        bias_v = jnp.concatenate([zeros_bv] + [zeros_h, ones_cols] * C_KV_HEADS
                                 + [jnp.zeros((A_KV,), f32)]).reshape(1, V_COLS)
        w_g = parts["g"]
        g = qk_gain[l]
        gain = _head_layout(jnp.concatenate(
            [jnp.tile(g[4] * (scale * LOG2E), C_HEADS), jnp.tile(g[0] * (scale * LOG2E), A_HEADS),
             jnp.tile(g[2] * (scale * LOG2E), 2 * B_HEADS), jnp.tile(g[3], 2 * B_HEADS),
             jnp.tile(g[5], C_KV_HEADS), jnp.tile(g[1], A_KV_HEADS)]).reshape(1, QK_COLS))

        h = _norm_call(xs, norm1[l], modl, 1, 0, seg_of_block_for(tm), tm, bf16)
        tp = _row_tile(nl + nc)
        tab_spec = ((tp, HEAD_DIM), lambda i, j: (i, 0))
        qk = _deferred_mm_call(_qk_epilogue, h, _tile_major(w_qk, tn_proj), (gain, cos, sin, ones_mat),
                               [((1, tn_proj), lambda i, j: (0, j)), tab_spec, tab_spec,
                                ((HEAD_DIM, HEAD_DIM), lambda i, j: (0, 0))],
                               nl + nc, tp, "qk_proj_norm_rope")
        v = _proj_call(_bias_proj_kernel, h, w_v, (bias_v,), [pl.BlockSpec((1, V_COLS), lambda i, j: (0, 0))],
                       nl + nc, tp, V_COLS, "v_proj")
        gates = _deferred_mm_call(_gate_epilogue, h, _tile_major(w_g, 2 * tn_proj), (), [], rows,
                                  _row_tile(rows), "gate_proj")

        oa, ob, oc = _attention(qk, v, a_sink[l], b_lambda[l], b_subln[l], nbatch=nbatch, seq=seq,
                                ctx=nctx_len, with_ctx=with_ctx, lam_init=lam_init)
        y = _merge_call(oa, ob, oc, jnp.swapaxes(_tile_major(w_branch[l].astype(bf16), tn_proj), 0, 1),
                        gates, tm)
        moe_layer = l % 2 == 1
        xn, h2 = _out_proj_call(y, w_out[l].astype(bf16), xs, modl, norm2[l], seg_of_block_for(256), 256,
                                f32 if moe_layer else bf16)
        i = l // 2
        if not moe_layer:
            xs = _dense_ffn_call(h2, _tile_major(dense_w1[i].astype(bf16), tn_proj),
                                 _tile_major(dense_w3[i].astype(bf16), tn_proj),
                                 dense_w2[i].astype(bf16), xn, modl, seg_of_block_for(tm), tm)
        else:
            xs = _moe_call(h2, xn, modl, moe_router[i], moe_w1[i], moe_w3[i].astype(bf16),
                           moe_w2[i].astype(bf16), seg_of_block_for(256))
    return xs[:nl].reshape(nbatch, seq, d)
```

```python
import functools
import math

import jax
import jax.numpy as jnp
from jax import lax
from jax.experimental import pallas as pl
from jax.experimental.pallas import tpu as pltpu

f32 = jnp.float32
bf16 = jnp.bfloat16

HEAD_DIM = 128
GRID_W = 64
ROPE_PAIRS = HEAD_DIM // 4
ROPE_THETA = 10000.0
EPS = 1e-6
NEG_INF = -1e30
WINDOW = 128
A_HEADS, A_KV_HEADS = 8, 2
B_HEADS = 4
C_HEADS, C_KV_HEADS = 8, 2
GQA_GROUP = 4
TOP_K = 2
LANES = 128
MIB = 1024 * 1024
LOG2E = math.log2(math.e)

ROW_TILE = 512
NARROW_ROW_TILE = 256
PROJ_ROW_TILES = (1024, 768, 512)
COL_TILE = 512
GATE_COL_TILE = 1024
MOD_COL_TILE = 1024
MOE_ROW_TILE = 512
GATHER_ROWS = 256
DMA_ISSUE_UNROLL = 8
TQ_WINDOW = 256
WINDOW_SUBBLOCKS = (4, 2, 1)
TQ_GLOBAL = 256
TQ_DIFF = 512
KEY_CHUNK = 512
CHUNK_GROUP = 8

A_Q, A_KV = A_HEADS * HEAD_DIM, A_KV_HEADS * HEAD_DIM
B_QK, B_V = B_HEADS * 2 * HEAD_DIM, B_HEADS * 2 * HEAD_DIM
C_Q, C_KV = C_HEADS * HEAD_DIM, C_KV_HEADS * HEAD_DIM
QK_CQ, QK_AQ, QK_BQ, QK_BK = 0, C_Q, C_Q + A_Q, C_Q + A_Q + B_QK
QK_CK = QK_BK + B_QK
QK_AK = QK_CK + C_KV
QK_COLS = QK_AK + A_KV
V_BV, V_CV = 0, B_V
V_AV = V_CV + C_KV_HEADS * 2 * HEAD_DIM
V_COLS = V_AV + A_KV


def _cparams(sem, vmem_mib):
    return pltpu.CompilerParams(dimension_semantics=sem, vmem_limit_bytes=vmem_mib * MIB)


def _dot(a, b):
    return jnp.dot(a, b, preferred_element_type=f32)


def _dot_nt(a, b):
    return lax.dot_general(a, b, (((1,), (1,)), ((), ())), preferred_element_type=f32)


def _sigmoid(x):
    return 0.5 * jnp.tanh(0.5 * x) + 0.5


def _rowmax(s):
    return jnp.max(s, axis=-1, keepdims=True)


def _rowsum(s):
    return jnp.sum(s, axis=-1, keepdims=True)


def _lane_tile(x, n):
    return jnp.concatenate([x] * n, axis=1)


def _mod_kernel(c_ref, w_ref, b_ref, o_ref):
    c = c_ref[...]
    a = (c * _sigmoid(c)).astype(bf16)
    o_ref[0] = _dot(a, w_ref[0].astype(bf16)) + b_ref[0]


def _mod_call(cc, w_mod, b_mod):
    depth, d, n = w_mod.shape
    tn = MOD_COL_TILE
    return pl.pallas_call(
        _mod_kernel,
        out_shape=jax.ShapeDtypeStruct((depth, 8, n), f32),
        grid=(depth, n // tn),
        in_specs=[pl.BlockSpec((8, d), lambda l, j: (0, 0)),
                  pl.BlockSpec((1, d, tn), lambda l, j: (l, 0, j)),
                  pl.BlockSpec((1, 1, tn), lambda l, j: (l, 0, j))],
        out_specs=pl.BlockSpec((1, 8, tn), lambda l, j: (l, 0, j)),
        compiler_params=_cparams(("arbitrary", "arbitrary"), 40),
        name="mod_vectors",
    )(cc, w_mod, b_mod.reshape(depth, 1, n))


def _norm_kernel(x_ref, nw_ref, sc_ref, sh_ref, o_ref):
    x = x_ref[...]
    y = x * lax.rsqrt(jnp.mean(x * x, axis=-1, keepdims=True) + EPS) * nw_ref[...]
    o_ref[...] = (y * (1.0 + sc_ref[0]) + sh_ref[0]).astype(o_ref.dtype)


def _norm_call(x, nw, modl, k_sc, k_sh, seg_of_block, tm, out_dtype):
    r, d = x.shape
    return pl.pallas_call(
        _norm_kernel,
        out_shape=jax.ShapeDtypeStruct((r, d), out_dtype),
        grid=(r // tm,),
        in_specs=[pl.BlockSpec((tm, d), lambda i: (i, 0)),
                  pl.BlockSpec((1, d), lambda i: (0, 0)),
                  pl.BlockSpec((1, 1, d), lambda i: (seg_of_block(i) * 6 + k_sc, 0, 0)),
                  pl.BlockSpec((1, 1, d), lambda i: (seg_of_block(i) * 6 + k_sh, 0, 0))],
        out_specs=pl.BlockSpec((tm, d), lambda i: (i, 0)),
        compiler_params=_cparams(("arbitrary",), 32),
        name="norm_modulate",
    )(x, nw.reshape(1, d), modl, modl)


def _qk_epilogue(acc_ref, o_ref, gain_ref, cos_ref, sin_ref, ones_ref):
    cos, sin = cos_ref[...], sin_ref[...]
    for hd in range(o_ref.shape[1] // HEAD_DIM):
        sl = slice(hd * HEAD_DIM, (hd + 1) * HEAD_DIM)
        t = acc_ref[:, sl]
        ss = _dot((t * t).astype(bf16), ones_ref[...])
        y = t * lax.rsqrt(ss * (1.0 / HEAD_DIM) + EPS) * gain_ref[:, sl]
        o_ref[:, sl] = (y * cos + pltpu.roll(y, HEAD_DIM // 2, axis=1) * sin).astype(o_ref.dtype)


def _gate_epilogue(acc_ref, o_ref):
    o_ref[...] = _sigmoid(acc_ref[...]).astype(o_ref.dtype)


def _deferred_mm_kernel(h_ref, w_ref, *rest, epilogue, n_extra):
    extra, o_ref, acc_ref = rest[:n_extra], rest[n_extra], rest[n_extra + 1]
    t = pl.program_id(0)

    @pl.when(t == 0)
    def _():
        acc_ref[1] = jnp.zeros(acc_ref.shape[1:], f32)

    for slot in (0, 1):
        @pl.when(lax.rem(t, 2) == slot)
        def _(slot=slot):
            epilogue(acc_ref.at[1 - slot], o_ref, *extra)
            acc_ref[slot] = _dot(h_ref[...], w_ref[...])


def _tile_major(w, tn):
    lead, (k, n) = w.shape[:-2], w.shape[-2:]
    return jnp.swapaxes(w.reshape(lead + (k, n // tn, tn)), -3, -2)


def _row_tile(nrows):
    return next(t for t in PROJ_ROW_TILES if nrows % t == 0)


def _deferred_mm_call(epilogue, h, w, extra, extra_specs, nrows, tm, name):
    k = h.shape[1]
    nj, _, tn = w.shape
    n = nj * tn
    n_tiles = (nrows // tm) * nj

    def cur(t):
        tt = jnp.minimum(t, n_tiles - 1)
        return tt // nj, tt % nj

    def prev(t):
        tp = jnp.maximum(t - 1, 0)
        return tp // nj, tp % nj

    return pl.pallas_call(
        functools.partial(_deferred_mm_kernel, epilogue=epilogue, n_extra=len(extra)),
        out_shape=jax.ShapeDtypeStruct((nrows, n), bf16),
        grid=(n_tiles + 1,),
        in_specs=[pl.BlockSpec((tm, k), lambda t: (cur(t)[0], 0)),
                  pl.BlockSpec((None, k, tn), lambda t: (cur(t)[1], 0, 0))]
                 + [pl.BlockSpec(shape, functools.partial(lambda t, f: f(*prev(t)), f=f))
                    for shape, f in extra_specs],
        out_specs=pl.BlockSpec((tm, tn), lambda t: prev(t)),
        scratch_shapes=[pltpu.VMEM((2, tm, tn), f32)],
        compiler_params=_cparams(("arbitrary",), 40),
        name=name,
    )(h, w, *extra)


def _bias_proj_kernel(h_ref, w_ref, b_ref, o_ref):
    o_ref[...] = (_dot(h_ref[...], w_ref[...]) + b_ref[...]).astype(o_ref.dtype)


def _proj_call(kernel, h, w, extra, extra_specs, nrows, tm, tn, name):
    k = h.shape[1]
    n = w.shape[1]
    return pl.pallas_call(
        kernel,
        out_shape=jax.ShapeDtypeStruct((nrows, n), bf16),
        grid=(nrows // tm, n // tn),
        in_specs=[pl.BlockSpec((tm, k), lambda i, j: (i, 0)),
                  pl.BlockSpec((k, tn), lambda i, j: (0, j))] + extra_specs,
        out_specs=pl.BlockSpec((tm, tn), lambda i, j: (i, j)),
        compiler_params=_cparams(("arbitrary", "arbitrary"), 40),
        name=name,
    )(h, w, *extra)


def _stack_heads(q, n):
    return jnp.concatenate([q[:, g * HEAD_DIM:(g + 1) * HEAD_DIM] for g in range(n)], axis=0)


def _unstack_heads(o, n):
    t = o.shape[0] // n
    return jnp.concatenate([o[g * t:(g + 1) * t] for g in range(n)], axis=1)


def _softmax_pv(s, v, ex, extra_logit=None):
    m = _rowmax(s)
    if extra_logit is not None:
        m = jnp.maximum(m, extra_logit)
    p = ex(s - m)
    l = _rowsum(p)
    if extra_logit is not None:
        l = l + ex(extra_logit - m)
    return _dot(p.astype(bf16), v) / l


def _online_softmax_sweep(scores, v_ref, m_ref, l_ref, acc_ref, *, tk, n_chunks):
    group = next(g for g in (CHUNK_GROUP, 4, 2, 1) if n_chunks % g == 0)
    width = acc_ref.shape[1] // LANES

    def body(gi, carry):
        base = pl.multiple_of(gi * (group * tk), group * tk)
        s_next = scores(base)
        for c in range(group):
            off = base + c * tk
            s = s_next
            if c + 1 < group:
                s_next = scores(off + tk)
            m_old = m_ref[...]
            m_new = jnp.maximum(m_old, _rowmax(s))
            alpha = jnp.exp2(m_old - m_new)
            p = jnp.exp2(s - _lane_tile(m_new, tk // LANES))
            if l_ref is not None:
                part = p[:, :LANES]
                for j in range(1, tk // LANES):
                    part = part + p[:, j * LANES:(j + 1) * LANES]
                l_ref[...] = alpha * l_ref[...] + part
            acc_ref[...] = (_lane_tile(alpha, width) * acc_ref[...]
                            + _dot(p.astype(bf16), v_ref[pl.ds(off, tk), :]))
            m_ref[...] = m_new
        return carry

    lax.fori_loop(0, n_chunks // group, body, 0)


def _sink_column(sink_ref, kh, tq):
    return jnp.concatenate(
        [jnp.full((tq, 1), sink_ref[kh * GQA_GROUP + g], f32) for g in range(GQA_GROUP)], axis=0)


def _diff_finish(on0, on1, lam_ref, g_ref, lam_init):
    bl = lam_ref[...]
    lam = (jnp.exp(_rowsum(bl[0:1] * bl[1:2])) - jnp.exp(_rowsum(bl[2:3] * bl[3:4])) + lam_init)
    o = on0 - lam * on1
    y = o * lax.rsqrt(jnp.mean(o * o, axis=-1, keepdims=True) + EPS) * g_ref[...]
    return y * (1.0 - lam_init)


def _attn_a_lat_kernel(sink_ref, q_ref, kl_ref, kc_ref, vl_ref, vc_ref, rel_ref, o_ref, *, tq, nsub, seq):
    kh = pl.program_id(1)
    qi = pl.program_id(2)
    band = tq + 2 * WINDOW
    sink = _sink_column(sink_ref, kh, tq)
    scores, probs = [], []
    for sub in range(nsub):
        qs = _stack_heads(q_ref[sub * tq:(sub + 1) * tq, :], GQA_GROUP)
        q0 = (qi * nsub + sub) * tq
        start = pl.multiple_of(jnp.clip(q0 - WINDOW, 0, seq - band), WINDOW)
        s_ctx = _dot_nt(qs, kc_ref[...])
        s_loc = _dot_nt(qs, kl_ref[pl.ds(start, band), :])
        s_loc = s_loc + rel_ref[(q0 - start) // WINDOW]
        scores.append((s_ctx, s_loc, start))
    for s_ctx, s_loc, start in scores:
        m = jnp.maximum(jnp.maximum(_rowmax(s_ctx), _rowmax(s_loc)), sink)
        p_ctx = jnp.exp2(s_ctx - m)
        p_loc = jnp.exp2(s_loc - m)
        l = _rowsum(p_ctx) + _rowsum(p_loc) + jnp.exp2(sink - m)
        probs.append((p_ctx.astype(bf16), p_loc.astype(bf16), l, start))
    for sub, (p_ctx, p_loc, l, start) in enumerate(probs):
        o = (_dot(p_ctx, vc_ref[...]) + _dot(p_loc, vl_ref[pl.ds(start, band), :])) / l
        o_ref[sub * tq:(sub + 1) * tq, :] = _unstack_heads(o, GQA_GROUP).astype(o_ref.dtype)


def _attn_a_ctx_kernel(sink_ref, q_ref, kc_ref, vc_ref, o_ref, *, tq):
    qs = _stack_heads(q_ref[...], GQA_GROUP)
    o = _softmax_pv(_dot_nt(qs, kc_ref[...]), vc_ref[...], jnp.exp2,
                    _sink_column(sink_ref, pl.program_id(1), tq))
    o_ref[...] = _unstack_heads(o, GQA_GROUP).astype(o_ref.dtype)


def _attn_c_lat_kernel(q_ref, kl_ref, kc_ref, vl_ref, vc_ref, o_ref, m_ref, acc_ref, *, tk, n_chunks):
    qs = _stack_heads(q_ref[...], GQA_GROUP)
    s = _dot_nt(qs, kc_ref[...])
    m0 = _rowmax(s)
    m_ref[...] = jnp.broadcast_to(m0, m_ref.shape)
    acc_ref[...] = _dot(jnp.exp2(s - m0).astype(bf16), vc_ref[...])

    _online_softmax_sweep(lambda off: _dot_nt(qs, kl_ref[pl.ds(off, tk), :]), vl_ref, m_ref, None, acc_ref,
                          tk=tk, n_chunks=n_chunks)
    acc = acc_ref[...]
    o = acc[:, :HEAD_DIM] / acc[:, HEAD_DIM:]
    o_ref[...] = _unstack_heads(o, GQA_GROUP).astype(o_ref.dtype)


def _attn_c_ctx_kernel(q_ref, kc_ref, vc_ref, o_ref):
    qs = _stack_heads(q_ref[...], GQA_GROUP)
    o = _softmax_pv(_dot_nt(qs, kc_ref[...]), vc_ref[...], jnp.exp2)
    o_ref[...] = _unstack_heads(o, GQA_GROUP).astype(o_ref.dtype)


def _diff_scores(q, k):
    return jnp.concatenate([_dot_nt(q[:, :HEAD_DIM], k[:, :HEAD_DIM]),
                            _dot_nt(q[:, HEAD_DIM:], k[:, HEAD_DIM:])], axis=0)


def _attn_b_lat_kernel(q_ref, kl_ref, kc_ref, vl_ref, vc_ref, lam_ref, g_ref, o_ref,
                       m_ref, l_ref, acc_ref, *, tk, n_chunks, lam_init):
    q = q_ref[...]
    tq = q.shape[0]
    s = _diff_scores(q, kc_ref[...])
    m0 = _rowmax(s)
    p = jnp.exp2(s - m0)
    m_ref[...] = jnp.broadcast_to(m0, m_ref.shape)
    part = p[:, :LANES]
    for j in range(1, p.shape[1] // LANES):
        part = part + p[:, j * LANES:(j + 1) * LANES]
    l_ref[...] = part
    acc_ref[...] = _dot(p.astype(bf16), vc_ref[...])

    _online_softmax_sweep(lambda off: _diff_scores(q, kl_ref[pl.ds(off, tk), :]), vl_ref, m_ref, l_ref, acc_ref,
                          tk=tk, n_chunks=n_chunks)
    on = acc_ref[...] / _rowsum(l_ref[...])
    o_ref[...] = _diff_finish(on[:tq], on[tq:], lam_ref, g_ref, lam_init).astype(o_ref.dtype)


def _attn_b_ctx_kernel(q_ref, kc_ref, vc_ref, lam_ref, g_ref, o_ref, *, lam_init):
    q = q_ref[...]
    tq = q.shape[0]
    on = _softmax_pv(_diff_scores(q, kc_ref[...]), vc_ref[...], jnp.exp2)
    o_ref[...] = _diff_finish(on[:tq], on[tq:], lam_ref, g_ref, lam_init).astype(o_ref.dtype)


def _attention(qk, v, a_sink, b_lam, b_subln, *, nbatch, seq, ctx, with_ctx, lam_init):
    tq_win = TQ_WINDOW
    tq_gqa = min(TQ_GLOBAL, seq)
    tq_diff = min(TQ_DIFF, seq)
    tq_ctx = ctx
    tk = min(KEY_CHUNK, seq)
    nl_rows = nbatch * seq
    nc_rows = nbatch * ctx
    cb = nl_rows // ctx
    gw = GQA_GROUP * HEAD_DIM
    bw = 2 * HEAD_DIM
    sem = ("arbitrary", "arbitrary", "arbitrary")

    def lat_q(tq, width, off):
        return pl.BlockSpec((tq, width), lambda b, h, qi, *_: (b * (seq // tq) + qi, off // width + h))

    def ctx_q(width, off):
        return pl.BlockSpec((tq_ctx, width), lambda b, h, qi, *_: (cb + b, off // width + h))

    def lat_kv(width, off):
        return pl.BlockSpec((seq, width), lambda b, h, qi, *_: (b, off // width + h))

    def ctx_kv(width, off):
        return pl.BlockSpec((ctx, width), lambda b, h, qi, *_: (cb + b, off // width + h))

    def ctx_out(width):
        return pl.BlockSpec((tq_ctx, width), lambda b, h, qi, *_: (b, h))

    nsub = next(n for n in WINDOW_SUBBLOCKS if seq % (n * tq_win) == 0)
    band = tq_win + 2 * WINDOW
    m_win = GQA_GROUP * tq_win
    dist = ((jnp.arange(m_win, dtype=jnp.int32) % tq_win)[None, :, None]
            - jnp.arange(band, dtype=jnp.int32)[None, None, :]
            + WINDOW * jnp.arange(3, dtype=jnp.int32)[:, None, None])
    rel = jnp.where(jnp.abs(dist) <= WINDOW, 0.0, NEG_INF).astype(f32)
    a_sink = a_sink * LOG2E
    oa = pl.pallas_call(
        functools.partial(_attn_a_lat_kernel, tq=tq_win, nsub=nsub, seq=seq),
        out_shape=jax.ShapeDtypeStruct((nl_rows, A_Q), bf16),
        grid_spec=pltpu.PrefetchScalarGridSpec(
            num_scalar_prefetch=1,
            grid=(nbatch, A_KV_HEADS, seq // (nsub * tq_win)),
            in_specs=[lat_q(nsub * tq_win, gw, QK_AQ), lat_kv(HEAD_DIM, QK_AK), ctx_kv(HEAD_DIM, QK_AK),
                      lat_kv(HEAD_DIM, V_AV), ctx_kv(HEAD_DIM, V_AV),
                      pl.BlockSpec((3, m_win, band), lambda b, h, qi, *_: (0, 0, 0))],
            out_specs=lat_q(nsub * tq_win, gw, 0),
        ),
        compiler_params=_cparams(sem, 48),
        name="attn_window",
    )(a_sink, qk, qk, qk, v, v, rel)
    if with_ctx:
        oa_ctx = pl.pallas_call(
            functools.partial(_attn_a_ctx_kernel, tq=tq_ctx),
            out_shape=jax.ShapeDtypeStruct((nc_rows, A_Q), bf16),
            grid_spec=pltpu.PrefetchScalarGridSpec(
                num_scalar_prefetch=1,
                grid=(nbatch, A_KV_HEADS, 1),
                in_specs=[ctx_q(gw, QK_AQ), ctx_kv(HEAD_DIM, QK_AK), ctx_kv(HEAD_DIM, V_AV)],
                out_specs=ctx_out(gw),
            ),
            compiler_params=_cparams(sem, 32),
            name="attn_window_ctx",
        )(a_sink, qk, qk, v)
        oa = jnp.concatenate([oa, oa_ctx], axis=0)

    m_rows = GQA_GROUP * tq_gqa
    oc = pl.pallas_call(
        functools.partial(_attn_c_lat_kernel, tk=tk, n_chunks=seq // tk),
        out_shape=jax.ShapeDtypeStruct((nl_rows, C_Q), bf16),
        grid=(nbatch, C_KV_HEADS, seq // tq_gqa),
        in_specs=[lat_q(tq_gqa, gw, QK_CQ), lat_kv(HEAD_DIM, QK_CK), ctx_kv(HEAD_DIM, QK_CK),
                  lat_kv(2 * HEAD_DIM, V_CV), ctx_kv(2 * HEAD_DIM, V_CV)],
        out_specs=lat_q(tq_gqa, gw, 0),
        scratch_shapes=[pltpu.VMEM((m_rows, LANES), f32), pltpu.VMEM((m_rows, 2 * HEAD_DIM), f32)],
        compiler_params=_cparams(sem, 56),
        name="attn_global",
    )(qk, qk, qk, v, v)
    if with_ctx:
        oc_ctx = pl.pallas_call(
            _attn_c_ctx_kernel,
            out_shape=jax.ShapeDtypeStruct((nc_rows, C_Q), bf16),
            grid=(nbatch, C_KV_HEADS, 1),
            in_specs=[ctx_q(gw, QK_CQ), ctx_kv(HEAD_DIM, QK_CK),
                      pl.BlockSpec((ctx, HEAD_DIM), lambda b, h, qi: (cb + b, V_CV // HEAD_DIM + 2 * h))],
            out_specs=ctx_out(gw),
            compiler_params=_cparams(sem, 32),
            name="attn_global_ctx",
        )(qk, qk, v)
        oc = jnp.concatenate([oc, oc_ctx], axis=0)

    lam_spec = pl.BlockSpec((4, HEAD_DIM), lambda b, h, qi: (0, 0))
    g_spec = pl.BlockSpec((1, bw), lambda b, h, qi: (0, 0))
    g2d = b_subln.reshape(1, bw)
    ob = pl.pallas_call(
        functools.partial(_attn_b_lat_kernel, tk=tk, n_chunks=seq // tk, lam_init=lam_init),
        out_shape=jax.ShapeDtypeStruct((nl_rows, B_V), bf16),
        grid=(nbatch, B_HEADS, seq // tq_diff),
        in_specs=[lat_q(tq_diff, bw, QK_BQ), lat_kv(bw, QK_BK), ctx_kv(bw, QK_BK),
                  lat_kv(bw, V_BV), ctx_kv(bw, V_BV), lam_spec, g_spec],
        out_specs=lat_q(tq_diff, bw, 0),
        scratch_shapes=[pltpu.VMEM((2 * tq_diff, LANES), f32), pltpu.VMEM((2 * tq_diff, LANES), f32),
                        pltpu.VMEM((2 * tq_diff, bw), f32)],
        compiler_params=_cparams(sem, 56),
        name="attn_diff",
    )(qk, qk, qk, v, v, b_lam, g2d)
    if with_ctx:
        ob_ctx = pl.pallas_call(
            functools.partial(_attn_b_ctx_kernel, lam_init=lam_init),
            out_shape=jax.ShapeDtypeStruct((nc_rows, B_V), bf16),
            grid=(nbatch, B_HEADS, 1),
            in_specs=[ctx_q(bw, QK_BQ), ctx_kv(bw, QK_BK), ctx_kv(bw, V_BV), lam_spec, g_spec],
            out_specs=ctx_out(bw),
            compiler_params=_cparams(sem, 32),
            name="attn_diff_ctx",
        )(qk, qk, v, b_lam, g2d)
        ob = jnp.concatenate([ob, ob_ctx], axis=0)
    return oa, ob, oc


def _merge_kernel(oa_ref, ob_ref, oc_ref, w_ref, ga_ref, gb_ref, gc_ref, y_ref):
    y = (ga_ref[...].astype(f32) * _dot(oa_ref[...], w_ref[0])
         + gb_ref[...].astype(f32) * _dot(ob_ref[...], w_ref[1])
         + gc_ref[...].astype(f32) * _dot(oc_ref[...], w_ref[2]))
    y_ref[...] = y.astype(y_ref.dtype)


def _merge_call(oa, ob, oc, wb, gates, tm):
    rows, bwid = oa.shape
    nj, _, _, tn = wb.shape
    d = nj * tn
    o_spec = pl.BlockSpec((tm, bwid), lambda i, j: (i, 0))
    return pl.pallas_call(
        _merge_kernel,
        out_shape=jax.ShapeDtypeStruct((rows, d), bf16),
        grid=(rows // tm, nj),
        in_specs=[o_spec, o_spec, o_spec,
                  pl.BlockSpec((None, 3, bwid, tn), lambda i, j: (j, 0, 0, 0)),
                  pl.BlockSpec((tm, tn), lambda i, j: (i, j)),
                  pl.BlockSpec((tm, tn), lambda i, j: (i, nj + j)),
                  pl.BlockSpec((tm, tn), lambda i, j: (i, 2 * nj + j))],
        out_specs=pl.BlockSpec((tm, tn), lambda i, j: (i, j)),
        compiler_params=_cparams(("arbitrary", "arbitrary"), 40),
        name="branch_merge",
    )(oa, ob, oc, wb, gates, gates, gates)


def _out_proj_kernel(y_ref, w_ref, x_ref, g1_ref, nw_ref, sc_ref, sh_ref, xo_ref, h_ref):
    xn = x_ref[...] + g1_ref[0] * _dot(y_ref[...], w_ref[...])
    xo_ref[...] = xn
    t = xn * lax.rsqrt(jnp.mean(xn * xn, axis=-1, keepdims=True) + EPS) * nw_ref[...]
    h_ref[...] = (t * (1.0 + sc_ref[0]) + sh_ref[0]).astype(h_ref.dtype)


def _out_proj_call(y, w_out, x, modl, nw2, seg_of_block, tm, h_dtype):
    rows, d = y.shape

    def mod_spec(k):
        return pl.BlockSpec((1, 1, d), lambda i: (seg_of_block(i) * 6 + k, 0, 0))

    row_spec = pl.BlockSpec((tm, d), lambda i: (i, 0))
    return pl.pallas_call(
        _out_proj_kernel,
        out_shape=(jax.ShapeDtypeStruct((rows, d), f32), jax.ShapeDtypeStruct((rows, d), h_dtype)),
        grid=(rows // tm,),
        in_specs=[row_spec, pl.BlockSpec((d, d), lambda i: (0, 0)), row_spec,
                  mod_spec(2), pl.BlockSpec((1, d), lambda i: (0, 0)), mod_spec(4), mod_spec(3)],
        out_specs=(row_spec, row_spec),
        compiler_params=_cparams(("arbitrary",), 48),
        name="out_proj_residual_norm",
    )(y, w_out, x, modl, nw2.reshape(1, d), modl, modl)


def _swiglu_stream_step(t, h_ref, w1_ref, w3_ref, w2_ref, acc_ref, a_ref, b_ref):
    for slot in (0, 1):
        @pl.when(lax.rem(t, 2) == slot)
        def _(slot=slot):
            h = h_ref[...]
            a_ref[slot] = _dot(h, w1_ref[...].astype(bf16))
            a = a_ref[1 - slot]
            b = b_ref[1 - slot]
            g = (a * _sigmoid(a) * b).astype(bf16)
            acc_ref[...] += _dot(g, w2_ref[...])
            b_ref[slot] = _dot(h, w3_ref[...].astype(bf16))


def _dense_ffn_kernel(h_ref, w1_ref, w3_ref, w2_ref, x_ref, g2_ref, o_ref, acc_ref, a_ref, b_ref, *, nf):
    t = pl.program_id(0)
    jp = lax.rem(jnp.maximum(t - 1, 0), nf)

    @pl.when(t == 0)
    def _():
        a_ref[1] = jnp.zeros(a_ref.shape[1:], f32)
        b_ref[1] = jnp.zeros(b_ref.shape[1:], f32)

    @pl.when(jp == 0)
    def _():
        acc_ref[...] = jnp.zeros_like(acc_ref)

    _swiglu_stream_step(t, h_ref, w1_ref, w3_ref, w2_ref, acc_ref, a_ref, b_ref)

    @pl.when(jnp.logical_and(jp == nf - 1, t > 0))
    def _():
        o_ref[...] = x_ref[...] + g2_ref[0] * acc_ref[...]


def _dense_ffn_call(h2, w1, w3, w2, x, modl, seg_of_block, tm):
    rows, d = h2.shape
    nf, _, tf = w1.shape
    n_pairs = (rows // tm) * nf

    def cur(t):
        tt = jnp.minimum(t, n_pairs - 1)
        return tt // nf, tt % nf

    def prev(t):
        tp = jnp.maximum(t - 1, 0)
        return tp // nf, tp % nf

    prev_row = pl.BlockSpec((tm, d), lambda t: (prev(t)[0], 0))
    return pl.pallas_call(
        functools.partial(_dense_ffn_kernel, nf=nf),
        out_shape=jax.ShapeDtypeStruct((rows, d), f32),
        grid=(n_pairs + 1,),
        in_specs=[pl.BlockSpec((tm, d), lambda t: (cur(t)[0], 0)),
                  pl.BlockSpec((None, d, tf), lambda t: (cur(t)[1], 0, 0)),
                  pl.BlockSpec((None, d, tf), lambda t: (cur(t)[1], 0, 0)),
                  pl.BlockSpec((tf, d), lambda t: (prev(t)[1], 0)),
                  prev_row,
                  pl.BlockSpec((1, 1, d), lambda t: (seg_of_block(prev(t)[0]) * 6 + 5, 0, 0))],
        out_specs=prev_row,
        scratch_shapes=[pltpu.VMEM((tm, d), f32), pltpu.VMEM((2, tm, tf), f32), pltpu.VMEM((2, tm, tf), f32)],
        compiler_params=_cparams(("arbitrary",), 56),
        name="dense_swiglu",
    )(h2, w1, w3, w2, x, modl)


def _router_kernel(h_ref, w_ref, idx_ref, wt_ref, *, n_experts):
    logits = jnp.dot(h_ref[...], w_ref[...], preferred_element_type=f32,
                     precision=lax.Precision.HIGHEST)
    lane = lax.broadcasted_iota(jnp.int32, logits.shape, 1)
    lg = jnp.where(lane < n_experts, logits, -jnp.inf)
    m1 = _rowmax(lg)
    i1 = jnp.min(jnp.where(lg == m1, lane, LANES), axis=-1, keepdims=True)
    lg2 = jnp.where(lane == i1, -jnp.inf, lg)
    m2 = _rowmax(lg2)
    i2 = jnp.min(jnp.where(lg2 == m2, lane, LANES), axis=-1, keepdims=True)
    e2 = jnp.exp(m2 - m1)
    w1 = 1.0 / (1.0 + e2)
    w2 = e2 / (1.0 + e2)
    idx_ref[...] = jnp.where(lane == 0, i1, jnp.where(lane == 1, i2, 0))
    wt_ref[...] = jnp.where(lane == 0, w1, jnp.where(lane == 1, w2, 0.0))


def _router_call(h2, w_router, tm):
    n, d = h2.shape
    e = w_router.shape[1]
    wp = jnp.zeros((d, LANES), f32).at[:, :e].set(w_router)
    return pl.pallas_call(
        functools.partial(_router_kernel, n_experts=e),
        out_shape=(jax.ShapeDtypeStruct((n, LANES), jnp.int32), jax.ShapeDtypeStruct((n, LANES), f32)),
        grid=(n // tm,),
        in_specs=[pl.BlockSpec((tm, d), lambda i: (i, 0)), pl.BlockSpec((d, LANES), lambda i: (0, 0))],
        out_specs=(pl.BlockSpec((tm, LANES), lambda i: (i, 0)), pl.BlockSpec((tm, LANES), lambda i: (i, 0))),
        compiler_params=_cparams(("arbitrary",), 32),
        name="moe_router_top2",
    )(h2, wp)


def _gather_rows_kernel(src_ref, x_hbm, o_ref, buf, sem, *, chunk):
    i = pl.program_id(0)
    slot = lax.rem(i, 2)

    def row_copy(step, sl, r):
        return pltpu.make_async_copy(x_hbm.at[pl.ds(src_ref[step * chunk + r], 1)],
                                     buf.at[sl, pl.ds(r, 1)], sem.at[sl])

    def issue(step, sl):
        def body(r, c):
            row_copy(step, sl, r).start()
            return c
        lax.fori_loop(0, chunk, body, 0, unroll=DMA_ISSUE_UNROLL)

    def drain(step, sl):
        def body(r, c):
            row_copy(step, sl, r).wait()
            return c
        lax.fori_loop(0, chunk, body, 0, unroll=DMA_ISSUE_UNROLL)

    @pl.when(i == 0)
    def _():
        issue(0, 0)

    @pl.when(i + 1 < pl.num_programs(0))
    def _():
        issue(i + 1, 1 - slot)

    drain(i, slot)
    o_ref[...] = buf[slot].astype(o_ref.dtype)


def _gather_rows_call(x, src, chunk):
    cap = src.shape[0]
    d = x.shape[1]
    return pl.pallas_call(
        functools.partial(_gather_rows_kernel, chunk=chunk),
        out_shape=jax.ShapeDtypeStruct((cap, d), bf16),
        grid_spec=pltpu.PrefetchScalarGridSpec(
            num_scalar_prefetch=1,
            grid=(cap // chunk,),
            in_specs=[pl.BlockSpec(memory_space=pl.ANY)],
            out_specs=pl.BlockSpec((chunk, d), lambda i, s: (i, 0)),
            scratch_shapes=[pltpu.VMEM((2, chunk, d), x.dtype), pltpu.SemaphoreType.DMA((2,))],
        ),
        compiler_params=_cparams(("arbitrary",), 32),
        name="moe_gather_rows",
    )(src, x)


def _expert_kernel(be_ref, nu_ref, x_ref, w1_ref, w3_ref, w2_ref, o_ref, acc_ref, a_ref, b_ref, *, nf):
    t = pl.program_id(0)
    tp = jnp.maximum(t - 1, 0)
    ip = tp // nf
    jp = lax.rem(tp, nf)
    n_live = nu_ref[0] * nf
    live = t <= n_live

    @pl.when(t == 0)
    def _():
        a_ref[1] = jnp.zeros(a_ref.shape[1:], f32)
        b_ref[1] = jnp.zeros(b_ref.shape[1:], f32)

    @pl.when(jnp.logical_and(jp == 0, live))
    def _():
        acc_ref[...] = jnp.zeros_like(acc_ref)

    @pl.when(live)
    def _():
        _swiglu_stream_step(t, x_ref, w1_ref.at[0], w3_ref.at[0], w2_ref.at[0], acc_ref, a_ref, b_ref)

    last = jnp.logical_and(jp == nf - 1, t > 0)

    @pl.when(jnp.logical_and(last, ip < nu_ref[0]))
    def _():
        o_ref[...] = acc_ref[...]

    @pl.when(jnp.logical_and(last, ip >= nu_ref[0]))
    def _():
        o_ref[...] = jnp.zeros_like(o_ref)


def _expert_call(xs, blk_e, n_used, w1, w3, w2, tm, tf):
    cap, d = xs.shape
    nf = w1.shape[2] // tf
    n_pairs = (cap // tm) * nf

    def pair(tt, nu):
        i, j = tt // nf, tt % nf
        return jnp.minimum(i, nu[0] - 1), jnp.where(i < nu[0], j, nf - 1)

    def cur(t, nu):
        return pair(jnp.minimum(t, n_pairs - 1), nu)

    def prev(t, nu):
        return pair(jnp.maximum(t - 1, 0), nu)

    def up_spec():
        return pl.BlockSpec((1, d, tf), lambda t, be, nu: (be[cur(t, nu)[0]], 0, cur(t, nu)[1]))

    return pl.pallas_call(
        functools.partial(_expert_kernel, nf=nf),
        out_shape=jax.ShapeDtypeStruct((cap, d), f32),
        grid_spec=pltpu.PrefetchScalarGridSpec(
            num_scalar_prefetch=2,
            grid=(n_pairs + 1,),
            in_specs=[pl.BlockSpec((tm, d), lambda t, be, nu: (cur(t, nu)[0], 0)),
                      up_spec(), up_spec(),
                      pl.BlockSpec((1, tf, d), lambda t, be, nu: (be[prev(t, nu)[0]], prev(t, nu)[1], 0))],
            out_specs=pl.BlockSpec((tm, d), lambda t, be, nu: (jnp.maximum(t - 1, 0) // nf, 0)),
            scratch_shapes=[pltpu.VMEM((tm, d), f32), pltpu.VMEM((2, tm, tf), f32),
                            pltpu.VMEM((2, tm, tf), f32)],
        ),
        compiler_params=_cparams(("arbitrary",), 56),
        name="moe_expert_swiglu",
    )(blk_e, n_used, xs, w1, w3, w2)


def _combine_kernel(dest_ref, y_hbm, x_ref, g2_ref, wt_ref, o_ref, buf0, buf1, sem, *, tm):
    i = pl.program_id(0)
    slot = lax.rem(i, 2)

    def copies(step, sl, r):
        s = (step * tm + r) * TOP_K
        return (pltpu.make_async_copy(y_hbm.at[pl.ds(dest_ref[s], 1)], buf0.at[sl, pl.ds(r, 1)], sem.at[sl]),
                pltpu.make_async_copy(y_hbm.at[pl.ds(dest_ref[s + 1], 1)], buf1.at[sl, pl.ds(r, 1)],
                                      sem.at[sl]))

    def issue(step, sl):
        def body(r, c):
            c0, c1 = copies(step, sl, r)
            c0.start()
            c1.start()
            return c
        lax.fori_loop(0, tm, body, 0, unroll=DMA_ISSUE_UNROLL)

    def drain(step, sl):
        def body(r, c):
            c0, c1 = copies(step, sl, r)
            c0.wait()
            c1.wait()
            return c
        lax.fori_loop(0, tm, body, 0, unroll=DMA_ISSUE_UNROLL)

    @pl.when(i == 0)
    def _():
        issue(0, 0)

    @pl.when(i + 1 < pl.num_programs(0))
    def _():
        issue(i + 1, 1 - slot)

    drain(i, slot)
    wt = wt_ref[...]
    mix = wt[:, 0:1] * buf0[slot] + wt[:, 1:2] * buf1[slot]
    o_ref[...] = x_ref[...] + g2_ref[0] * mix


def _combine_call(yb, dest, x, modl, wts, seg_of_block, tm):
    n, d = x.shape
    return pl.pallas_call(
        functools.partial(_combine_kernel, tm=tm),
        out_shape=jax.ShapeDtypeStruct((n, d), f32),
        grid_spec=pltpu.PrefetchScalarGridSpec(
            num_scalar_prefetch=1,
            grid=(n // tm,),
            in_specs=[pl.BlockSpec(memory_space=pl.ANY),
                      pl.BlockSpec((tm, d), lambda i, dst: (i, 0)),
                      pl.BlockSpec((1, 1, d), lambda i, dst: (seg_of_block(i) * 6 + 5, 0, 0)),
                      pl.BlockSpec((tm, LANES), lambda i, dst: (i, 0))],
            out_specs=pl.BlockSpec((tm, d), lambda i, dst: (i, 0)),
            scratch_shapes=[pltpu.VMEM((2, tm, d), f32), pltpu.VMEM((2, tm, d), f32),
                            pltpu.SemaphoreType.DMA((2,))],
        ),
        compiler_params=_cparams(("arbitrary",), 32),
        name="moe_combine_residual",
    )(dest, yb, x, modl, wts)


def _moe_call(h2, x, modl, w_router, w1, w3, w2, seg_of_block):
    n, d = h2.shape
    n_exp = w_router.shape[1]
    tm = MOE_ROW_TILE
    idx, wts = _router_call(h2, w_router, NARROW_ROW_TILE)
    flat_e = idx[:, :TOP_K].reshape(-1)
    n_slots = n * TOP_K
    onehot = (flat_e[:, None] == jnp.arange(n_exp, dtype=jnp.int32)[None, :]).astype(jnp.int32)
    csum = jnp.cumsum(onehot, axis=0)
    counts = csum[-1]
    rank = jnp.sum((csum - 1) * onehot, axis=1)
    padded = (counts + tm - 1) // tm * tm
    pend = jnp.cumsum(padded)
    pstart = pend - padded
    dest = (pstart[flat_e] + rank).astype(jnp.int32)
    cap = (n_slots // tm + n_exp) * tm
    src = jnp.zeros((cap,), jnp.int32).at[dest].set(jnp.arange(n_slots, dtype=jnp.int32) // TOP_K)
    n_blocks = cap // tm
    blk_start = jnp.arange(n_blocks, dtype=jnp.int32) * tm
    blk_e = jnp.minimum(jnp.searchsorted(pend, blk_start, side='right'), n_exp - 1).astype(jnp.int32)
    n_used = (pend[-1:] // tm).astype(jnp.int32)

    xs = _gather_rows_call(h2, src, GATHER_ROWS)
    yb = _expert_call(xs, blk_e, n_used, w1, w3, w2, tm, COL_TILE)
    return _combine_call(yb, dest, x, modl, wts, seg_of_block, GATHER_ROWS)


def _rope_tables(nbatch, seq, ctx):
    n_rows = seq // GRID_W
    inv = ROPE_THETA ** (-jnp.arange(ROPE_PAIRS, dtype=f32) / ROPE_PAIRS)
    ang_r = jnp.arange(n_rows, dtype=f32)[:, None] * inv[None, :]
    ang_c = jnp.arange(GRID_W, dtype=f32)[:, None] * inv[None, :]

    def per_token(fn):
        r = jnp.repeat(fn(ang_r), GRID_W, axis=0)
        c = jnp.tile(fn(ang_c), (n_rows, 1))
        return r, c

    (cr, cc), (sr, sc) = per_token(jnp.cos), per_token(jnp.sin)
    cos = jnp.concatenate([cr, cc, cr, cc], axis=1)
    sin = jnp.concatenate([-sr, -sc, sr, sc], axis=1)
    nctx = nbatch * ctx

    def full(tab, fill):
        return jnp.concatenate([jnp.tile(tab, (nbatch, 1)), jnp.full((nctx, HEAD_DIM), fill, f32)], axis=0)

    return full(cos, 1.0), full(sin, 0.0)


def _head_layout(w):
    lead = w.shape[:-1]
    w5 = w.reshape(lead + (w.shape[-1] // HEAD_DIM, 2, 2, ROPE_PAIRS))
    return jnp.swapaxes(w5, -3, -2).reshape(w.shape)


def kernel(x, c, ctx, c_ctx, w_mod, b_mod, norm1, norm2, w_in, qk_gain, a_sink, b_lambda, b_subln,
           w_branch, w_out, dense_w1, dense_w3, dense_w2, moe_router, moe_w1, moe_w3, moe_w2):
    nbatch, seq, d = x.shape
    nctx_len = ctx.shape[1]
    depth = w_mod.shape[0]
    nl = nbatch * seq
    nc = nbatch * nctx_len
    tm = ROW_TILE
    tn_proj = COL_TILE

    def seg_of_block_for(t):
        return lambda i: jnp.minimum(i // (seq // t), nbatch)

    xs = jnp.concatenate([x.reshape(nl, d), ctx.reshape(nc, d)], axis=0)
    cc = jnp.zeros((8, d), f32).at[:nbatch].set(c).at[nbatch].set(c_ctx)
    mod = _mod_call(cc, w_mod, b_mod)
    cos, sin = _rope_tables(nbatch, seq, nctx_len)
    ones_mat = jnp.ones((HEAD_DIM, HEAD_DIM), bf16)
    scale = HEAD_DIM ** -0.5
    ones_cols = jnp.ones((HEAD_DIM,), f32)
    zero_w = jnp.zeros((d, HEAD_DIM), bf16)

    for l in range(depth):
        with_ctx = l < depth - 1
        lam_init = 0.8 - 0.6 * math.exp(-0.3 * l)
        rows = nl + nc if with_ctx else nl
        modl = mod[l].reshape(8 * 6, 1, d)

        wl = w_in[l]
        o = 0
        parts = {}
        for name, width in (("aq", A_Q), ("ak", A_KV), ("av", A_KV), ("bq", B_QK), ("bk", B_QK),
                            ("bv", B_V), ("cq", C_Q), ("ck", C_KV), ("cv", C_KV), ("g", 3 * d)):
            parts[name] = wl[:, o:o + width].astype(bf16)
            o += width
        w_qk = _head_layout(jnp.concatenate([parts[k] for k in ("cq", "aq", "bq", "bk", "ck", "ak")], axis=1))
        cv = parts["cv"]
        cv_cols = []
        for hd in range(C_KV_HEADS):
            cv_cols += [cv[:, hd * HEAD_DIM:(hd + 1) * HEAD_DIM], zero_w]
        w_v = jnp.concatenate([parts["bv"]] + cv_cols + [parts["av"]], axis=1)
        zeros_bv = jnp.zeros((B_V,), f32)
        zeros_h = jnp.zeros((HEAD_DIM,), f32)
        bias_v = jnp.concatenate([zeros_bv] + [zeros_h, ones_cols] * C_KV_HEADS
                                 + [jnp.zeros((A_KV,), f32)]).reshape(1, V_COLS)
        w_g = parts["g"]
        g = qk_gain[l]
        gain = _head_layout(jnp.concatenate(
            [jnp.tile(g[4] * (scale * LOG2E), C_HEADS), jnp.tile(g[0] * (scale * LOG2E), A_HEADS),
             jnp.tile(g[2] * (scale * LOG2E), 2 * B_HEADS), jnp.tile(g[3], 2 * B_HEADS),
             jnp.tile(g[5], C_KV_HEADS), jnp.tile(g[1], A_KV_HEADS)]).reshape(1, QK_COLS))

        h = _norm_call(xs, norm1[l], modl, 1, 0, seg_of_block_for(tm), tm, bf16)
        tp = _row_tile(nl + nc)
        tab_spec = ((tp, HEAD_DIM), lambda i, j: (i, 0))
        qk = _deferred_mm_call(_qk_epilogue, h, _tile_major(w_qk, tn_proj), (gain, cos, sin, ones_mat),
                               [((1, tn_proj), lambda i, j: (0, j)), tab_spec, tab_spec,
                                ((HEAD_DIM, HEAD_DIM), lambda i, j: (0, 0))],
                               nl + nc, tp, "qk_proj_norm_rope")
        v = _proj_call(_bias_proj_kernel, h, w_v, (bias_v,), [pl.BlockSpec((1, V_COLS), lambda i, j: (0, 0))],
                       nl + nc, tp, V_COLS, "v_proj")
        gates = _deferred_mm_call(_gate_epilogue, h, _tile_major(w_g, GATE_COL_TILE), (), [], rows,
                                  _row_tile(rows), "gate_proj")

        oa, ob, oc = _attention(qk, v, a_sink[l], b_lambda[l], b_subln[l], nbatch=nbatch, seq=seq,
                                ctx=nctx_len, with_ctx=with_ctx, lam_init=lam_init)
        y = _merge_call(oa, ob, oc, jnp.swapaxes(_tile_major(w_branch[l].astype(bf16), tn_proj), 0, 1),
                        gates, tm)
        moe_layer = l % 2 == 1
        xn, h2 = _out_proj_call(y, w_out[l].astype(bf16), xs, modl, norm2[l], seg_of_block_for(NARROW_ROW_TILE), NARROW_ROW_TILE,
                                f32 if moe_layer else bf16)
        i = l // 2
        if not moe_layer:
            xs = _dense_ffn_call(h2, _tile_major(dense_w1[i].astype(bf16), tn_proj),
                                 _tile_major(dense_w3[i].astype(bf16), tn_proj),
                                 dense_w2[i].astype(bf16), xn, modl, seg_of_block_for(tm), tm)
        else:
            xs = _moe_call(h2, xn, modl, moe_router[i], moe_w1[i], moe_w3[i].astype(bf16),
                           moe_w2[i].astype(bf16), seg_of_block_for(GATHER_ROWS))
    return xs[:nl].reshape(nbatch, seq, d)
```

```python
import functools
import math

import jax
import jax.numpy as jnp
from jax import lax
from jax.experimental import pallas as pl
from jax.experimental.pallas import tpu as pltpu

f32 = jnp.float32
bf16 = jnp.bfloat16

HEAD_DIM = 128
GRID_W = 64
ROPE_PAIRS = HEAD_DIM // 4
ROPE_THETA = 10000.0
EPS = 1e-6
NEG_INF = -1e30
WINDOW = 128
A_HEADS, A_KV_HEADS = 8, 2
B_HEADS = 4
C_HEADS, C_KV_HEADS = 8, 2
GQA_GROUP = 4
TOP_K = 2
LANES = 128
MIB = 1024 * 1024
LOG2E = math.log2(math.e)

ROW_TILE = 512
NARROW_ROW_TILE = 256
PROJ_ROW_TILES = (1024, 768, 512)
COL_TILE = 512
GATE_COL_TILE = 1024
MOD_COL_TILE = 1024
MOE_ROW_TILE = 512
GATHER_ROWS = 512
DMA_ISSUE_UNROLL = 8
TQ_WINDOW = 256
WINDOW_SUBBLOCKS = (4, 2, 1)
TQ_GLOBAL = 256
TQ_DIFF = 512
KEY_CHUNK = 512
CHUNK_GROUP = 8

A_Q, A_KV = A_HEADS * HEAD_DIM, A_KV_HEADS * HEAD_DIM
B_QK, B_V = B_HEADS * 2 * HEAD_DIM, B_HEADS * 2 * HEAD_DIM
C_Q, C_KV = C_HEADS * HEAD_DIM, C_KV_HEADS * HEAD_DIM
QK_CQ, QK_AQ, QK_BQ, QK_BK = 0, C_Q, C_Q + A_Q, C_Q + A_Q + B_QK
QK_CK = QK_BK + B_QK
QK_AK = QK_CK + C_KV
QK_COLS = QK_AK + A_KV
V_BV, V_CV = 0, B_V
V_AV = V_CV + C_KV_HEADS * 2 * HEAD_DIM
V_COLS = V_AV + A_KV


def _cparams(sem, vmem_mib):
    return pltpu.CompilerParams(dimension_semantics=sem, vmem_limit_bytes=vmem_mib * MIB)


def _dot(a, b):
    return jnp.dot(a, b, preferred_element_type=f32)


def _dot_nt(a, b):
    return lax.dot_general(a, b, (((1,), (1,)), ((), ())), preferred_element_type=f32)


def _sigmoid(x):
    return 0.5 * jnp.tanh(0.5 * x) + 0.5


def _rowmax(s):
    return jnp.max(s, axis=-1, keepdims=True)


def _rowsum(s):
    return jnp.sum(s, axis=-1, keepdims=True)


def _lane_tile(x, n):
    return jnp.concatenate([x] * n, axis=1)


def _mod_kernel(c_ref, w_ref, b_ref, o_ref):
    c = c_ref[...]
    a = (c * _sigmoid(c)).astype(bf16)
    o_ref[0] = _dot(a, w_ref[0].astype(bf16)) + b_ref[0]


def _mod_call(cc, w_mod, b_mod):
    depth, d, n = w_mod.shape
    tn = MOD_COL_TILE
    return pl.pallas_call(
        _mod_kernel,
        out_shape=jax.ShapeDtypeStruct((depth, 8, n), f32),
        grid=(depth, n // tn),
        in_specs=[pl.BlockSpec((8, d), lambda l, j: (0, 0)),
                  pl.BlockSpec((1, d, tn), lambda l, j: (l, 0, j)),
                  pl.BlockSpec((1, 1, tn), lambda l, j: (l, 0, j))],
        out_specs=pl.BlockSpec((1, 8, tn), lambda l, j: (l, 0, j)),
        compiler_params=_cparams(("arbitrary", "arbitrary"), 40),
        name="mod_vectors",
    )(cc, w_mod, b_mod.reshape(depth, 1, n))


def _row_stream(parts, tm):
    parts = tuple(parts) if isinstance(parts, (tuple, list)) else (parts,)
    width = parts[0].shape[1]
    n0 = parts[0].shape[0] // tm
    specs = [pl.BlockSpec((tm, width), lambda i, *_: (jnp.minimum(i, n0 - 1), 0))]
    if len(parts) == 2:
        n1 = parts[1].shape[0] // tm
        specs.append(pl.BlockSpec((tm, width), lambda i, *_: (jnp.clip(i - n0, 0, n1 - 1), 0)))
    return parts, specs, sum(p.shape[0] for p in parts)


def _row_block(refs, n_first):
    if len(refs) == 1:
        return refs[0][...]
    return jnp.where(pl.program_id(0) >= n_first, refs[1][...], refs[0][...])


def _norm_kernel(*refs, n_src, n_first):
    x_refs, (nw_ref, sc_ref, sh_ref, o_ref) = refs[:n_src], refs[n_src:]
    x = _row_block(x_refs, n_first)
    y = x * lax.rsqrt(jnp.mean(x * x, axis=-1, keepdims=True) + EPS) * nw_ref[...]
    o_ref[...] = (y * (1.0 + sc_ref[0]) + sh_ref[0]).astype(o_ref.dtype)


def _norm_call(x, nw, modl, k_sc, k_sh, seg_of_block, tm, out_dtype):
    parts, x_specs, r = _row_stream(x, tm)
    d = parts[0].shape[1]
    return pl.pallas_call(
        functools.partial(_norm_kernel, n_src=len(parts), n_first=parts[0].shape[0] // tm),
        out_shape=jax.ShapeDtypeStruct((r, d), out_dtype),
        grid=(r // tm,),
        in_specs=x_specs + [
            pl.BlockSpec((1, d), lambda i: (0, 0)),
            pl.BlockSpec((1, 1, d), lambda i: (seg_of_block(i) * 6 + k_sc, 0, 0)),
            pl.BlockSpec((1, 1, d), lambda i: (seg_of_block(i) * 6 + k_sh, 0, 0))],
        out_specs=pl.BlockSpec((tm, d), lambda i: (i, 0)),
        compiler_params=_cparams(("arbitrary",), 32),
        name="norm_modulate",
    )(*parts, nw.reshape(1, d), modl, modl)


def _qk_epilogue(acc_ref, o_ref, gain_ref, cos_ref, sin_ref, ones_ref):
    cos, sin = cos_ref[...], sin_ref[...]
    for hd in range(o_ref.shape[1] // HEAD_DIM):
        sl = slice(hd * HEAD_DIM, (hd + 1) * HEAD_DIM)
        t = acc_ref[:, sl]
        ss = _dot((t * t).astype(bf16), ones_ref[...])
        y = t * lax.rsqrt(ss * (1.0 / HEAD_DIM) + EPS) * gain_ref[:, sl]
        o_ref[:, sl] = (y * cos + pltpu.roll(y, HEAD_DIM // 2, axis=1) * sin).astype(o_ref.dtype)


def _gate_epilogue(acc_ref, o_ref):
    o_ref[...] = _sigmoid(acc_ref[...]).astype(o_ref.dtype)


def _deferred_mm_kernel(h_ref, w_ref, *rest, epilogue, n_extra):
    extra, o_ref, acc_ref = rest[:n_extra], rest[n_extra], rest[n_extra + 1]
    t = pl.program_id(0)

    @pl.when(t == 0)
    def _():
        acc_ref[1] = jnp.zeros(acc_ref.shape[1:], f32)

    for slot in (0, 1):
        @pl.when(lax.rem(t, 2) == slot)
        def _(slot=slot):
            epilogue(acc_ref.at[1 - slot], o_ref, *extra)
            acc_ref[slot] = _dot(h_ref[...], w_ref[...])


def _tile_major(w, tn):
    lead, (k, n) = w.shape[:-2], w.shape[-2:]
    return jnp.swapaxes(w.reshape(lead + (k, n // tn, tn)), -3, -2)


def _row_tile(nrows):
    return next(t for t in PROJ_ROW_TILES if nrows % t == 0)


def _deferred_mm_call(epilogue, h, w, extra, extra_specs, nrows, tm, name):
    k = h.shape[1]
    nj, _, tn = w.shape
    n = nj * tn
    n_tiles = (nrows // tm) * nj

    def cur(t):
        tt = jnp.minimum(t, n_tiles - 1)
        return tt // nj, tt % nj

    def prev(t):
        tp = jnp.maximum(t - 1, 0)
        return tp // nj, tp % nj

    return pl.pallas_call(
        functools.partial(_deferred_mm_kernel, epilogue=epilogue, n_extra=len(extra)),
        out_shape=jax.ShapeDtypeStruct((nrows, n), bf16),
        grid=(n_tiles + 1,),
        in_specs=[pl.BlockSpec((tm, k), lambda t: (cur(t)[0], 0)),
                  pl.BlockSpec((None, k, tn), lambda t: (cur(t)[1], 0, 0))]
                 + [pl.BlockSpec(shape, functools.partial(lambda t, f: f(*prev(t)), f=f))
                    for shape, f in extra_specs],
        out_specs=pl.BlockSpec((tm, tn), lambda t: prev(t)),
        scratch_shapes=[pltpu.VMEM((2, tm, tn), f32)],
        compiler_params=_cparams(("arbitrary",), 40),
        name=name,
    )(h, w, *extra)


def _bias_proj_kernel(h_ref, w_ref, b_ref, o_ref):
    o_ref[...] = (_dot(h_ref[...], w_ref[...]) + b_ref[...]).astype(o_ref.dtype)


def _proj_call(kernel, h, w, extra, extra_specs, nrows, tm, tn, name):
    k = h.shape[1]
    n = w.shape[1]
    return pl.pallas_call(
        kernel,
        out_shape=jax.ShapeDtypeStruct((nrows, n), bf16),
        grid=(nrows // tm, n // tn),
        in_specs=[pl.BlockSpec((tm, k), lambda i, j: (i, 0)),
                  pl.BlockSpec((k, tn), lambda i, j: (0, j))] + extra_specs,
        out_specs=pl.BlockSpec((tm, tn), lambda i, j: (i, j)),
        compiler_params=_cparams(("arbitrary", "arbitrary"), 40),
        name=name,
    )(h, w, *extra)


def _stack_heads(q, n):
    return jnp.concatenate([q[:, g * HEAD_DIM:(g + 1) * HEAD_DIM] for g in range(n)], axis=0)


def _unstack_heads(o, n):
    t = o.shape[0] // n
    return jnp.concatenate([o[g * t:(g + 1) * t] for g in range(n)], axis=1)


def _softmax_pv(s, v, ex, extra_logit=None):
    m = _rowmax(s)
    if extra_logit is not None:
        m = jnp.maximum(m, extra_logit)
    p = ex(s - m)
    l = _rowsum(p)
    if extra_logit is not None:
        l = l + ex(extra_logit - m)
    return _dot(p.astype(bf16), v) / l


def _online_softmax_sweep(scores, v_ref, m_ref, l_ref, acc_ref, *, tk, n_chunks):
    group = next(g for g in (CHUNK_GROUP, 4, 2, 1) if n_chunks % g == 0)
    width = acc_ref.shape[1] // LANES

    def body(gi, carry):
        base = pl.multiple_of(gi * (group * tk), group * tk)
        s_next = scores(base)
        for c in range(group):
            off = base + c * tk
            s = s_next
            if c + 1 < group:
                s_next = scores(off + tk)
            m_old = m_ref[...]
            m_new = jnp.maximum(m_old, _rowmax(s))
            alpha = jnp.exp2(m_old - m_new)
            p = jnp.exp2(s - _lane_tile(m_new, tk // LANES))
            if l_ref is not None:
                part = p[:, :LANES]
                for j in range(1, tk // LANES):
                    part = part + p[:, j * LANES:(j + 1) * LANES]
                l_ref[...] = alpha * l_ref[...] + part
            acc_ref[...] = (_lane_tile(alpha, width) * acc_ref[...]
                            + _dot(p.astype(bf16), v_ref[pl.ds(off, tk), :]))
            m_ref[...] = m_new
        return carry

    lax.fori_loop(0, n_chunks // group, body, 0)


def _sink_column(sink_ref, kh, tq):
    return jnp.concatenate(
        [jnp.full((tq, 1), sink_ref[kh * GQA_GROUP + g], f32) for g in range(GQA_GROUP)], axis=0)


def _diff_finish(on0, on1, lam_ref, g_ref, lam_init):
    bl = lam_ref[...]
    lam = (jnp.exp(_rowsum(bl[0:1] * bl[1:2])) - jnp.exp(_rowsum(bl[2:3] * bl[3:4])) + lam_init)
    o = on0 - lam * on1
    y = o * lax.rsqrt(jnp.mean(o * o, axis=-1, keepdims=True) + EPS) * g_ref[...]
    return y * (1.0 - lam_init)


def _attn_a_lat_kernel(sink_ref, q_ref, kl_ref, kc_ref, vl_ref, vc_ref, rel_ref, o_ref, *, tq, nsub, seq):
    kh = pl.program_id(1)
    qi = pl.program_id(2)
    band = tq + 2 * WINDOW
    sink = _sink_column(sink_ref, kh, tq)
    scores, probs = [], []
    for sub in range(nsub):
        qs = _stack_heads(q_ref[sub * tq:(sub + 1) * tq, :], GQA_GROUP)
        q0 = (qi * nsub + sub) * tq
        start = pl.multiple_of(jnp.clip(q0 - WINDOW, 0, seq - band), WINDOW)
        s_ctx = _dot_nt(qs, kc_ref[...])
        s_loc = _dot_nt(qs, kl_ref[pl.ds(start, band), :])
        s_loc = s_loc + rel_ref[(q0 - start) // WINDOW]
        scores.append((s_ctx, s_loc, start))
    for s_ctx, s_loc, start in scores:
        m = jnp.maximum(jnp.maximum(_rowmax(s_ctx), _rowmax(s_loc)), sink)
        p_ctx = jnp.exp2(s_ctx - m)
        p_loc = jnp.exp2(s_loc - m)
        l = _rowsum(p_ctx) + _rowsum(p_loc) + jnp.exp2(sink - m)
        probs.append((p_ctx.astype(bf16), p_loc.astype(bf16), l, start))
    for sub, (p_ctx, p_loc, l, start) in enumerate(probs):
        o = (_dot(p_ctx, vc_ref[...]) + _dot(p_loc, vl_ref[pl.ds(start, band), :])) / l
        o_ref[sub * tq:(sub + 1) * tq, :] = _unstack_heads(o, GQA_GROUP).astype(o_ref.dtype)


def _attn_a_ctx_kernel(sink_ref, q_ref, kc_ref, vc_ref, o_ref, *, tq):
    qs = _stack_heads(q_ref[...], GQA_GROUP)
    o = _softmax_pv(_dot_nt(qs, kc_ref[...]), vc_ref[...], jnp.exp2,
                    _sink_column(sink_ref, pl.program_id(1), tq))
    o_ref[...] = _unstack_heads(o, GQA_GROUP).astype(o_ref.dtype)


def _attn_c_lat_kernel(q_ref, kl_ref, kc_ref, vl_ref, vc_ref, o_ref, m_ref, acc_ref, *, tk, n_chunks):
    qs = _stack_heads(q_ref[...], GQA_GROUP)
    s = _dot_nt(qs, kc_ref[...])
    m0 = _rowmax(s)
    m_ref[...] = jnp.broadcast_to(m0, m_ref.shape)
    acc_ref[...] = _dot(jnp.exp2(s - m0).astype(bf16), vc_ref[...])

    _online_softmax_sweep(lambda off: _dot_nt(qs, kl_ref[pl.ds(off, tk), :]), vl_ref, m_ref, None, acc_ref,
                          tk=tk, n_chunks=n_chunks)
    acc = acc_ref[...]
    o = acc[:, :HEAD_DIM] / acc[:, HEAD_DIM:]
    o_ref[...] = _unstack_heads(o, GQA_GROUP).astype(o_ref.dtype)


def _attn_c_ctx_kernel(q_ref, kc_ref, vc_ref, o_ref):
    qs = _stack_heads(q_ref[...], GQA_GROUP)
    o = _softmax_pv(_dot_nt(qs, kc_ref[...]), vc_ref[...], jnp.exp2)
    o_ref[...] = _unstack_heads(o, GQA_GROUP).astype(o_ref.dtype)


def _diff_scores(q, k):
    return jnp.concatenate([_dot_nt(q[:, :HEAD_DIM], k[:, :HEAD_DIM]),
                            _dot_nt(q[:, HEAD_DIM:], k[:, HEAD_DIM:])], axis=0)


def _attn_b_lat_kernel(q_ref, kl_ref, kc_ref, vl_ref, vc_ref, lam_ref, g_ref, o_ref,
                       m_ref, l_ref, acc_ref, *, tk, n_chunks, lam_init):
    q = q_ref[...]
    tq = q.shape[0]
    s = _diff_scores(q, kc_ref[...])
    m0 = _rowmax(s)
    p = jnp.exp2(s - m0)
    m_ref[...] = jnp.broadcast_to(m0, m_ref.shape)
    part = p[:, :LANES]
    for j in range(1, p.shape[1] // LANES):
        part = part + p[:, j * LANES:(j + 1) * LANES]
    l_ref[...] = part
    acc_ref[...] = _dot(p.astype(bf16), vc_ref[...])

    _online_softmax_sweep(lambda off: _diff_scores(q, kl_ref[pl.ds(off, tk), :]), vl_ref, m_ref, l_ref, acc_ref,
                          tk=tk, n_chunks=n_chunks)
    on = acc_ref[...] / _rowsum(l_ref[...])
    o_ref[...] = _diff_finish(on[:tq], on[tq:], lam_ref, g_ref, lam_init).astype(o_ref.dtype)


def _attn_b_ctx_kernel(q_ref, kc_ref, vc_ref, lam_ref, g_ref, o_ref, *, lam_init):
    q = q_ref[...]
    tq = q.shape[0]
    on = _softmax_pv(_diff_scores(q, kc_ref[...]), vc_ref[...], jnp.exp2)
    o_ref[...] = _diff_finish(on[:tq], on[tq:], lam_ref, g_ref, lam_init).astype(o_ref.dtype)


def _attention(qk, v, a_sink, b_lam, b_subln, *, nbatch, seq, ctx, with_ctx, lam_init):
    tq_win = TQ_WINDOW
    tq_gqa = min(TQ_GLOBAL, seq)
    tq_diff = min(TQ_DIFF, seq)
    tq_ctx = ctx
    tk = min(KEY_CHUNK, seq)
    nl_rows = nbatch * seq
    nc_rows = nbatch * ctx
    cb = nl_rows // ctx
    gw = GQA_GROUP * HEAD_DIM
    bw = 2 * HEAD_DIM
    sem = ("arbitrary", "arbitrary", "arbitrary")

    def lat_q(tq, width, off):
        return pl.BlockSpec((tq, width), lambda b, h, qi, *_: (b * (seq // tq) + qi, off // width + h))

    def ctx_q(width, off):
        return pl.BlockSpec((tq_ctx, width), lambda b, h, qi, *_: (cb + b, off // width + h))

    def lat_kv(width, off):
        return pl.BlockSpec((seq, width), lambda b, h, qi, *_: (b, off // width + h))

    def ctx_kv(width, off):
        return pl.BlockSpec((ctx, width), lambda b, h, qi, *_: (cb + b, off // width + h))

    def ctx_out(width):
        return pl.BlockSpec((tq_ctx, width), lambda b, h, qi, *_: (b, h))

    nsub = next(n for n in WINDOW_SUBBLOCKS if seq % (n * tq_win) == 0)
    band = tq_win + 2 * WINDOW
    m_win = GQA_GROUP * tq_win
    dist = ((jnp.arange(m_win, dtype=jnp.int32) % tq_win)[None, :, None]
            - jnp.arange(band, dtype=jnp.int32)[None, None, :]
            + WINDOW * jnp.arange(3, dtype=jnp.int32)[:, None, None])
    rel = jnp.where(jnp.abs(dist) <= WINDOW, 0.0, NEG_INF).astype(f32)
    a_sink = a_sink * LOG2E
    oa = pl.pallas_call(
        functools.partial(_attn_a_lat_kernel, tq=tq_win, nsub=nsub, seq=seq),
        out_shape=jax.ShapeDtypeStruct((nl_rows, A_Q), bf16),
        grid_spec=pltpu.PrefetchScalarGridSpec(
            num_scalar_prefetch=1,
            grid=(nbatch, A_KV_HEADS, seq // (nsub * tq_win)),
            in_specs=[lat_q(nsub * tq_win, gw, QK_AQ), lat_kv(HEAD_DIM, QK_AK), ctx_kv(HEAD_DIM, QK_AK),
                      lat_kv(HEAD_DIM, V_AV), ctx_kv(HEAD_DIM, V_AV),
                      pl.BlockSpec((3, m_win, band), lambda b, h, qi, *_: (0, 0, 0))],
            out_specs=lat_q(nsub * tq_win, gw, 0),
        ),
        compiler_params=_cparams(sem, 48),
        name="attn_window",
    )(a_sink, qk, qk, qk, v, v, rel)
    if with_ctx:
        oa_ctx = pl.pallas_call(
            functools.partial(_attn_a_ctx_kernel, tq=tq_ctx),
            out_shape=jax.ShapeDtypeStruct((nc_rows, A_Q), bf16),
            grid_spec=pltpu.PrefetchScalarGridSpec(
                num_scalar_prefetch=1,
                grid=(nbatch, A_KV_HEADS, 1),
                in_specs=[ctx_q(gw, QK_AQ), ctx_kv(HEAD_DIM, QK_AK), ctx_kv(HEAD_DIM, V_AV)],
                out_specs=ctx_out(gw),
            ),
            compiler_params=_cparams(sem, 32),
            name="attn_window_ctx",
        )(a_sink, qk, qk, v)
        oa = (oa, oa_ctx)

    m_rows = GQA_GROUP * tq_gqa
    oc = pl.pallas_call(
        functools.partial(_attn_c_lat_kernel, tk=tk, n_chunks=seq // tk),
        out_shape=jax.ShapeDtypeStruct((nl_rows, C_Q), bf16),
        grid=(nbatch, C_KV_HEADS, seq // tq_gqa),
        in_specs=[lat_q(tq_gqa, gw, QK_CQ), lat_kv(HEAD_DIM, QK_CK), ctx_kv(HEAD_DIM, QK_CK),
                  lat_kv(2 * HEAD_DIM, V_CV), ctx_kv(2 * HEAD_DIM, V_CV)],
        out_specs=lat_q(tq_gqa, gw, 0),
        scratch_shapes=[pltpu.VMEM((m_rows, LANES), f32), pltpu.VMEM((m_rows, 2 * HEAD_DIM), f32)],
        compiler_params=_cparams(sem, 56),
        name="attn_global",
    )(qk, qk, qk, v, v)
    if with_ctx:
        oc_ctx = pl.pallas_call(
            _attn_c_ctx_kernel,
            out_shape=jax.ShapeDtypeStruct((nc_rows, C_Q), bf16),
            grid=(nbatch, C_KV_HEADS, 1),
            in_specs=[ctx_q(gw, QK_CQ), ctx_kv(HEAD_DIM, QK_CK),
                      pl.BlockSpec((ctx, HEAD_DIM), lambda b, h, qi: (cb + b, V_CV // HEAD_DIM + 2 * h))],
            out_specs=ctx_out(gw),
            compiler_params=_cparams(sem, 32),
            name="attn_global_ctx",
        )(qk, qk, v)
        oc = (oc, oc_ctx)

    lam_spec = pl.BlockSpec((4, HEAD_DIM), lambda b, h, qi: (0, 0))
    g_spec = pl.BlockSpec((1, bw), lambda b, h, qi: (0, 0))
    g2d = b_subln.reshape(1, bw)
    ob = pl.pallas_call(
        functools.partial(_attn_b_lat_kernel, tk=tk, n_chunks=seq // tk, lam_init=lam_init),
        out_shape=jax.ShapeDtypeStruct((nl_rows, B_V), bf16),
        grid=(nbatch, B_HEADS, seq // tq_diff),
        in_specs=[lat_q(tq_diff, bw, QK_BQ), lat_kv(bw, QK_BK), ctx_kv(bw, QK_BK),
                  lat_kv(bw, V_BV), ctx_kv(bw, V_BV), lam_spec, g_spec],
        out_specs=lat_q(tq_diff, bw, 0),
        scratch_shapes=[pltpu.VMEM((2 * tq_diff, LANES), f32), pltpu.VMEM((2 * tq_diff, LANES), f32),
                        pltpu.VMEM((2 * tq_diff, bw), f32)],
        compiler_params=_cparams(sem, 56),
        name="attn_diff",
    )(qk, qk, qk, v, v, b_lam, g2d)
    if with_ctx:
        ob_ctx = pl.pallas_call(
            functools.partial(_attn_b_ctx_kernel, lam_init=lam_init),
            out_shape=jax.ShapeDtypeStruct((nc_rows, B_V), bf16),
            grid=(nbatch, B_HEADS, 1),
            in_specs=[ctx_q(bw, QK_BQ), ctx_kv(bw, QK_BK), ctx_kv(bw, V_BV), lam_spec, g_spec],
            out_specs=ctx_out(bw),
            compiler_params=_cparams(sem, 32),
            name="attn_diff_ctx",
        )(qk, qk, v, b_lam, g2d)
        ob = (ob, ob_ctx)
    return oa, ob, oc


def _merge_kernel(*refs, n_src, n_first):
    o_refs, (w_ref, ga_ref, gb_ref, gc_ref, y_ref) = refs[:3 * n_src], refs[3 * n_src:]
    oa, ob, oc = (_row_block(o_refs[k * n_src:(k + 1) * n_src], n_first) for k in range(3))
    y = (ga_ref[...].astype(f32) * _dot(oa, w_ref[0])
         + gb_ref[...].astype(f32) * _dot(ob, w_ref[1])
         + gc_ref[...].astype(f32) * _dot(oc, w_ref[2]))
    y_ref[...] = y.astype(y_ref.dtype)


def _merge_call(oa, ob, oc, wb, gates, tm):
    nj, _, bwid, tn = wb.shape
    d = nj * tn
    streams = [_row_stream(o, tm) for o in (oa, ob, oc)]
    rows = streams[0][2]
    n_src = len(streams[0][0])
    return pl.pallas_call(
        functools.partial(_merge_kernel, n_src=n_src, n_first=streams[0][0][0].shape[0] // tm),
        out_shape=jax.ShapeDtypeStruct((rows, d), bf16),
        grid=(rows // tm, nj),
        in_specs=[spec for _, specs, _ in streams for spec in specs] + [
            pl.BlockSpec((None, 3, bwid, tn), lambda i, j: (j, 0, 0, 0)),
            pl.BlockSpec((tm, tn), lambda i, j: (i, j)),
            pl.BlockSpec((tm, tn), lambda i, j: (i, nj + j)),
            pl.BlockSpec((tm, tn), lambda i, j: (i, 2 * nj + j))],
        out_specs=pl.BlockSpec((tm, tn), lambda i, j: (i, j)),
        compiler_params=_cparams(("arbitrary", "arbitrary"), 40),
        name="branch_merge",
    )(*[p for parts, _, _ in streams for p in parts], wb, gates, gates, gates)


def _out_proj_kernel(y_ref, w_ref, *refs, n_src, n_first):
    x_refs, (g1_ref, nw_ref, sc_ref, sh_ref, xo_ref, h_ref) = refs[:n_src], refs[n_src:]
    xn = _row_block(x_refs, n_first) + g1_ref[0] * _dot(y_ref[...], w_ref[...])
    xo_ref[...] = xn
    t = xn * lax.rsqrt(jnp.mean(xn * xn, axis=-1, keepdims=True) + EPS) * nw_ref[...]
    h_ref[...] = (t * (1.0 + sc_ref[0]) + sh_ref[0]).astype(h_ref.dtype)


def _out_proj_call(y, w_out, x, modl, nw2, seg_of_block, tm, h_dtype):
    rows, d = y.shape
    parts, x_specs, _ = _row_stream(x, tm)

    def mod_spec(k):
        return pl.BlockSpec((1, 1, d), lambda i: (seg_of_block(i) * 6 + k, 0, 0))

    row_spec = pl.BlockSpec((tm, d), lambda i: (i, 0))
    return pl.pallas_call(
        functools.partial(_out_proj_kernel, n_src=len(parts), n_first=parts[0].shape[0] // tm),
        out_shape=(jax.ShapeDtypeStruct((rows, d), f32), jax.ShapeDtypeStruct((rows, d), h_dtype)),
        grid=(rows // tm,),
        in_specs=[row_spec, pl.BlockSpec((d, d), lambda i: (0, 0))] + x_specs + [
            mod_spec(2), pl.BlockSpec((1, d), lambda i: (0, 0)), mod_spec(4), mod_spec(3)],
        out_specs=(row_spec, row_spec),
        compiler_params=_cparams(("arbitrary",), 48),
        name="out_proj_residual_norm",
    )(y, w_out, *parts, modl, nw2.reshape(1, d), modl, modl)


def _swiglu_stream_step(t, h_ref, w1_ref, w3_ref, w2_ref, acc_ref, a_ref, b_ref):
    for slot in (0, 1):
        @pl.when(lax.rem(t, 2) == slot)
        def _(slot=slot):
            h = h_ref[...]
            a_ref[slot] = _dot(h, w1_ref[...].astype(bf16))
            a = a_ref[1 - slot]
            b = b_ref[1 - slot]
            g = (a * _sigmoid(a) * b).astype(bf16)
            acc_ref[...] += _dot(g, w2_ref[...])
            b_ref[slot] = _dot(h, w3_ref[...].astype(bf16))


def _dense_ffn_kernel(h_ref, w1_ref, w3_ref, w2_ref, x_ref, g2_ref, o_ref, acc_ref, a_ref, b_ref, *, nf):
    t = pl.program_id(0)
    jp = lax.rem(jnp.maximum(t - 1, 0), nf)

    @pl.when(t == 0)
    def _():
        a_ref[1] = jnp.zeros(a_ref.shape[1:], f32)
        b_ref[1] = jnp.zeros(b_ref.shape[1:], f32)

    @pl.when(jp == 0)
    def _():
        acc_ref[...] = jnp.zeros_like(acc_ref)

    _swiglu_stream_step(t, h_ref, w1_ref, w3_ref, w2_ref, acc_ref, a_ref, b_ref)

    @pl.when(jnp.logical_and(jp == nf - 1, t > 0))
    def _():
        o_ref[...] = x_ref[...] + g2_ref[0] * acc_ref[...]


def _dense_ffn_call(h2, w1, w3, w2, x, modl, seg_of_block, tm):
    rows, d = h2.shape
    nf, _, tf = w1.shape
    n_pairs = (rows // tm) * nf

    def cur(t):
        tt = jnp.minimum(t, n_pairs - 1)
        return tt // nf, tt % nf

    def prev(t):
        tp = jnp.maximum(t - 1, 0)
        return tp // nf, tp % nf

    prev_row = pl.BlockSpec((tm, d), lambda t: (prev(t)[0], 0))
    return pl.pallas_call(
        functools.partial(_dense_ffn_kernel, nf=nf),
        out_shape=jax.ShapeDtypeStruct((rows, d), f32),
        grid=(n_pairs + 1,),
        in_specs=[pl.BlockSpec((tm, d), lambda t: (cur(t)[0], 0)),
                  pl.BlockSpec((None, d, tf), lambda t: (cur(t)[1], 0, 0)),
                  pl.BlockSpec((None, d, tf), lambda t: (cur(t)[1], 0, 0)),
                  pl.BlockSpec((tf, d), lambda t: (prev(t)[1], 0)),
                  prev_row,
                  pl.BlockSpec((1, 1, d), lambda t: (seg_of_block(prev(t)[0]) * 6 + 5, 0, 0))],
        out_specs=prev_row,
        scratch_shapes=[pltpu.VMEM((tm, d), f32), pltpu.VMEM((2, tm, tf), f32), pltpu.VMEM((2, tm, tf), f32)],
        compiler_params=_cparams(("arbitrary",), 56),
        name="dense_swiglu",
    )(h2, w1, w3, w2, x, modl)


def _router_kernel(h_ref, w_ref, idx_ref, wt_ref, *, n_experts):
    logits = jnp.dot(h_ref[...], w_ref[...], preferred_element_type=f32,
                     precision=lax.Precision.HIGHEST)
    lane = lax.broadcasted_iota(jnp.int32, logits.shape, 1)
    lg = jnp.where(lane < n_experts, logits, -jnp.inf)
    m1 = _rowmax(lg)
    i1 = jnp.min(jnp.where(lg == m1, lane, LANES), axis=-1, keepdims=True)
    lg2 = jnp.where(lane == i1, -jnp.inf, lg)
    m2 = _rowmax(lg2)
    i2 = jnp.min(jnp.where(lg2 == m2, lane, LANES), axis=-1, keepdims=True)
    e2 = jnp.exp(m2 - m1)
    w1 = 1.0 / (1.0 + e2)
    w2 = e2 / (1.0 + e2)
    idx_ref[...] = jnp.where(lane == 0, i1, jnp.where(lane == 1, i2, 0))
    wt_ref[...] = jnp.where(lane == 0, w1, jnp.where(lane == 1, w2, 0.0))


def _router_call(h2, w_router, tm):
    n, d = h2.shape
    e = w_router.shape[1]
    wp = jnp.zeros((d, LANES), f32).at[:, :e].set(w_router)
    return pl.pallas_call(
        functools.partial(_router_kernel, n_experts=e),
        out_shape=(jax.ShapeDtypeStruct((n, LANES), jnp.int32), jax.ShapeDtypeStruct((n, LANES), f32)),
        grid=(n // tm,),
        in_specs=[pl.BlockSpec((tm, d), lambda i: (i, 0)), pl.BlockSpec((d, LANES), lambda i: (0, 0))],
        out_specs=(pl.BlockSpec((tm, LANES), lambda i: (i, 0)), pl.BlockSpec((tm, LANES), lambda i: (i, 0))),
        compiler_params=_cparams(("arbitrary",), 32),
        name="moe_router_top2",
    )(h2, wp)


def _gather_rows_kernel(src_ref, x_hbm, o_ref, buf, sem, *, chunk):
    i = pl.program_id(0)
    slot = lax.rem(i, 2)

    def row_copy(step, sl, r):
        return pltpu.make_async_copy(x_hbm.at[pl.ds(src_ref[step * chunk + r], 1)],
                                     buf.at[sl, pl.ds(r, 1)], sem.at[sl])

    def issue(step, sl):
        def body(r, c):
            row_copy(step, sl, r).start()
            return c
        lax.fori_loop(0, chunk, body, 0, unroll=DMA_ISSUE_UNROLL)

    def drain(step, sl):
        def body(r, c):
            row_copy(step, sl, r).wait()
            return c
        lax.fori_loop(0, chunk, body, 0, unroll=DMA_ISSUE_UNROLL)

    @pl.when(i == 0)
    def _():
        issue(0, 0)

    @pl.when(i + 1 < pl.num_programs(0))
    def _():
        issue(i + 1, 1 - slot)

    drain(i, slot)
    o_ref[...] = buf[slot].astype(o_ref.dtype)


def _gather_rows_call(x, src, chunk):
    cap = src.shape[0]
    d = x.shape[1]
    return pl.pallas_call(
        functools.partial(_gather_rows_kernel, chunk=chunk),
        out_shape=jax.ShapeDtypeStruct((cap, d), bf16),
        grid_spec=pltpu.PrefetchScalarGridSpec(
            num_scalar_prefetch=1,
            grid=(cap // chunk,),
            in_specs=[pl.BlockSpec(memory_space=pl.ANY)],
            out_specs=pl.BlockSpec((chunk, d), lambda i, s: (i, 0)),
            scratch_shapes=[pltpu.VMEM((2, chunk, d), x.dtype), pltpu.SemaphoreType.DMA((2,))],
        ),
        compiler_params=_cparams(("arbitrary",), 32),
        name="moe_gather_rows",
    )(src, x)


def _expert_kernel(be_ref, nu_ref, x_ref, w1_ref, w3_ref, w2_ref, o_ref, acc_ref, a_ref, b_ref, *, nf):
    t = pl.program_id(0)
    tp = jnp.maximum(t - 1, 0)
    ip = tp // nf
    jp = lax.rem(tp, nf)
    n_live = nu_ref[0] * nf
    live = t <= n_live

    @pl.when(t == 0)
    def _():
        a_ref[1] = jnp.zeros(a_ref.shape[1:], f32)
        b_ref[1] = jnp.zeros(b_ref.shape[1:], f32)

    @pl.when(jnp.logical_and(jp == 0, live))
    def _():
        acc_ref[...] = jnp.zeros_like(acc_ref)

    @pl.when(live)
    def _():
        _swiglu_stream_step(t, x_ref, w1_ref.at[0], w3_ref.at[0], w2_ref.at[0], acc_ref, a_ref, b_ref)

    last = jnp.logical_and(jp == nf - 1, t > 0)

    @pl.when(jnp.logical_and(last, ip < nu_ref[0]))
    def _():
        o_ref[...] = acc_ref[...]

    @pl.when(jnp.logical_and(last, ip >= nu_ref[0]))
    def _():
        o_ref[...] = jnp.zeros_like(o_ref)


def _expert_call(xs, blk_e, n_used, w1, w3, w2, tm, tf):
    cap, d = xs.shape
    nf = w1.shape[2] // tf
    n_pairs = (cap // tm) * nf

    def pair(tt, nu):
        i, j = tt // nf, tt % nf
        return jnp.minimum(i, nu[0] - 1), jnp.where(i < nu[0], j, nf - 1)

    def cur(t, nu):
        return pair(jnp.minimum(t, n_pairs - 1), nu)

    def prev(t, nu):
        return pair(jnp.maximum(t - 1, 0), nu)

    def up_spec():
        return pl.BlockSpec((1, d, tf), lambda t, be, nu: (be[cur(t, nu)[0]], 0, cur(t, nu)[1]))

    return pl.pallas_call(
        functools.partial(_expert_kernel, nf=nf),
        out_shape=jax.ShapeDtypeStruct((cap, d), f32),
        grid_spec=pltpu.PrefetchScalarGridSpec(
            num_scalar_prefetch=2,
            grid=(n_pairs + 1,),
            in_specs=[pl.BlockSpec((tm, d), lambda t, be, nu: (cur(t, nu)[0], 0)),
                      up_spec(), up_spec(),
                      pl.BlockSpec((1, tf, d), lambda t, be, nu: (be[prev(t, nu)[0]], prev(t, nu)[1], 0))],
            out_specs=pl.BlockSpec((tm, d), lambda t, be, nu: (jnp.maximum(t - 1, 0) // nf, 0)),
            scratch_shapes=[pltpu.VMEM((tm, d), f32), pltpu.VMEM((2, tm, tf), f32),
                            pltpu.VMEM((2, tm, tf), f32)],
        ),
        compiler_params=_cparams(("arbitrary",), 56),
        name="moe_expert_swiglu",
    )(blk_e, n_used, xs, w1, w3, w2)


def _combine_kernel(dest_ref, y_hbm, x_ref, g2_ref, wt_ref, o_ref, buf0, buf1, sem, *, tm):
    i = pl.program_id(0)
    slot = lax.rem(i, 2)

    def copies(step, sl, r):
        s = (step * tm + r) * TOP_K
        return (pltpu.make_async_copy(y_hbm.at[pl.ds(dest_ref[s], 1)], buf0.at[sl, pl.ds(r, 1)], sem.at[sl]),
                pltpu.make_async_copy(y_hbm.at[pl.ds(dest_ref[s + 1], 1)], buf1.at[sl, pl.ds(r, 1)],
                                      sem.at[sl]))

    def issue(step, sl):
        def body(r, c):
            c0, c1 = copies(step, sl, r)
            c0.start()
            c1.start()
            return c
        lax.fori_loop(0, tm, body, 0, unroll=DMA_ISSUE_UNROLL)

    def drain(step, sl):
        def body(r, c):
            c0, c1 = copies(step, sl, r)
            c0.wait()
            c1.wait()
            return c
        lax.fori_loop(0, tm, body, 0, unroll=DMA_ISSUE_UNROLL)

    @pl.when(i == 0)
    def _():
        issue(0, 0)

    @pl.when(i + 1 < pl.num_programs(0))
    def _():
        issue(i + 1, 1 - slot)

    drain(i, slot)
    wt = wt_ref[...]
    mix = wt[:, 0:1] * buf0[slot] + wt[:, 1:2] * buf1[slot]
    o_ref[...] = x_ref[...] + g2_ref[0] * mix


def _combine_call(yb, dest, x, modl, wts, seg_of_block, tm):
    n, d = x.shape
    return pl.pallas_call(
        functools.partial(_combine_kernel, tm=tm),
        out_shape=jax.ShapeDtypeStruct((n, d), f32),
        grid_spec=pltpu.PrefetchScalarGridSpec(
            num_scalar_prefetch=1,
            grid=(n // tm,),
            in_specs=[pl.BlockSpec(memory_space=pl.ANY),
                      pl.BlockSpec((tm, d), lambda i, dst: (i, 0)),
                      pl.BlockSpec((1, 1, d), lambda i, dst: (seg_of_block(i) * 6 + 5, 0, 0)),
                      pl.BlockSpec((tm, LANES), lambda i, dst: (i, 0))],
            out_specs=pl.BlockSpec((tm, d), lambda i, dst: (i, 0)),
            scratch_shapes=[pltpu.VMEM((2, tm, d), f32), pltpu.VMEM((2, tm, d), f32),
                            pltpu.SemaphoreType.DMA((2,))],
        ),
        compiler_params=_cparams(("arbitrary",), 48),
        name="moe_combine_residual",
    )(dest, yb, x, modl, wts)


def _moe_call(h2, x, modl, w_router, w1, w3, w2, seg_of_block):
    n, d = h2.shape
    n_exp = w_router.shape[1]
    tm = MOE_ROW_TILE
    idx, wts = _router_call(h2, w_router, NARROW_ROW_TILE)
    flat_e = idx[:, :TOP_K].reshape(-1)
    n_slots = n * TOP_K
    onehot = (flat_e[:, None] == jnp.arange(n_exp, dtype=jnp.int32)[None, :]).astype(jnp.int32)
    csum = jnp.cumsum(onehot, axis=0)
    counts = csum[-1]
    rank = jnp.sum((csum - 1) * onehot, axis=1)
    padded = (counts + tm - 1) // tm * tm
    pend = jnp.cumsum(padded)
    pstart = pend - padded
    dest = (pstart[flat_e] + rank).astype(jnp.int32)
    cap = (n_slots // tm + n_exp) * tm
    src = jnp.zeros((cap,), jnp.int32).at[dest].set(jnp.arange(n_slots, dtype=jnp.int32) // TOP_K)
    n_blocks = cap // tm
    blk_start = jnp.arange(n_blocks, dtype=jnp.int32) * tm
    blk_e = jnp.minimum(jnp.searchsorted(pend, blk_start, side='right'), n_exp - 1).astype(jnp.int32)
    n_used = (pend[-1:] // tm).astype(jnp.int32)

    xs = _gather_rows_call(h2, src, GATHER_ROWS)
    yb = _expert_call(xs, blk_e, n_used, w1, w3, w2, tm, COL_TILE)
    return _combine_call(yb, dest, x, modl, wts, seg_of_block, GATHER_ROWS)


def _rope_tables(nbatch, seq, ctx):
    n_rows = seq // GRID_W
    inv = ROPE_THETA ** (-jnp.arange(ROPE_PAIRS, dtype=f32) / ROPE_PAIRS)
    ang_r = jnp.arange(n_rows, dtype=f32)[:, None] * inv[None, :]
    ang_c = jnp.arange(GRID_W, dtype=f32)[:, None] * inv[None, :]

    def per_token(fn):
        r = jnp.repeat(fn(ang_r), GRID_W, axis=0)
        c = jnp.tile(fn(ang_c), (n_rows, 1))
        return r, c

    (cr, cc), (sr, sc) = per_token(jnp.cos), per_token(jnp.sin)
    cos = jnp.concatenate([cr, cc, cr, cc], axis=1)
    sin = jnp.concatenate([-sr, -sc, sr, sc], axis=1)
    nctx = nbatch * ctx

    def full(tab, fill):
        return jnp.concatenate([jnp.tile(tab, (nbatch, 1)), jnp.full((nctx, HEAD_DIM), fill, f32)], axis=0)

    return full(cos, 1.0), full(sin, 0.0)


def _head_layout(w):
    lead = w.shape[:-1]
    w5 = w.reshape(lead + (w.shape[-1] // HEAD_DIM, 2, 2, ROPE_PAIRS))
    return jnp.swapaxes(w5, -3, -2).reshape(w.shape)


def kernel(x, c, ctx, c_ctx, w_mod, b_mod, norm1, norm2, w_in, qk_gain, a_sink, b_lambda, b_subln,
           w_branch, w_out, dense_w1, dense_w3, dense_w2, moe_router, moe_w1, moe_w3, moe_w2):
    nbatch, seq, d = x.shape
    nctx_len = ctx.shape[1]
    depth = w_mod.shape[0]
    nl = nbatch * seq
    nc = nbatch * nctx_len
    tm = ROW_TILE
    tn_proj = COL_TILE

    def seg_of_block_for(t):
        return lambda i: jnp.minimum(i // (seq // t), nbatch)

    xs = (x.reshape(nl, d), ctx.reshape(nc, d))
    cc = jnp.zeros((8, d), f32).at[:nbatch].set(c).at[nbatch].set(c_ctx)
    mod = _mod_call(cc, w_mod, b_mod)
    cos, sin = _rope_tables(nbatch, seq, nctx_len)
    ones_mat = jnp.ones((HEAD_DIM, HEAD_DIM), bf16)
    scale = HEAD_DIM ** -0.5
    ones_cols = jnp.ones((HEAD_DIM,), f32)
    zero_w = jnp.zeros((d, HEAD_DIM), bf16)

    for l in range(depth):
        with_ctx = l < depth - 1
        lam_init = 0.8 - 0.6 * math.exp(-0.3 * l)
        rows = nl + nc if with_ctx else nl
        modl = mod[l].reshape(8 * 6, 1, d)

        wl = w_in[l]
        o = 0
        parts = {}
        for name, width in (("aq", A_Q), ("ak", A_KV), ("av", A_KV), ("bq", B_QK), ("bk", B_QK),
                            ("bv", B_V), ("cq", C_Q), ("ck", C_KV), ("cv", C_KV), ("g", 3 * d)):
            parts[name] = wl[:, o:o + width].astype(bf16)
            o += width
        w_qk = _head_layout(jnp.concatenate([parts[k] for k in ("cq", "aq", "bq", "bk", "ck", "ak")], axis=1))
        cv = parts["cv"]
        cv_cols = []
        for hd in range(C_KV_HEADS):
            cv_cols += [cv[:, hd * HEAD_DIM:(hd + 1) * HEAD_DIM], zero_w]
        w_v = jnp.concatenate([parts["bv"]] + cv_cols + [parts["av"]], axis=1)
        zeros_bv = jnp.zeros((B_V,), f32)
        zeros_h = jnp.zeros((HEAD_DIM,), f32)
        bias_v = jnp.concatenate([zeros_bv] + [zeros_h, ones_cols] * C_KV_HEADS
                                 + [jnp.zeros((A_KV,), f32)]).reshape(1, V_COLS)
        w_g = parts["g"]
        g = qk_gain[l]
        gain = _head_layout(jnp.concatenate(
            [jnp.tile(g[4] * (scale * LOG2E), C_HEADS), jnp.tile(g[0] * (scale * LOG2E), A_HEADS),
             jnp.tile(g[2] * (scale * LOG2E), 2 * B_HEADS), jnp.tile(g[3], 2 * B_HEADS),
             jnp.tile(g[5], C_KV_HEADS), jnp.tile(g[1], A_KV_HEADS)]).reshape(1, QK_COLS))

        h = _norm_call(xs, norm1[l], modl, 1, 0, seg_of_block_for(tm), tm, bf16)
        tp = _row_tile(nl + nc)
        tab_spec = ((tp, HEAD_DIM), lambda i, j: (i, 0))
        qk = _deferred_mm_call(_qk_epilogue, h, _tile_major(w_qk, tn_proj), (gain, cos, sin, ones_mat),
                               [((1, tn_proj), lambda i, j: (0, j)), tab_spec, tab_spec,
                                ((HEAD_DIM, HEAD_DIM), lambda i, j: (0, 0))],
                               nl + nc, tp, "qk_proj_norm_rope")
        v = _proj_call(_bias_proj_kernel, h, w_v, (bias_v,), [pl.BlockSpec((1, V_COLS), lambda i, j: (0, 0))],
                       nl + nc, tp, V_COLS, "v_proj")
        gates = _deferred_mm_call(_gate_epilogue, h, _tile_major(w_g, GATE_COL_TILE), (), [], rows,
                                  _row_tile(rows), "gate_proj")

        oa, ob, oc = _attention(qk, v, a_sink[l], b_lambda[l], b_subln[l], nbatch=nbatch, seq=seq,
                                ctx=nctx_len, with_ctx=with_ctx, lam_init=lam_init)
        y = _merge_call(oa, ob, oc, jnp.swapaxes(_tile_major(w_branch[l].astype(bf16), tn_proj), 0, 1),
                        gates, tm)
        moe_layer = l % 2 == 1
        xn, h2 = _out_proj_call(y, w_out[l].astype(bf16), xs, modl, norm2[l], seg_of_block_for(NARROW_ROW_TILE), NARROW_ROW_TILE,
                                f32 if moe_layer else bf16)
        i = l // 2
        if not moe_layer:
            xs = _dense_ffn_call(h2, _tile_major(dense_w1[i].astype(bf16), tn_proj),
                                 _tile_major(dense_w3[i].astype(bf16), tn_proj),
                                 dense_w2[i].astype(bf16), xn, modl, seg_of_block_for(tm), tm)
        else:
            xs = _moe_call(h2, xn, modl, moe_router[i], moe_w1[i], moe_w3[i].astype(bf16),
                           moe_w2[i].astype(bf16), seg_of_block_for(GATHER_ROWS))
    return xs[:nl].reshape(nbatch, seq, d)
```
